```python
import jax, jax.numpy as jnp
from jax import lax
import numpy as np

D_MODEL = 1024
BATCH = 4
SEQ = 4096
DEPTH = 2
DEC_BATCH = 32
DEC_SEQ = 1
PAST_LEN = 8192
PAGE_SIZE = 128

N_EVEN = (DEPTH + 1) // 2
N_ODD = DEPTH // 2
H_A = 8
DH_A = 64
W_A = H_A * DH_A
MOBA_BLOCK = 256
MOBA_TOPK = 3
MOBA_QCHUNK = 32
H_B = 4
DH_B = 128
W_B = H_B * DH_B
MLSTM_CHUNK = 64
MLSTM_CONV = 4
H_C = 16
DH_C = 64
W_C = H_C * DH_C
FOX_QBLOCK = 128
D_FF = 2816
FFN_CONV = 3
D_IN_EVEN = 3 * W_A + 4 * W_B + 2 * H_B
D_IN_ODD = 3 * W_C + H_C
RMS_EPS = 1e-6

kernel_name = 'hybrid_moba_mlstm_fox_convffn_step'


def rmsnorm(x, w):
    xf = x.astype(jnp.float32)
    xf = xf * lax.rsqrt(jnp.mean(xf * xf, axis=-1, keepdims=True) + RMS_EPS)
    return xf.astype(x.dtype) * w


def causal_dwconv(x, buf, w, b):
    width = w.shape[0]
    s = x.shape[1]
    xp = jnp.concatenate([buf.astype(x.dtype), x], axis=1)
    y = xp[:, 0:s] * w[0]
    for j in range(1, width):
        y = y + xp[:, j:j + s] * w[j]
    return y + b, xp[:, xp.shape[1] - (width - 1):]


def split_even(z):
    n, s, _ = z.shape
    o1 = 3 * W_A
    o2 = o1 + 2 * W_B
    o3 = o2 + W_B
    o4 = o3 + W_B
    o5 = o4 + H_B
    qa = z[..., 0:W_A].reshape(n, s, H_A, DH_A)
    ka = z[..., W_A:2 * W_A].reshape(n, s, H_A, DH_A)
    va = z[..., 2 * W_A:o1].reshape(n, s, H_A, DH_A)
    return qa, ka, va, z[..., o1:o2], z[..., o2:o3], z[..., o3:o4], z[..., o4:o5], z[..., o5:]


def split_odd(z, b_f):
    n, s, _ = z.shape
    q = z[..., 0:W_C].reshape(n, s, H_C, DH_C)
    k = z[..., W_C:2 * W_C].reshape(n, s, H_C, DH_C)
    v = z[..., 2 * W_C:3 * W_C].reshape(n, s, H_C, DH_C)
    lf = jax.nn.log_sigmoid((z[..., 3 * W_C:] + b_f).astype(jnp.float32))
    return q, k, v, lf


def moba_prompt(q, k, v):
    n, s, h, d = q.shape
    f32 = jnp.float32
    scale = d ** -0.5
    nb_full = s // MOBA_BLOCK
    nbc = max(nb_full, MOBA_TOPK)
    n_blk = max(nbc, -(-s // MOBA_BLOCK))
    pad = n_blk * MOBA_BLOCK - s
    kp = jnp.pad(k, ((0, 0), (0, pad), (0, 0), (0, 0)))
    vp = jnp.pad(v, ((0, 0), (0, pad), (0, 0), (0, 0)))
    kb = kp.reshape(n, n_blk, MOBA_BLOCK, h, d)
    kmean = jnp.mean(kb[:, :nb_full].astype(f32), axis=2)
    kmean = jnp.pad(kmean, ((0, 0), (0, nbc - nb_full), (0, 0), (0, 0)))
    kbh = kb.transpose(0, 3, 1, 2, 4)
    vbh = vp.reshape(n, n_blk, MOBA_BLOCK, h, d).transpose(0, 3, 1, 2, 4)
    bi = jnp.arange(n)[:, None, None, None]
    hi = jnp.arange(h)[None, None, :, None]

    def one_chunk(ci):
        t0 = ci * MOBA_QCHUNK
        qc = lax.dynamic_slice_in_dim(q, t0, MOBA_QCHUNK, axis=1)
        qpos = t0 + jnp.arange(MOBA_QCHUNK)
        blk = t0 // MOBA_BLOCK
        gate = jnp.einsum('nqhd,njhd->nqhj', qc.astype(f32), kmean)
        gate = jnp.where(jnp.arange(nbc) < blk, gate, -jnp.inf)
        _, idx = lax.top_k(gate, MOBA_TOPK)
        valid = idx < blk
        ks = kbh[bi, hi, idx]
        vs = vbh[bi, hi, idx]
        l_sel = jnp.einsum('nqhd,nqhjkd->nqhjk', qc, ks).astype(f32) * scale
        l_sel = jnp.where(valid[..., None], l_sel, -jnp.inf).reshape(n, MOBA_QCHUNK, h, MOBA_TOPK * MOBA_BLOCK)
        ko = lax.dynamic_slice_in_dim(kp, blk * MOBA_BLOCK, MOBA_BLOCK, axis=1)
        vo = lax.dynamic_slice_in_dim(vp, blk * MOBA_BLOCK, MOBA_BLOCK, axis=1)
        kpos = blk * MOBA_BLOCK + jnp.arange(MOBA_BLOCK)
        l_own = jnp.einsum('nqhd,nkhd->nqhk', qc, ko).astype(f32) * scale
        l_own = jnp.where((kpos[None, :] <= qpos[:, None])[None, :, None, :], l_own, -jnp.inf)
        p = jax.nn.softmax(jnp.concatenate([l_sel, l_own], axis=-1), axis=-1).astype(v.dtype)
        p_sel = p[..., :MOBA_TOPK * MOBA_BLOCK].reshape(n, MOBA_QCHUNK, h, MOBA_TOPK, MOBA_BLOCK)
        p_own = p[..., MOBA_TOPK * MOBA_BLOCK:]
        return jnp.einsum('nqhjk,nqhjkd->nqhd', p_sel, vs) + jnp.einsum('nqhk,nkhd->nqhd', p_own, vo)

    out = lax.map(one_chunk, jnp.arange(s // MOBA_QCHUNK))
    return jnp.moveaxis(out, 0, 1).reshape(n, s, h, d)


def moba_sample(q, k_new, v_new, pool_k, pool_v, page_table):
    n, s, h, d = q.shape
    f32 = jnp.float32
    scale = d ** -0.5
    n_pages = page_table.shape[1]
    past = n_pages * PAGE_SIZE
    ppb = MOBA_BLOCK // PAGE_SIZE
    nbp = past // MOBA_BLOCK
    nbc = max(nbp, MOBA_TOPK)
    page_means = lax.map(lambda phys: jnp.mean(pool_k[phys].astype(f32), axis=1),
                         page_table[:, :nbp * ppb].T)
    kmean = jnp.mean(jnp.moveaxis(page_means, 0, 1).reshape(n, nbp, ppb, h, d), axis=2)
    kmean = jnp.pad(kmean, ((0, 0), (0, nbc - nbp), (0, 0), (0, 0)))
    qpos = past + jnp.arange(s)
    qblk = qpos // MOBA_BLOCK
    gate = jnp.einsum('nqhd,njhd->nqhj', q.astype(f32), kmean)
    gate = jnp.where((jnp.arange(nbc)[None, :] < qblk[:, None])[None, :, None, :], gate, -jnp.inf)
    _, idx = lax.top_k(gate, MOBA_TOPK)
    valid = idx < qblk[None, :, None, None]
    lpage = jnp.minimum(idx[..., None] * ppb + jnp.arange(ppb), n_pages - 1)
    phys = page_table[jnp.arange(n)[:, None, None, None, None], lpage]
    hi = jnp.arange(h)[None, None, :, None, None]
    ks = pool_k[phys, :, hi].reshape(n, s, h, MOBA_TOPK, MOBA_BLOCK, d)
    vs = pool_v[phys, :, hi].reshape(n, s, h, MOBA_TOPK, MOBA_BLOCK, d)
    l_sel = jnp.einsum('nqhd,nqhjkd->nqhjk', q, ks).astype(f32) * scale
    l_sel = jnp.where(valid[..., None], l_sel, -jnp.inf).reshape(n, s, h, MOBA_TOPK * MOBA_BLOCK)
    last_phys = page_table[:, n_pages - 1]
    k_last = pool_k[last_phys]
    v_last = pool_v[last_phys]
    last_pos = past - PAGE_SIZE + jnp.arange(PAGE_SIZE)
    mask_last = (last_pos[None, :] // MOBA_BLOCK) == qblk[:, None]
    l_last = jnp.einsum('nqhd,nkhd->nqhk', q, k_last).astype(f32) * scale
    l_last = jnp.where(mask_last[None, :, None, :], l_last, -jnp.inf)
    mask_new = (qpos[None, :] <= qpos[:, None]) & (qblk[None, :] == qblk[:, None])
    l_new = jnp.einsum('nqhd,nkhd->nqhk', q, k_new).astype(f32) * scale
    l_new = jnp.where(mask_new[None, :, None, :], l_new, -jnp.inf)
    p = jax.nn.softmax(jnp.concatenate([l_sel, l_last, l_new], axis=-1), axis=-1).astype(v_new.dtype)
    n_sel = MOBA_TOPK * MOBA_BLOCK
    p_sel = p[..., :n_sel].reshape(n, s, h, MOBA_TOPK, MOBA_BLOCK)
    p_last = p[..., n_sel:n_sel + PAGE_SIZE]
    p_new = p[..., n_sel + PAGE_SIZE:]
    return (jnp.einsum('nqhjk,nqhjkd->nqhd', p_sel, vs)
            + jnp.einsum('nqhk,nkhd->nqhd', p_last, v_last)
            + jnp.einsum('nqhk,nkhd->nqhd', p_new, v_new))


def mlstm_inputs(qk_pre, v_pre, i_pre, f_pre, conv_buf, conv_w, conv_b, b_i, b_f):
    n, s, _ = qk_pre.shape
    qk, new_buf = causal_dwconv(qk_pre, conv_buf, conv_w, conv_b)
    qk = jax.nn.silu(qk)
    q = qk[..., :W_B].reshape(n, s, H_B, DH_B)
    k = qk[..., W_B:].reshape(n, s, H_B, DH_B) * (DH_B ** -0.5)
    v = v_pre.reshape(n, s, H_B, DH_B)
    ig = (i_pre + b_i).astype(jnp.float32)
    lf = jax.nn.log_sigmoid((f_pre + b_f).astype(jnp.float32))
    return q, k, v, ig, lf, new_buf


def mlstm_chunk(carry, inp):
    c, nv, m = carry
    q, k, v, ig, lf = inp
    f32 = jnp.float32
    qf, kf, vf = q.astype(f32), k.astype(f32), v.astype(f32)
    L = q.shape[1]
    a = jnp.cumsum(lf, axis=1)
    causal = jnp.tril(jnp.ones((L, L), dtype=bool))
    dmat = a[:, :, None, :] - a[:, None, :, :] + ig[:, None, :, :]
    dmat = jnp.where(causal[None, :, :, None], dmat, -jnp.inf)
    inter = a + m[:, None, :]
    m_row = jnp.maximum(jnp.max(dmat, axis=2), inter)
    w_inter = jnp.exp(inter - m_row)
    s_qk = jnp.einsum('nthd,nshd->ntsh', qf, kf) * jnp.exp(dmat - m_row[:, :, None, :])
    num = jnp.einsum('ntsh,nshe->nthe', s_qk, vf) + w_inter[..., None] * jnp.einsum('nthd,nhde->nthe', qf, c)
    den = jnp.sum(s_qk, axis=2) + w_inter * jnp.einsum('nthd,nhd->nth', qf, nv)
    h = num / jnp.maximum(jnp.abs(den), jnp.exp(-m_row))[..., None]
    a_last = a[:, -1, :]
    g = a_last[:, None, :] - a + ig
    m_new = jnp.maximum(a_last + m, jnp.max(g, axis=1))
    w_s = jnp.exp(g - m_new[:, None, :])
    w_c = jnp.exp(a_last + m - m_new)
    c_new = w_c[..., None, None] * c + jnp.einsum('nsh,nshd,nshe->nhde', w_s, kf, vf)
    n_new = w_c[..., None] * nv + jnp.einsum('nsh,nshd->nhd', w_s, kf)
    return (c_new, n_new, m_new), h


def mlstm_prompt(q, k, v, ig, lf):
    n, s, h, d = q.shape
    f32 = jnp.float32
    nc = s // MLSTM_CHUNK

    def chunks(t):
        return jnp.moveaxis(t.reshape((n, nc, MLSTM_CHUNK) + t.shape[2:]), 1, 0)

    init = (jnp.zeros((n, h, d, d), f32), jnp.zeros((n, h, d), f32), jnp.zeros((n, h), f32))
    state, hs = lax.scan(mlstm_chunk, init, (chunks(q), chunks(k), chunks(v), chunks(ig), chunks(lf)))
    return jnp.moveaxis(hs, 0, 1).reshape(n, s, h, d), state


def even_merge(att, hm, ob, norm_w, w_out):
    n, s = ob.shape[:2]
    hm = rmsnorm(hm, norm_w.reshape(H_B, DH_B)).astype(ob.dtype).reshape(n, s, W_B)
    return jnp.concatenate([att.reshape(n, s, W_A), jax.nn.sigmoid(ob) * hm], axis=-1) @ w_out


def fox_prompt(q, k, v, lf):
    n, s, h, d = q.shape
    scale = d ** -0.5
    cum = jnp.cumsum(lf, axis=1).transpose(0, 2, 1)
    kpos = jnp.arange(s)

    def one_block(bi):
        t0 = bi * FOX_QBLOCK
        qb = lax.dynamic_slice_in_dim(q, t0, FOX_QBLOCK, axis=1)
        cq = lax.dynamic_slice_in_dim(cum, t0, FOX_QBLOCK, axis=2)
        logits = (jnp.einsum('nqhd,nkhd->nhqk', qb, k).astype(jnp.float32) * scale
                  + (cq[..., None] - cum[:, :, None, :]))
        qpos = t0 + jnp.arange(FOX_QBLOCK)
        logits = jnp.where(kpos[None, :] <= qpos[:, None], logits, -jnp.inf)
        p = jax.nn.softmax(logits, axis=-1)
        return jnp.einsum('nhqk,nkhd->nqhd', p.astype(v.dtype), v)

    out = lax.map(one_block, jnp.arange(s // FOX_QBLOCK))
    return jnp.moveaxis(out, 0, 1).reshape(n, s, h, d)


def fox_sample(q, k_new, v_new, lf_new, pool_k, pool_v, pool_lf, page_table):
    n, s, h, d = q.shape
    f32 = jnp.float32
    scale = d ** -0.5
    n_pages = page_table.shape[1]
    lf_past = pool_lf[page_table].astype(f32).reshape(n, n_pages * PAGE_SIZE, h)
    cum_past = jnp.cumsum(lf_past, axis=1)
    cum_new = cum_past[:, -1:, :] + jnp.cumsum(lf_new, axis=1)
    cq = cum_new.transpose(0, 2, 1)
    cum_blocks = cum_past.reshape(n, n_pages, PAGE_SIZE, h).transpose(1, 0, 3, 2)

    def step(carry, xs):
        m, l, acc = carry
        phys, cb = xs
        kb = pool_k[phys]
        vb = pool_v[phys]
        logits = (jnp.einsum('nqhd,nkhd->nhqk', q, kb).astype(f32) * scale
                  + (cq[..., None] - cb[:, :, None, :]))
        m_new = jnp.maximum(m, jnp.max(logits, axis=-1))
        corr = jnp.exp(m - m_new)
        p = jnp.exp(logits - m_new[..., None])
        l = l * corr + jnp.sum(p, axis=-1)
        acc = acc * corr[..., None] + jnp.einsum('nhqk,nkhd->nhqd', p, vb.astype(f32))
        return (m_new, l, acc), None

    init = (jnp.full((n, h, s), -jnp.inf, f32), jnp.zeros((n, h, s), f32), jnp.zeros((n, h, s, d), f32))
    (m, l, acc), _ = lax.scan(step, init, (page_table.T, cum_blocks))
    logits = (jnp.einsum('nqhd,nkhd->nhqk', q, k_new).astype(f32) * scale
              + (cq[..., None] - cq[:, :, None, :]))
    logits = jnp.where(jnp.tril(jnp.ones((s, s), dtype=bool)), logits, -jnp.inf)
    m_f = jnp.maximum(m, jnp.max(logits, axis=-1))
    corr = jnp.exp(m - m_f)
    p = jnp.exp(logits - m_f[..., None])
    l = l * corr + jnp.sum(p, axis=-1)
    acc = acc * corr[..., None] + jnp.einsum('nhqk,nkhd->nhqd', p, v_new.astype(f32))
    return (acc / l[..., None]).transpose(0, 2, 1, 3).astype(q.dtype)


def conv_ffn(h, buf, w_in, conv_w, conv_b, w_out):
    u, new_buf = causal_dwconv(h @ w_in, buf, conv_w, conv_b)
    return (jax.nn.gelu(u[..., :D_FF], approximate=True) * u[..., D_FF:]) @ w_out, new_buf


def setup_inputs(seed: int = 0) -> dict:
    key = jax.random.key(seed)
    ks = jax.random.split(key, 32)
    f32 = jnp.float32
    n_pages = PAST_LEN // PAGE_SIZE
    n_used = DEC_BATCH * n_pages
    n_pool = n_used + max(1, n_used // 4)

    def nrm(k, shape, scale=1.0):
        return scale * jax.random.normal(k, shape, f32)

    page_table = jax.random.permutation(ks[12], n_pool)[:n_used].reshape(DEC_BATCH, n_pages).astype(jnp.int32)
    return {
        'x_prompt': nrm(ks[0], (BATCH, SEQ, D_MODEL)),
        'x_sample': nrm(ks[1], (DEC_BATCH, DEC_SEQ, D_MODEL)),
        'cache_moba_k': nrm(ks[2], (N_EVEN, n_pool, PAGE_SIZE, H_A, DH_A)),
        'cache_moba_v': nrm(ks[3], (N_EVEN, n_pool, PAGE_SIZE, H_A, DH_A)),
        'state_mlstm_c': nrm(ks[4], (N_EVEN, DEC_BATCH, H_B, DH_B, DH_B), 0.05),
        'state_mlstm_n': nrm(ks[5], (N_EVEN, DEC_BATCH, H_B, DH_B), 0.5),
        'state_mlstm_m': nrm(ks[6], (N_EVEN, DEC_BATCH, H_B)),
        'state_mlstm_conv': nrm(ks[7], (N_EVEN, DEC_BATCH, MLSTM_CONV - 1, 2 * W_B)),
        'cache_fox_k': nrm(ks[8], (N_ODD, n_pool, PAGE_SIZE, H_C, DH_C)),
        'cache_fox_v': nrm(ks[9], (N_ODD, n_pool, PAGE_SIZE, H_C, DH_C)),
        'cache_fox_logf': jax.nn.log_sigmoid(3.0 + nrm(ks[10], (N_ODD, n_pool, PAGE_SIZE, H_C))),
        'state_ffn_conv': nrm(ks[11], (DEPTH, DEC_BATCH, FFN_CONV - 1, 2 * D_FF)),
        'page_table': page_table,
        'norm_w': 1.0 + nrm(ks[13], (DEPTH, 4, D_MODEL), 0.1),
        'even_w_in': nrm(ks[14], (N_EVEN, D_MODEL, D_IN_EVEN), D_MODEL ** -0.5),
        'mlstm_conv_w': nrm(ks[15], (N_EVEN, MLSTM_CONV, 2 * W_B), MLSTM_CONV ** -0.5),
        'mlstm_conv_b': nrm(ks[16], (N_EVEN, 2 * W_B), 0.01),
        'mlstm_b_i': nrm(ks[17], (N_EVEN, H_B), 0.1),
        'mlstm_b_f': 3.0 + nrm(ks[18], (N_EVEN, H_B), 0.5),
        'mlstm_norm_w': 1.0 + nrm(ks[19], (N_EVEN, W_B), 0.1),
        'even_w_out': nrm(ks[20], (N_EVEN, W_A + W_B, D_MODEL), (W_A + W_B) ** -0.5),
        'fox_w_in': nrm(ks[21], (N_ODD, D_MODEL, D_IN_ODD), D_MODEL ** -0.5),
        'fox_b_f': 3.0 + nrm(ks[22], (N_ODD, H_C), 0.5),
        'fox_w_out': nrm(ks[23], (N_ODD, W_C, D_MODEL), W_C ** -0.5),
        'ffn_w_in': nrm(ks[24], (DEPTH, D_MODEL, 2 * D_FF), D_MODEL ** -0.5),
        'ffn_conv_w': nrm(ks[25], (DEPTH, FFN_CONV, 2 * D_FF), FFN_CONV ** -0.5),
        'ffn_conv_b': nrm(ks[26], (DEPTH, 2 * D_FF), 0.01),
        'ffn_w_out': nrm(ks[27], (DEPTH, D_FF, D_MODEL), D_FF ** -0.5),
    }


def reference(x_prompt, x_sample, cache_moba_k, cache_moba_v, state_mlstm_c, state_mlstm_n, state_mlstm_m,
              state_mlstm_conv, cache_fox_k, cache_fox_v, cache_fox_logf, state_ffn_conv, page_table,
              norm_w, even_w_in, mlstm_conv_w, mlstm_conv_b, mlstm_b_i, mlstm_b_f, mlstm_norm_w, even_w_out,
              fox_w_in, fox_b_f, fox_w_out, ffn_w_in, ffn_conv_w, ffn_conv_b, ffn_w_out):
    f32 = jnp.float32
    bp, sp = x_prompt.shape[0], x_prompt.shape[1]
    bs, ss = x_sample.shape[0], x_sample.shape[1]
    yp, ys = x_prompt, x_sample
    mk_p, mv_p, mk_s, mv_s = [], [], [], []
    mc_p, mn_p, mm_p, mconv_p = [], [], [], []
    mc_s, mn_s, mm_s, mconv_s = [], [], [], []
    fk_p, fv_p, fl_p, fk_s, fv_s, fl_s = [], [], [], [], [], []
    ff_p, ff_s = [], []
    for layer in range(DEPTH):
        hp = rmsnorm(yp, norm_w[layer, 0])
        hs = rmsnorm(ys, norm_w[layer, 0])
        if layer % 2 == 0:
            e = layer // 2
            qa, ka, va, qkb, vb, ob, ib, fb = split_even(hp @ even_w_in[e])
            att = moba_prompt(qa, ka, va)
            q, k, v, ig, lf, cbuf = mlstm_inputs(qkb, vb, ib, fb, jnp.zeros((bp, MLSTM_CONV - 1, 2 * W_B), hp.dtype),
                                                 mlstm_conv_w[e], mlstm_conv_b[e], mlstm_b_i[e], mlstm_b_f[e])
            hm, (cc, nn, mm) = mlstm_prompt(q, k, v, ig, lf)
            mix_p = even_merge(att, hm, ob, mlstm_norm_w[e], even_w_out[e])
            mk_p.append(ka); mv_p.append(va)
            mc_p.append(cc); mn_p.append(nn); mm_p.append(mm); mconv_p.append(cbuf)
            qa, ka, va, qkb, vb, ob, ib, fb = split_even(hs @ even_w_in[e])
            att = moba_sample(qa, ka, va, cache_moba_k[e], cache_moba_v[e], page_table)
            q, k, v, ig, lf, cbuf = mlstm_inputs(qkb, vb, ib, fb, state_mlstm_conv[e],
                                                 mlstm_conv_w[e], mlstm_conv_b[e], mlstm_b_i[e], mlstm_b_f[e])
            (cc, nn, mm), hm = mlstm_chunk((state_mlstm_c[e].astype(f32), state_mlstm_n[e].astype(f32),
                                            state_mlstm_m[e].astype(f32)), (q, k, v, ig, lf))
            mix_s = even_merge(att, hm, ob, mlstm_norm_w[e], even_w_out[e])
            mk_s.append(ka); mv_s.append(va)
            mc_s.append(cc); mn_s.append(nn); mm_s.append(mm); mconv_s.append(cbuf)
        else:
            o = layer // 2
            q, k, v, lf = split_odd(hp @ fox_w_in[o], fox_b_f[o])
            mix_p = fox_prompt(q, k, v, lf).reshape(bp, sp, W_C) @ fox_w_out[o]
            fk_p.append(k); fv_p.append(v); fl_p.append(lf)
            q, k, v, lf = split_odd(hs @ fox_w_in[o], fox_b_f[o])
            att = fox_sample(q, k, v, lf, cache_fox_k[o], cache_fox_v[o], cache_fox_logf[o], page_table)
            mix_s = att.reshape(bs, ss, W_C) @ fox_w_out[o]
            fk_s.append(k); fv_s.append(v); fl_s.append(lf)
        yp = yp + rmsnorm(mix_p, norm_w[layer, 1])
        ys = ys + rmsnorm(mix_s, norm_w[layer, 1])
        fp, fbuf_p = conv_ffn(rmsnorm(yp, norm_w[layer, 2]), jnp.zeros((bp, FFN_CONV - 1, 2 * D_FF), yp.dtype),
                              ffn_w_in[layer], ffn_conv_w[layer], ffn_conv_b[layer], ffn_w_out[layer])
        fs, fbuf_s = conv_ffn(rmsnorm(ys, norm_w[layer, 2]), state_ffn_conv[layer],
                              ffn_w_in[layer], ffn_conv_w[layer], ffn_conv_b[layer], ffn_w_out[layer])
        yp = yp + rmsnorm(fp, norm_w[layer, 3])
        ys = ys + rmsnorm(fs, norm_w[layer, 3])
        ff_p.append(fbuf_p); ff_s.append(fbuf_s)
    moba_k_prompt = jnp.stack(mk_p)
    moba_v_prompt = jnp.stack(mv_p)
    moba_k_sample = jnp.stack(mk_s)
    moba_v_sample = jnp.stack(mv_s)
    mlstm_c_prompt = jnp.stack(mc_p)
    mlstm_n_prompt = jnp.stack(mn_p)
    mlstm_m_prompt = jnp.stack(mm_p)
    mlstm_conv_prompt = jnp.stack(mconv_p)
    mlstm_c_sample = jnp.stack(mc_s)
    mlstm_n_sample = jnp.stack(mn_s)
    mlstm_m_sample = jnp.stack(mm_s)
    mlstm_conv_sample = jnp.stack(mconv_s)
    fox_k_prompt = jnp.stack(fk_p)
    fox_v_prompt = jnp.stack(fv_p)
    fox_logf_prompt = jnp.stack(fl_p)
    fox_k_sample = jnp.stack(fk_s)
    fox_v_sample = jnp.stack(fv_s)
    fox_logf_sample = jnp.stack(fl_s)
    ffn_conv_prompt = jnp.stack(ff_p)
    ffn_conv_sample = jnp.stack(ff_s)
    return (yp, ys, moba_k_prompt, moba_v_prompt, moba_k_sample, moba_v_sample,
            mlstm_c_prompt, mlstm_n_prompt, mlstm_m_prompt, mlstm_conv_prompt,
            mlstm_c_sample, mlstm_n_sample, mlstm_m_sample, mlstm_conv_sample,
            fox_k_prompt, fox_v_prompt, fox_logf_prompt, fox_k_sample, fox_v_sample, fox_logf_sample,
            ffn_conv_prompt, ffn_conv_sample)
```

```python
import functools

import jax
import jax.numpy as jnp
from jax import lax
from jax.experimental import pallas as pl
from jax.experimental.pallas import tpu as pltpu

F32 = jnp.float32
BF16 = jnp.bfloat16
NEG_INF = float("-inf")

D_MODEL = 1024
PAGE_SIZE = 128
H_A, DH_A = 8, 64
W_A = H_A * DH_A
MOBA_BLOCK = 256
MOBA_TOPK = 3
H_B, DH_B = 4, 128
W_B = H_B * DH_B
MLSTM_CONV = 4
H_C, DH_C = 16, 64
W_C = H_C * DH_C
D_FF = 2816
FFN_CONV = 3
RMS_EPS = 1e-6

LANES = 128
SUBLANES = 8
VMEM_LIMIT = 56 * 1024 * 1024

ROW_TILE = 256
ATT_TILE = 256
MLSTM_TILE = 256
FFN_ROWS = 512
FFN_COLS = 256
PAGES_PER_STEP = 8


def _params(*sem):
    return pltpu.CompilerParams(dimension_semantics=sem, vmem_limit_bytes=VMEM_LIMIT)


def _rms(x, w):
    return x * lax.rsqrt(jnp.mean(x * x, axis=-1, keepdims=True) + RMS_EPS) * w


def _sigmoid(x):
    return 1.0 / (1.0 + jnp.exp(-x))


def _log_sigmoid(x):
    return -(jnp.maximum(-x, 0.0) + jnp.log1p(jnp.exp(-jnp.abs(x))))


def _gelu_tanh(x):
    c = 0.7978845608028654
    return 0.5 * x * (1.0 + jnp.tanh(c * (x + 0.044715 * (x * x * x))))


def _dot(a, b):
    return jnp.dot(a, b, preferred_element_type=F32)


def _dot_nt(a, b):
    return lax.dot_general(a, b, (((1,), (1,)), ((), ())), preferred_element_type=F32)


def _split_bf16(x):
    hi = x.astype(BF16)
    lo = (x - hi.astype(F32)).astype(BF16)
    return hi, lo


def _pad_lanes(a, width=LANES):
    return jnp.pad(a, ((0, 0),) * (a.ndim - 1) + ((0, width - a.shape[-1]),))


def _rms_proj_kernel(*refs, segs, n_hi, gate_lf):
    x_ref, nw_ref, w_ref = refs[:3]
    pos = 3
    wlo_ref = None
    if n_hi:
        wlo_ref = refs[pos]
        pos += 1
    gb_ref = refs[pos]
    out_refs = refs[pos + 1:]
    h = _rms(x_ref[...], nw_ref[...])
    hb, hl = _split_bf16(h)
    lo_start = 0
    for i, (start, width) in enumerate(segs):
        w = w_ref[:, start:start + width]
        z = _dot(hb, w)
        if i < n_hi:
            z = z + _dot(hl, w) + _dot(hb, wlo_ref[:, lo_start:lo_start + width])
            lo_start += width
        if i == len(segs) - 1:
            z = z + gb_ref[...]
            lane = lax.broadcasted_iota(jnp.int32, z.shape, 1)
            z = jnp.where((lane >= gate_lf[0]) & (lane < gate_lf[1]), _log_sigmoid(z), z)
        out_refs[i][...] = z


def _rms_proj(x, nw, w_bf, w_lo, gate_bias, segs, n_hi, gate_lf, tm):
    rows, d = x.shape
    ncols = w_bf.shape[1]
    in_specs = [pl.BlockSpec((tm, d), lambda i: (i, 0)),
                pl.BlockSpec((1, d), lambda i: (0, 0)),
                pl.BlockSpec((d, ncols), lambda i: (0, 0))]
    args = [x, nw.reshape(1, d), w_bf]
    if n_hi:
        in_specs.append(pl.BlockSpec(w_lo.shape, lambda i: (0, 0)))
        args.append(w_lo)
    in_specs.append(pl.BlockSpec((1, LANES), lambda i: (0, 0)))
    args.append(gate_bias)
    return pl.pallas_call(
        functools.partial(_rms_proj_kernel, segs=segs, n_hi=n_hi, gate_lf=gate_lf),
        out_shape=[jax.ShapeDtypeStruct((rows, wd), F32) for _, wd in segs],
        grid=(rows // tm,),
        in_specs=in_specs,
        out_specs=[pl.BlockSpec((tm, wd), lambda i: (i, 0)) for _, wd in segs],
        compiler_params=_params("arbitrary"),
        name="rms_proj",
    )(*args)


def _proj_res_kernel(*refs, n_in):
    a_refs = refs[:n_in]
    w_refs = refs[n_in:2 * n_in]
    x_ref, nw_ref, o_ref = refs[2 * n_in:]
    acc = None
    for a_ref, w_ref in zip(a_refs, w_refs):
        z = _dot(a_ref[...].astype(BF16), w_ref[...])
        acc = z if acc is None else acc + z
    o_ref[...] = x_ref[...] + _rms(acc, nw_ref[...])


def _proj_res(acts, ws, x, nw, tm):
    rows, d = x.shape
    n_in = len(acts)
    in_specs = ([pl.BlockSpec((tm, a.shape[1]), lambda i: (i, 0)) for a in acts]
                + [pl.BlockSpec(w.shape, lambda i: (0, 0)) for w in ws]
                + [pl.BlockSpec((tm, d), lambda i: (i, 0)), pl.BlockSpec((1, d), lambda i: (0, 0))])
    return pl.pallas_call(
        functools.partial(_proj_res_kernel, n_in=n_in),
        out_shape=jax.ShapeDtypeStruct((rows, d), F32),
        grid=(rows // tm,),
        in_specs=in_specs,
        out_specs=pl.BlockSpec((tm, d), lambda i: (i, 0)),
        compiler_params=_params("arbitrary"),
        name="proj_res",
    )(*acts, *ws, x, nw.reshape(1, d))


def _ffn_kernel(*refs, tm, tf, nj, tiles_per_seq, stateful):
    (x_ref, nw_in_ref, wa_ref, wb_ref, cwa_ref, cwb_ref, cba_ref, cbb_ref, wo_ref, nw_out_ref) = refs[:10]
    pos = 10
    if stateful:
        s0a_ref, s0b_ref, s1a_ref, s1b_ref = refs[pos:pos + 4]
        pos += 4
    y_ref, oa_ref, ob_ref = refs[pos:pos + 3]
    pos += 3
    hb_ref, acc_ref = refs[pos:pos + 2]
    pos += 2
    if not stateful:
        ubuf_ref, carry_ref = refs[pos:pos + 2]
    i = pl.program_id(0)
    j = pl.program_id(1)

    @pl.when(j == 0)
    def _():
        hb_ref[...] = _rms(x_ref[...], nw_in_ref[...]).astype(BF16)
        acc_ref[...] = jnp.zeros_like(acc_ref)

    hb = hb_ref[...]
    halves = []
    for half, (w_ref, cw_ref, cb_ref) in enumerate(((wa_ref, cwa_ref, cba_ref), (wb_ref, cwb_ref, cbb_ref))):
        u = _dot(hb, w_ref[...])
        cw = cw_ref[...]
        if stateful:
            s0 = (s0a_ref, s0b_ref)[half][...]
            s1 = (s1a_ref, s1b_ref)[half][...]
            y = cw[0:1] * s0 + cw[1:2] * s1 + cw[2:3] * u + cb_ref[...]
            (oa_ref, ob_ref)[half][...] = u
        else:
            ubuf_ref[half, SUBLANES:, :] = u

            @pl.when(i % tiles_per_seq == 0)
            def _():
                ubuf_ref[half, 0:SUBLANES, :] = jnp.zeros((SUBLANES, tf), F32)

            @pl.when(i % tiles_per_seq != 0)
            def _():
                ubuf_ref[half, 0:SUBLANES, :] = carry_ref[half, j]

            y = (cw[0:1] * ubuf_ref[half, SUBLANES - 2:SUBLANES - 2 + tm, :]
                 + cw[1:2] * ubuf_ref[half, SUBLANES - 1:SUBLANES - 1 + tm, :]
                 + cw[2:3] * u + cb_ref[...])
            carry_ref[half, j] = ubuf_ref[half, tm:tm + SUBLANES, :]
            (oa_ref, ob_ref)[half][0] = ubuf_ref[half, tm + SUBLANES - 2:tm + SUBLANES, :]
        halves.append(y)
    g = _gelu_tanh(halves[0]) * halves[1]
    acc_ref[...] += _dot(g.astype(BF16), wo_ref[...])

    @pl.when(j == nj - 1)
    def _():
        y_ref[...] = x_ref[...] + _rms(acc_ref[...], nw_out_ref[...])


def _ffn(x, nw_in, w_in_bf, conv_w, conv_b, w_out_bf, nw_out, tm, seq_rows, state=None):
    rows, d = x.shape
    tf = FFN_COLS
    nj = D_FF // tf
    stateful = state is not None
    cb2 = conv_b.reshape(1, 2 * D_FF)
    in_specs = [pl.BlockSpec((tm, d), lambda i, j: (i, 0)),
                pl.BlockSpec((1, d), lambda i, j: (0, 0)),
                pl.BlockSpec((d, tf), lambda i, j: (0, j)),
                pl.BlockSpec((d, tf), lambda i, j: (0, nj + j)),
                pl.BlockSpec((FFN_CONV, tf), lambda i, j: (0, j)),
                pl.BlockSpec((FFN_CONV, tf), lambda i, j: (0, nj + j)),
                pl.BlockSpec((1, tf), lambda i, j: (0, j)),
                pl.BlockSpec((1, tf), lambda i, j: (0, nj + j)),
                pl.BlockSpec((tf, d), lambda i, j: (j, 0)),
                pl.BlockSpec((1, d), lambda i, j: (0, 0))]
    args = [x, nw_in.reshape(1, d), w_in_bf, w_in_bf, conv_w, conv_w, cb2, cb2, w_out_bf, nw_out.reshape(1, d)]
    scratch = [pltpu.VMEM((tm, d), BF16), pltpu.VMEM((tm, d), F32)]
    if stateful:
        s0, s1 = state[:, 0, :], state[:, 1, :]
        in_specs += [pl.BlockSpec((tm, tf), lambda i, j: (i, j)), pl.BlockSpec((tm, tf), lambda i, j: (i, nj + j)),
                     pl.BlockSpec((tm, tf), lambda i, j: (i, j)), pl.BlockSpec((tm, tf), lambda i, j: (i, nj + j))]
        args += [s0, s0, s1, s1]
        out_shape = [jax.ShapeDtypeStruct((rows, d), F32), jax.ShapeDtypeStruct((rows, D_FF), F32),
                     jax.ShapeDtypeStruct((rows, D_FF), F32)]
        out_specs = [pl.BlockSpec((tm, d), lambda i, j: (i, 0)), pl.BlockSpec((tm, tf), lambda i, j: (i, j)),
                     pl.BlockSpec((tm, tf), lambda i, j: (i, j))]
        tiles_per_seq = 1
    else:
        tiles_per_seq = seq_rows // tm
        ntiles = rows // tm
        out_shape = [jax.ShapeDtypeStruct((rows, d), F32), jax.ShapeDtypeStruct((ntiles, FFN_CONV - 1, D_FF), F32),
                     jax.ShapeDtypeStruct((ntiles, FFN_CONV - 1, D_FF), F32)]
        out_specs = [pl.BlockSpec((tm, d), lambda i, j: (i, 0)),
                     pl.BlockSpec((1, FFN_CONV - 1, tf), lambda i, j: (i, 0, j)),
                     pl.BlockSpec((1, FFN_CONV - 1, tf), lambda i, j: (i, 0, j))]
        scratch += [pltpu.VMEM((2, tm + SUBLANES, tf), F32), pltpu.VMEM((2, nj, SUBLANES, tf), F32)]
    y, ua, ub = pl.pallas_call(
        functools.partial(_ffn_kernel, tm=tm, tf=tf, nj=nj, tiles_per_seq=tiles_per_seq, stateful=stateful),
        out_shape=out_shape,
        grid=(rows // tm, nj),
        in_specs=in_specs,
        out_specs=out_specs,
        scratch_shapes=scratch,
        compiler_params=_params("arbitrary", "arbitrary"),
        name="conv_ffn",
    )(*args)
    if not stateful:
        ua, ub = ua[tiles_per_seq - 1::tiles_per_seq], ub[tiles_per_seq - 1::tiles_per_seq]
    return y, jnp.concatenate([ua, ub], axis=-1)


def _scan_lanes(x, op, fill):
    n = x.shape[1]
    lane = lax.broadcasted_iota(jnp.int32, x.shape, 1)
    s = 1
    while s < n:
        x = op(x, jnp.where(lane >= s, pltpu.roll(x, s, axis=1), fill))
        s *= 2
    return x


def _mlstm_gate_scan_kernel(ig_ref, lf_ref, b_ref, m_ref, mrow_ref):
    a = _scan_lanes(lf_ref[...], jnp.add, 0.0)
    b = ig_ref[...] - a
    m = jnp.maximum(_scan_lanes(b, jnp.maximum, NEG_INF), 0.0)
    b_ref[...] = b
    m_ref[...] = m
    mrow_ref[...] = a + m


def _mlstm_gate_scan(ig_t, lf_t, seq):
    rows, total = ig_t.shape
    spec = pl.BlockSpec((rows, seq), lambda n: (0, n))
    return pl.pallas_call(
        _mlstm_gate_scan_kernel,
        out_shape=[jax.ShapeDtypeStruct((rows, total), F32)] * 3,
        grid=(total // seq,),
        in_specs=[spec, spec],
        out_specs=[spec, spec, spec],
        compiler_params=_params("arbitrary"),
        name="mlstm_gate_scan",
    )(ig_t, lf_t)


def _cumsum_kernel(x_ref, o_ref):
    o_ref[...] = _scan_lanes(x_ref[...], jnp.add, 0.0)


def _cumsum_time(x_t, seq):
    rows, total = x_t.shape
    spec = pl.BlockSpec((rows, seq), lambda n: (0, n))
    return pl.pallas_call(
        _cumsum_kernel,
        out_shape=jax.ShapeDtypeStruct((rows, total), F32),
        grid=(total // seq,),
        in_specs=[spec],
        out_specs=spec,
        compiler_params=_params("arbitrary"),
        name="logf_cumsum",
    )(x_t)


def _mlstm_prompt_kernel(qk_ref, v_ref, ob_ref, rows_ref, cols_ref, cw_ref, cb_ref, nw_ref,
                         g_ref, c_ref, n_ref, mfin_ref, xbuf_ref, mprev_ref, *, tl):
    c = pl.program_id(1)
    nc = pl.num_programs(1)

    @pl.when(c == 0)
    def _():
        xbuf_ref[0:SUBLANES, :] = jnp.zeros((SUBLANES, 2 * W_B), F32)
        mprev_ref[...] = jnp.zeros_like(mprev_ref)
        c_ref[...] = jnp.zeros_like(c_ref)
        n_ref[...] = jnp.zeros_like(n_ref)

    xbuf_ref[SUBLANES:, :] = qk_ref[...]
    cw = cw_ref[...]
    y = cb_ref[...]
    for t in range(MLSTM_CONV):
        off = SUBLANES - (MLSTM_CONV - 1) + t
        y = y + cw[t:t + 1] * xbuf_ref[off:off + tl, :]
    xbuf_ref[0:SUBLANES, :] = xbuf_ref[tl:tl + SUBLANES, :]
    qk = y * _sigmoid(y)

    cols = cols_ref[...]
    mprev = mprev_ref[...]
    t_idx = lax.broadcasted_iota(jnp.int32, (tl, tl), 0)
    s_idx = lax.broadcasted_iota(jnp.int32, (tl, tl), 1)
    causal = s_idx <= t_idx
    for h in range(H_B):
        q = qk[:, h * DH_B:(h + 1) * DH_B]
        k = qk[:, W_B + h * DH_B:W_B + (h + 1) * DH_B] * (DH_B ** -0.5)
        v = v_ref[:, h * DH_B:(h + 1) * DH_B]
        qb, kb, vb = q.astype(BF16), k.astype(BF16), v.astype(BF16)
        b_row = rows_ref[h:h + 1, :]
        b_col = cols[:, h:h + 1]
        m_col = cols[:, H_B + h:H_B + h + 1]
        mrow_col = cols[:, 2 * H_B + h:2 * H_B + h + 1]
        m_last = cols[tl - 1:tl, H_B + h:H_B + h + 1]
        m_prev = mprev[:, H_B + h:H_B + h + 1]
        dmat = jnp.exp(jnp.where(causal, b_row - m_col, NEG_INF))
        s = _dot_nt(qb, kb) * dmat
        w_inter = jnp.exp(m_prev - m_col)
        cmat = c_ref[0, h]
        nvec = n_ref[0, h:h + 1, :]
        num = _dot(s.astype(BF16), vb) + w_inter * _dot(qb, cmat.astype(BF16))
        den = jnp.sum(s, axis=1, keepdims=True) + w_inter * jnp.sum(q * nvec, axis=1, keepdims=True)
        hh = num / jnp.maximum(jnp.abs(den), jnp.exp(-mrow_col))
        w_s = jnp.exp(b_col - m_last)
        w_c = jnp.exp(m_prev - m_last)
        kw = k * w_s
        c_ref[0, h] = w_c * cmat + lax.dot_general(kw.astype(BF16), vb, (((0,), (0,)), ((), ())),
                                                    preferred_element_type=F32)
        n_ref[0, h:h + 1, :] = w_c * nvec + jnp.sum(kw, axis=0, keepdims=True)
        hn = _rms(hh, nw_ref[:, h * DH_B:(h + 1) * DH_B])
        g_ref[:, h * DH_B:(h + 1) * DH_B] = _sigmoid(ob_ref[:, h * DH_B:(h + 1) * DH_B]) * hn
    mprev_ref[...] = cols[tl - 1:tl, :]

    @pl.when(c == nc - 1)
    def _():
        mfin_ref[0] = cols[tl - 1:tl, :]


def _mlstm_prompt(qk_pre, v_pre, ob, rows_pack, cols_pack, conv_w, conv_b, norm_w, nseq, seq):
    tl = MLSTM_TILE
    nc = seq // tl
    total = nseq * seq
    rmap = lambda n, c: (n * nc + c, 0)
    return pl.pallas_call(
        functools.partial(_mlstm_prompt_kernel, tl=tl),
        out_shape=[jax.ShapeDtypeStruct((total, W_B), F32),
                   jax.ShapeDtypeStruct((nseq, H_B, DH_B, DH_B), F32),
                   jax.ShapeDtypeStruct((nseq, H_B, DH_B), F32),
                   jax.ShapeDtypeStruct((nseq, 1, LANES), F32)],
        grid=(nseq, nc),
        in_specs=[pl.BlockSpec((tl, 2 * W_B), rmap),
                  pl.BlockSpec((tl, W_B), rmap),
                  pl.BlockSpec((tl, W_B), rmap),
                  pl.BlockSpec((SUBLANES, tl), lambda n, c: (0, n * nc + c)),
                  pl.BlockSpec((tl, LANES), rmap),
                  pl.BlockSpec((MLSTM_CONV, 2 * W_B), lambda n, c: (0, 0)),
                  pl.BlockSpec((1, 2 * W_B), lambda n, c: (0, 0)),
                  pl.BlockSpec((1, W_B), lambda n, c: (0, 0))],
        out_specs=[pl.BlockSpec((tl, W_B), rmap),
                   pl.BlockSpec((1, H_B, DH_B, DH_B), lambda n, c: (n, 0, 0, 0)),
                   pl.BlockSpec((1, H_B, DH_B), lambda n, c: (n, 0, 0)),
                   pl.BlockSpec((1, 1, LANES), lambda n, c: (n, 0, 0))],
        scratch_shapes=[pltpu.VMEM((tl + SUBLANES, 2 * W_B), F32), pltpu.VMEM((1, LANES), F32)],
        compiler_params=_params("arbitrary", "arbitrary"),
        name="mlstm_prompt",
    )(qk_pre, v_pre, ob, rows_pack, cols_pack, conv_w, conv_b.reshape(1, -1), norm_w.reshape(1, -1))


def _head_masks(shape):
    lane = lax.broadcasted_iota(jnp.int32, shape, 1)
    return lane < (LANES // 2), lane >= (LANES // 2)


def _softmax_step(state, s, vt):
    m, l, acc = state
    m_new = jnp.maximum(m, jnp.max(s, axis=1, keepdims=True))
    alpha = jnp.exp(m - m_new)
    p = jnp.exp(s - m_new)
    l = alpha * l + jnp.sum(p, axis=1, keepdims=True)
    acc = alpha * acc + _dot(p.astype(BF16), vt)
    return m_new, l, acc


def _init_state(tq):
    return (jnp.full((tq, 1), NEG_INF, F32), jnp.zeros((tq, 1), F32), jnp.zeros((tq, LANES), F32))


def _fox_prompt_kernel(q_ref, k_ref, v_ref, crow_ref, o_ref, kb_ref, vb_ref, *, tq):
    qi = pl.program_id(2)

    @pl.when(qi == 0)
    def _():
        kb_ref[...] = k_ref[...].astype(BF16)
        vb_ref[...] = v_ref[...].astype(BF16)

    q = q_ref[...] * (DH_C ** -0.5)
    lo, hi = _head_masks(q.shape)
    qs = (jnp.where(lo, q, 0.0).astype(BF16), jnp.where(hi, q, 0.0).astype(BF16))
    row = lax.broadcasted_iota(jnp.int32, (tq, tq), 0)
    col = lax.broadcasted_iota(jnp.int32, (tq, tq), 1)

    def tile(j, states, diagonal):
        start = pl.multiple_of(j * tq, tq)
        kt = kb_ref[pl.ds(start, tq), :]
        vt = vb_ref[pl.ds(start, tq), :]
        out = []
        for hh in range(2):
            s = _dot_nt(qs[hh], kt) - crow_ref[0, hh:hh + 1, pl.ds(start, tq)]
            if diagonal:
                s = jnp.where(col <= row, s, NEG_INF)
            out.append(_softmax_step(states[hh], s, vt))
        return tuple(out)

    states = tile(qi, (_init_state(tq), _init_state(tq)), True)
    states = lax.fori_loop(0, qi, lambda j, st: tile(j, st, False), states)
    o_ref[...] = jnp.where(lo, states[0][2] / states[0][1], states[1][2] / states[1][1])


def _fox_prompt(q, k, v, crow, nseq, seq):
    tq = ATT_TILE
    nq = seq // tq
    npair = H_C // 2
    return pl.pallas_call(
        functools.partial(_fox_prompt_kernel, tq=tq),
        out_shape=jax.ShapeDtypeStruct(q.shape, F32),
        grid=(nseq, npair, nq),
        in_specs=[pl.BlockSpec((tq, LANES), lambda n, p, i: (n * nq + i, p)),
                  pl.BlockSpec((seq, LANES), lambda n, p, i: (n, p)),
                  pl.BlockSpec((seq, LANES), lambda n, p, i: (n, p)),
                  pl.BlockSpec((1, 2, seq), lambda n, p, i: (p, 0, n))],
        out_specs=pl.BlockSpec((tq, LANES), lambda n, p, i: (n * nq + i, p)),
        scratch_shapes=[pltpu.VMEM((seq, LANES), BF16), pltpu.VMEM((seq, LANES), BF16)],
        compiler_params=_params("arbitrary", "arbitrary", "arbitrary"),
        name="fox_prompt",
    )(q, k, v, crow)


def _rank_rows(g, n_valid):
    nb = g.shape[0]
    r = lax.broadcasted_iota(jnp.int32, g.shape, 0)
    g = jnp.where(r < n_valid, g, NEG_INF)
    rank = jnp.zeros(g.shape, jnp.int32)
    for i in range(nb):
        gi = g[i:i + 1, :]
        rank = rank + ((gi > g) | ((gi == g) & (i < r))).astype(jnp.int32)
    return rank, r


def _moba_prompt_kernel(q_ref, k_ref, v_ref, o_ref, kb_ref, vb_ref, kmean_ref, *, tq, nb):
    qi = pl.program_id(2)

    @pl.when(qi == 0)
    def _():
        kb_ref[...] = k_ref[...].astype(BF16)
        vb_ref[...] = v_ref[...].astype(BF16)
        kmean_ref[...] = jnp.zeros_like(kmean_ref)
        for j in range(nb):
            kmean_ref[j:j + 1, :] = jnp.mean(k_ref[j * MOBA_BLOCK:(j + 1) * MOBA_BLOCK, :], axis=0, keepdims=True)

    q = q_ref[...]
    lo, hi = _head_masks(q.shape)
    qm = (jnp.where(lo, q, 0.0), jnp.where(hi, q, 0.0))
    km_hi, km_lo = _split_bf16(kmean_ref[...])
    eye = (lax.broadcasted_iota(jnp.int32, (tq, tq), 0) == lax.broadcasted_iota(jnp.int32, (tq, tq), 1))
    eye = eye.astype(BF16)
    sels, qs = [], []
    for hh in range(2):
        q_hi, q_lo = _split_bf16(qm[hh])
        gate = _dot_nt(km_hi, q_hi) + _dot_nt(km_lo, q_hi) + _dot_nt(km_hi, q_lo)
        rank, r = _rank_rows(gate, qi)
        sel_t = ((rank < MOBA_TOPK) & (r < qi)).astype(BF16)
        sels.append(_dot_nt(eye, sel_t))
        qs.append((qm[hh] * (DH_A ** -0.5)).astype(BF16))
    row = lax.broadcasted_iota(jnp.int32, (tq, tq), 0)
    col = lax.broadcasted_iota(jnp.int32, (tq, tq), 1)
    blk_lane = lax.broadcasted_iota(jnp.int32, sels[0].shape, 1)

    def tile(j, states, diagonal):
        start = pl.multiple_of(j * tq, tq)
        kt = kb_ref[pl.ds(start, tq), :]
        vt = vb_ref[pl.ds(start, tq), :]
        out = []
        for hh in range(2):
            s = _dot_nt(qs[hh], kt)
            if diagonal:
                s = jnp.where(col <= row, s, NEG_INF)
            else:
                chosen = jnp.sum(jnp.where(blk_lane == j, sels[hh], 0.0), axis=1, keepdims=True)
                s = jnp.where(chosen > 0.5, s, NEG_INF)
            out.append(_softmax_step(states[hh], s, vt))
        return tuple(out)

    states = tile(qi, (_init_state(tq), _init_state(tq)), True)
    states = lax.fori_loop(0, qi, lambda j, st: tile(j, st, False), states)
    o_ref[...] = jnp.where(lo, states[0][2] / states[0][1], states[1][2] / states[1][1])


def _moba_prompt(q, k, v, nseq, seq):
    tq = ATT_TILE
    assert tq == MOBA_BLOCK and seq % MOBA_BLOCK == 0 and seq // MOBA_BLOCK >= MOBA_TOPK
    nq = seq // tq
    nb = seq // MOBA_BLOCK
    nbp = -(-nb // SUBLANES) * SUBLANES
    npair = H_A // 2
    return pl.pallas_call(
        functools.partial(_moba_prompt_kernel, tq=tq, nb=nb),
        out_shape=jax.ShapeDtypeStruct(q.shape, F32),
        grid=(nseq, npair, nq),
        in_specs=[pl.BlockSpec((tq, LANES), lambda n, p, i: (n * nq + i, p)),
                  pl.BlockSpec((seq, LANES), lambda n, p, i: (n, p)),
                  pl.BlockSpec((seq, LANES), lambda n, p, i: (n, p))],
        out_specs=pl.BlockSpec((tq, LANES), lambda n, p, i: (n * nq + i, p)),
        scratch_shapes=[pltpu.VMEM((seq, LANES), BF16), pltpu.VMEM((seq, LANES), BF16),
                        pltpu.VMEM((nbp, LANES), F32)],
        compiler_params=_params("arbitrary", "arbitrary", "arbitrary"),
        name="moba_prompt",
    )(q, k, v)


def _moba_sample_select_kernel(pt_ref, q_ref, *refs, group, nblk):
    k_refs = refs[:group]
    idx_ref = refs[group]
    kmean_ref = refs[group + 1]
    g = pl.program_id(1)

    @pl.when(g == 0)
    def _():
        kmean_ref[...] = jnp.zeros_like(kmean_ref)

    r = lax.broadcasted_iota(jnp.int32, kmean_ref.shape, 0)
    pages_per_block = MOBA_BLOCK // PAGE_SIZE
    upd = jnp.zeros(kmean_ref.shape, F32)
    for i in range(group):
        pm = jnp.mean(k_refs[i][0], axis=0, keepdims=True)
        blk = (g * group + i) // pages_per_block
        upd = upd + jnp.where(r == blk, pm * (1.0 / pages_per_block), 0.0)
    kmean_ref[...] += upd

    @pl.when(g == pl.num_programs(1) - 1)
    def _():
        prod = kmean_ref[...] * q_ref[0]
        seg = (lax.broadcasted_iota(jnp.int32, (W_A, LANES), 0) // DH_A
               == lax.broadcasted_iota(jnp.int32, (W_A, LANES), 1)).astype(BF16)
        p0 = prod.astype(BF16)
        r1 = prod - p0.astype(F32)
        p1 = r1.astype(BF16)
        p2 = (r1 - p1.astype(F32)).astype(BF16)
        gate = _dot(p0, seg) + _dot(p1, seg) + _dot(p2, seg)
        rank, rr = _rank_rows(gate, nblk)
        out = jnp.zeros((SUBLANES, LANES), jnp.int32)
        orow = lax.broadcasted_iota(jnp.int32, (SUBLANES, LANES), 0)
        for t in range(MOBA_TOPK):
            it = jnp.sum(jnp.where(rank == t, rr, 0), axis=0, keepdims=True)
            out = jnp.where(orow == t, it, out)
        idx_ref[0] = out


def _moba_sample_select(q, pool_k, page_table, nblk):
    nb = q.shape[0]
    group = PAGES_PER_STEP
    npg = nblk * (MOBA_BLOCK // PAGE_SIZE)
    assert npg % group == 0 and nblk % SUBLANES == 0
    in_specs = [pl.BlockSpec((1, 1, W_A), lambda n, g, pt: (n, 0, 0))]
    for i in range(group):
        in_specs.append(pl.BlockSpec((1, PAGE_SIZE, W_A), lambda n, g, pt, i=i: (pt[n, g * group + i], 0, 0)))
    return pl.pallas_call(
        functools.partial(_moba_sample_select_kernel, group=group, nblk=nblk),
        out_shape=jax.ShapeDtypeStruct((nb, SUBLANES, LANES), jnp.int32),
        grid_spec=pltpu.PrefetchScalarGridSpec(
            num_scalar_prefetch=1,
            grid=(nb, npg // group),
            in_specs=in_specs,
            out_specs=pl.BlockSpec((1, SUBLANES, LANES), lambda n, g, pt: (n, 0, 0)),
            scratch_shapes=[pltpu.VMEM((nblk, W_A), F32)]),
        compiler_params=_params("arbitrary", "arbitrary"),
        name="moba_sample_select",
    )(page_table, q.reshape(nb, 1, W_A), *([pool_k] * group))


def _moba_sample_attend_kernel(pt_ref, idx_ref, q_ref, kn_ref, vn_ref, *refs, n_sel, n_own, qblk):
    n_pages = n_sel + n_own
    k_refs = refs[:n_pages]
    v_refs = refs[n_pages:2 * n_pages]
    o_ref = refs[2 * n_pages]
    n = pl.program_id(0)
    h = pl.program_id(1)
    lane = lax.broadcasted_iota(jnp.int32, (1, LANES), 1)
    mine = (lane // DH_A) == (h % 2)
    q = jnp.where(mine, q_ref[0], 0.0)
    qb = jnp.broadcast_to(q * (DH_A ** -0.5), (SUBLANES, LANES)).astype(BF16)
    pages_per_block = MOBA_BLOCK // PAGE_SIZE
    logits = []
    for i in range(n_pages):
        s = _dot_nt(qb, k_refs[i][0].astype(BF16))[0:1]
        if i < n_sel:
            valid = idx_ref[n, h, i // pages_per_block] < qblk
            s = jnp.where(valid, s, NEG_INF)
        logits.append(s)
    s_new = jnp.sum(q * kn_ref[0], axis=1, keepdims=True) * (DH_A ** -0.5)
    m = s_new
    for s in logits:
        m = jnp.maximum(m, jnp.max(s, axis=1, keepdims=True))
    p_new = jnp.exp(s_new - m)
    l = p_new
    acc = p_new * vn_ref[0]
    for i, s in enumerate(logits):
        p = jnp.exp(s - m)
        l = l + jnp.sum(p, axis=1, keepdims=True)
        pb = jnp.broadcast_to(p, (SUBLANES, PAGE_SIZE)).astype(BF16)
        acc = acc + _dot(pb, v_refs[i][0].astype(BF16))[0:1]
    res = acc / l

    @pl.when(h % 2 == 0)
    def _():
        o_ref[0] = jnp.where(mine, res, 0.0)

    @pl.when(h % 2 == 1)
    def _():
        o_ref[0] = jnp.where(mine, res, o_ref[0])


def _moba_sample_attend(q, k_new, v_new, pool_k, pool_v, page_table, idx):
    nb = q.shape[0]
    n_pages_total = page_table.shape[1]
    pages_per_block = MOBA_BLOCK // PAGE_SIZE
    past = n_pages_total * PAGE_SIZE
    qblk = past // MOBA_BLOCK
    n_sel = MOBA_TOPK * pages_per_block
    n_own = 1 if (past - PAGE_SIZE) // MOBA_BLOCK == qblk else 0

    def sel_map(i):
        def index_map(n, h, pt, ix):
            lpage = jnp.minimum(ix[n, h, i // pages_per_block] * pages_per_block + i % pages_per_block,
                                n_pages_total - 1)
            return (pt[n, lpage], 0, h // 2)
        return index_map

    def own_map(n, h, pt, ix):
        return (pt[n, n_pages_total - 1], 0, h // 2)

    page_specs = [pl.BlockSpec((1, PAGE_SIZE, LANES), sel_map(i)) for i in range(n_sel)]
    page_specs += [pl.BlockSpec((1, PAGE_SIZE, LANES), own_map)] * n_own
    tok_spec = pl.BlockSpec((1, 1, LANES), lambda n, h, pt, ix: (n, 0, h // 2))
    tok = lambda a: a.reshape(nb, 1, W_A)
    n_pg = n_sel + n_own
    return pl.pallas_call(
        functools.partial(_moba_sample_attend_kernel, n_sel=n_sel, n_own=n_own, qblk=qblk),
        out_shape=jax.ShapeDtypeStruct((nb, 1, W_A), F32),
        grid_spec=pltpu.PrefetchScalarGridSpec(
            num_scalar_prefetch=2,
            grid=(nb, H_A),
            in_specs=[tok_spec, tok_spec, tok_spec] + page_specs + page_specs,
            out_specs=tok_spec),
        compiler_params=_params("arbitrary", "arbitrary"),
        name="moba_sample_attend",
    )(page_table, idx, tok(q), tok(k_new), tok(v_new), *([pool_k] * n_pg), *([pool_v] * n_pg)).reshape(nb, W_A)


def _mlstm_sample_kernel(qk_ref, st_ref, v_ref, ob_ref, gate_ref, c_ref, n_ref, m_ref, cw_ref, cb_ref, nw_ref,
                         g_ref, cn_ref, nn_ref, mn_ref):
    cw = cw_ref[...]
    st = st_ref[0]
    y = cb_ref[...] + cw[MLSTM_CONV - 1:MLSTM_CONV] * qk_ref[0]
    for t in range(MLSTM_CONV - 1):
        y = y + cw[t:t + 1] * st[t:t + 1]
    qk = y * _sigmoid(y)
    gates = gate_ref[0]
    m_in = m_ref[0]
    m_out = jnp.zeros((1, LANES), F32)
    lane = lax.broadcasted_iota(jnp.int32, (1, LANES), 1)
    for h in range(H_B):
        q = qk[:, h * DH_B:(h + 1) * DH_B]
        k = qk[:, W_B + h * DH_B:W_B + (h + 1) * DH_B] * (DH_B ** -0.5)
        v = v_ref[0][:, h * DH_B:(h + 1) * DH_B]
        ig = gates[:, h:h + 1]
        lf = gates[:, H_B + h:H_B + h + 1]
        m_old = m_in[:, h:h + 1]
        m_row = jnp.maximum(ig, lf + m_old)
        w_inter = jnp.exp(lf + m_old - m_row)
        w_s = jnp.exp(ig - m_row)
        s = jnp.sum(q * k, axis=1, keepdims=True) * w_s
        cmat = c_ref[0, h]
        nvec = n_ref[0, h:h + 1, :]
        qc = _dot(jnp.broadcast_to(q, (SUBLANES, DH_B)).astype(BF16), cmat.astype(BF16))[0:1]
        num = s * v + w_inter * qc
        den = s + w_inter * jnp.sum(q * nvec, axis=1, keepdims=True)
        hh = num / jnp.maximum(jnp.abs(den), jnp.exp(-m_row))
        k_col = jnp.transpose(jnp.broadcast_to(k, (DH_B, DH_B)))
        cn_ref[0, h] = w_inter * cmat + w_s * (k_col * v)
        nn_ref[0, h:h + 1, :] = w_inter * nvec + w_s * k
        m_out = jnp.where(lane == h, m_row, m_out)
        hn = _rms(hh, nw_ref[:, h * DH_B:(h + 1) * DH_B])
        g_ref[0, :, h * DH_B:(h + 1) * DH_B] = _sigmoid(ob_ref[0][:, h * DH_B:(h + 1) * DH_B]) * hn
    mn_ref[0] = m_out


def _mlstm_sample(qk_pre, conv_state, v_pre, ob, gates, c, nvec, m, conv_w, conv_b, norm_w):
    nb = qk_pre.shape[0]
    tok = lambda a: a.reshape(nb, 1, a.shape[-1])
    tspec = lambda w: pl.BlockSpec((1, 1, w), lambda n: (n, 0, 0))
    full = lambda shape: pl.BlockSpec(shape, lambda n: (0,) * len(shape))
    g, cn, nn, mn = pl.pallas_call(
        _mlstm_sample_kernel,
        out_shape=[jax.ShapeDtypeStruct((nb, 1, W_B), F32),
                   jax.ShapeDtypeStruct((nb, H_B, DH_B, DH_B), F32),
                   jax.ShapeDtypeStruct((nb, H_B, DH_B), F32),
                   jax.ShapeDtypeStruct((nb, 1, LANES), F32)],
        grid=(nb,),
        in_specs=[tspec(2 * W_B),
                  pl.BlockSpec((1, MLSTM_CONV - 1, 2 * W_B), lambda n: (n, 0, 0)),
                  tspec(W_B), tspec(W_B), tspec(LANES),
                  pl.BlockSpec((1, H_B, DH_B, DH_B), lambda n: (n, 0, 0, 0)),
                  pl.BlockSpec((1, H_B, DH_B), lambda n: (n, 0, 0)),
                  tspec(LANES),
                  full((MLSTM_CONV, 2 * W_B)), full((1, 2 * W_B)), full((1, W_B))],
        out_specs=[tspec(W_B),
                   pl.BlockSpec((1, H_B, DH_B, DH_B), lambda n: (n, 0, 0, 0)),
                   pl.BlockSpec((1, H_B, DH_B), lambda n: (n, 0, 0)),
                   tspec(LANES)],
        compiler_params=_params("arbitrary"),
        name="mlstm_sample",
    )(tok(qk_pre), conv_state, tok(v_pre), tok(ob), tok(gates), c, nvec, tok(_pad_lanes(m)),
      conv_w, conv_b.reshape(1, -1), norm_w.reshape(1, -1))
    return g.reshape(nb, W_B), cn, nn, mn[:, 0, :H_B]


def _fox_sample_kernel(pt_ref, q_ref, kn_ref, vn_ref, lfn_ref, *refs, group):
    k_refs = refs[:group]
    v_refs = refs[group:2 * group]
    lf_refs = refs[2 * group:3 * group]
    o_ref = refs[3 * group]
    m_ref, l_ref, run_ref, acc_ref = refs[3 * group + 1:]
    g = pl.program_id(1)

    @pl.when(g == 0)
    def _():
        m_ref[...] = jnp.full(m_ref.shape, NEG_INF, F32)
        l_ref[...] = jnp.zeros_like(l_ref)
        run_ref[...] = jnp.zeros_like(run_ref)
        acc_ref[...] = jnp.zeros_like(acc_ref)

    head = lax.broadcasted_iota(jnp.int32, (H_C, W_C), 0)
    mine = (lax.broadcasted_iota(jnp.int32, (H_C, W_C), 1) // DH_C) == head
    q_bd = jnp.where(mine, jnp.broadcast_to(q_ref[0], (H_C, W_C)), 0.0) * (DH_C ** -0.5)
    q_bf = q_bd.astype(BF16)
    m, l, run, acc = m_ref[...], l_ref[...], run_ref[...], acc_ref[...]
    for i in range(group):
        s = _dot_nt(q_bf, k_refs[i][0].astype(BF16))
        cum = run + _scan_lanes(lf_refs[i][0], jnp.add, 0.0)
        run = cum[:, PAGE_SIZE - 1:PAGE_SIZE]
        s = s - cum
        m_new = jnp.maximum(m, jnp.max(s, axis=1, keepdims=True))
        alpha = jnp.exp(m - m_new)
        p = jnp.exp(s - m_new)
        l = alpha * l + jnp.sum(p, axis=1, keepdims=True)
        acc = alpha * acc + _dot(p.astype(BF16), v_refs[i][0].astype(BF16))
        m = m_new
    m_ref[...], l_ref[...], run_ref[...], acc_ref[...] = m, l, run, acc

    @pl.when(g == pl.num_programs(1) - 1)
    def _():
        hrow = lax.broadcasted_iota(jnp.int32, (H_C, LANES), 0)
        hlane = lax.broadcasted_iota(jnp.int32, (H_C, LANES), 1)
        lf_new = jnp.sum(jnp.where(hrow == hlane, jnp.broadcast_to(lfn_ref[0], (H_C, LANES)), 0.0),
                         axis=1, keepdims=True)
        s_new = jnp.sum(q_bd * kn_ref[0], axis=1, keepdims=True) - (run + lf_new)
        m_f = jnp.maximum(m, s_new)
        alpha = jnp.exp(m - m_f)
        p_new = jnp.exp(s_new - m_f)
        l_f = alpha * l + p_new
        acc_f = alpha * acc + p_new * vn_ref[0]
        o_ref[0] = jnp.sum(jnp.where(mine, acc_f / l_f, 0.0), axis=0, keepdims=True)


def _fox_sample(q, k_new, v_new, lf_new, pool_k, pool_v, pool_lf_t, page_table):
    nb = q.shape[0]
    n_pages = page_table.shape[1]
    group = PAGES_PER_STEP
    assert n_pages % group == 0
    tok_spec = pl.BlockSpec((1, 1, W_C), lambda n, g, pt: (n, 0, 0))
    tok = lambda a: a.reshape(nb, 1, a.shape[-1])
    page = lambda i, shape: pl.BlockSpec(shape, lambda n, g, pt: (pt[n, g * group + i], 0, 0))
    in_specs = [tok_spec, tok_spec, tok_spec, pl.BlockSpec((1, 1, LANES), lambda n, g, pt: (n, 0, 0))]
    in_specs += [page(i, (1, PAGE_SIZE, W_C)) for i in range(group)]
    in_specs += [page(i, (1, PAGE_SIZE, W_C)) for i in range(group)]
    in_specs += [page(i, (1, H_C, PAGE_SIZE)) for i in range(group)]
    return pl.pallas_call(
        functools.partial(_fox_sample_kernel, group=group),
        out_shape=jax.ShapeDtypeStruct((nb, 1, W_C), F32),
        grid_spec=pltpu.PrefetchScalarGridSpec(
            num_scalar_prefetch=1,
            grid=(nb, n_pages // group),
            in_specs=in_specs,
            out_specs=tok_spec,
            scratch_shapes=[pltpu.VMEM((H_C, 1), F32), pltpu.VMEM((H_C, 1), F32), pltpu.VMEM((H_C, 1), F32),
                            pltpu.VMEM((H_C, W_C), F32)]),
        compiler_params=_params("arbitrary", "arbitrary"),
        name="fox_sample",
    )(page_table, tok(q), tok(k_new), tok(v_new), tok(lf_new),
      *([pool_k] * group), *([pool_v] * group), *([pool_lf_t] * group)).reshape(nb, W_C)


EVEN_SEGS = ((0, W_A), (W_A, W_A), (2 * W_A, W_A), (3 * W_A, 2 * W_B), (3 * W_A + 2 * W_B, W_B),
             (3 * W_A + 3 * W_B, W_B), (3 * W_A + 4 * W_B, LANES))
ODD_SEGS = ((0, W_C), (W_C, W_C), (2 * W_C, W_C), (3 * W_C, LANES))


def _even_weights(w_in, b_i, b_f):
    main = 3 * W_A + 4 * W_B
    w_pad = jnp.concatenate([w_in[:, :main], _pad_lanes(w_in[:, main:])], axis=1)
    w_bf = w_pad.astype(BF16)
    wq = w_in[:, :W_A]
    w_lo = (wq - wq.astype(BF16).astype(F32)).astype(BF16)
    gate_bias = _pad_lanes(jnp.concatenate([b_i, b_f]).reshape(1, -1))
    return w_bf, w_lo, gate_bias


def _odd_weights(w_in, b_f):
    w_pad = jnp.concatenate([w_in[:, :3 * W_C], _pad_lanes(w_in[:, 3 * W_C:])], axis=1)
    return w_pad.astype(BF16), _pad_lanes(b_f.reshape(1, -1))


def _rows_to_sublanes(a):
    return jnp.pad(a.T, ((0, SUBLANES - a.shape[1]), (0, 0)))


def kernel(x_prompt, x_sample, cache_moba_k, cache_moba_v, state_mlstm_c, state_mlstm_n, state_mlstm_m,
           state_mlstm_conv, cache_fox_k, cache_fox_v, cache_fox_logf, state_ffn_conv, page_table, norm_w,
           even_w_in, mlstm_conv_w, mlstm_conv_b, mlstm_b_i, mlstm_b_f, mlstm_norm_w, even_w_out, fox_w_in,
           fox_b_f, fox_w_out, ffn_w_in, ffn_conv_w, ffn_conv_b, ffn_w_out):
    bp, sp, d = x_prompt.shape
    bs, ss, _ = x_sample.shape
    assert ss == 1 and d == D_MODEL
    depth = norm_w.shape[0]
    n_pool = cache_fox_k.shape[1] if cache_fox_k.shape[0] else cache_moba_k.shape[1]
    n_pages = page_table.shape[1]
    tp = bp * sp
    yp = x_prompt.reshape(tp, d)
    ys = x_sample.reshape(bs, d)
    outs = {name: [] for name in (
        "mk_p", "mv_p", "mk_s", "mv_s", "mc_p", "mn_p", "mm_p", "mconv_p", "mc_s", "mn_s", "mm_s", "mconv_s",
        "fk_p", "fv_p", "fl_p", "fk_s", "fv_s", "fl_s", "ff_p", "ff_s")}
    for layer in range(depth):
        nw = norm_w[layer]
        if layer % 2 == 0:
            e = layer // 2
            w_bf, w_lo, gate_bias = _even_weights(even_w_in[e], mlstm_b_i[e], mlstm_b_f[e])
            w_out_bf = even_w_out[e].astype(BF16)
            proj = lambda x, tm: _rms_proj(x, nw[0], w_bf, w_lo, gate_bias, EVEN_SEGS, 1, (H_B, 2 * H_B), tm)
            qa, ka, va, qkb, vb, ob, gates = proj(yp, ROW_TILE)
            att = _moba_prompt(qa, ka, va, bp, sp)
            b_t, m_t, mrow_t = _mlstm_gate_scan(_rows_to_sublanes(gates[:, :H_B]),
                                                _rows_to_sublanes(gates[:, H_B:2 * H_B]), sp)
            cols = _pad_lanes(jnp.concatenate([b_t[:H_B].T, m_t[:H_B].T, mrow_t[:H_B].T], axis=1))
            gated, cc, nn, mfin = _mlstm_prompt(qkb, vb, ob, b_t, cols, mlstm_conv_w[e], mlstm_conv_b[e],
                                                mlstm_norm_w[e], bp, sp)
            yp = _proj_res([att, gated], [w_out_bf[:W_A], w_out_bf[W_A:]], yp, nw[1], FFN_ROWS)
            outs["mk_p"].append(ka.reshape(bp, sp, H_A, DH_A))
            outs["mv_p"].append(va.reshape(bp, sp, H_A, DH_A))
            outs["mc_p"].append(cc)
            outs["mn_p"].append(nn)
            outs["mm_p"].append(mfin[:, 0, 2 * H_B:3 * H_B])
            outs["mconv_p"].append(qkb.reshape(bp, sp, 2 * W_B)[:, sp - (MLSTM_CONV - 1):])
            qa, ka, va, qkb, vb, ob, gates = proj(ys, bs)
            pool_k = cache_moba_k[e].reshape(n_pool, PAGE_SIZE, W_A)
            pool_v = cache_moba_v[e].reshape(n_pool, PAGE_SIZE, W_A)
            nblk = (n_pages * PAGE_SIZE) // MOBA_BLOCK
            assert nblk >= MOBA_TOPK
            idx = _moba_sample_select(qa, pool_k, page_table, nblk)
            idx = jnp.transpose(idx[:, :MOBA_TOPK, :H_A], (0, 2, 1))
            att = _moba_sample_attend(qa, ka, va, pool_k, pool_v, page_table, idx)
            gated, cc, nn, mm = _mlstm_sample(qkb, state_mlstm_conv[e], vb, ob, gates, state_mlstm_c[e],
                                              state_mlstm_n[e], state_mlstm_m[e], mlstm_conv_w[e],
                                              mlstm_conv_b[e], mlstm_norm_w[e])
            ys = _proj_res([att, gated], [w_out_bf[:W_A], w_out_bf[W_A:]], ys, nw[1], bs)
            outs["mk_s"].append(ka.reshape(bs, ss, H_A, DH_A))
            outs["mv_s"].append(va.reshape(bs, ss, H_A, DH_A))
            outs["mc_s"].append(cc)
            outs["mn_s"].append(nn)
            outs["mm_s"].append(mm)
            outs["mconv_s"].append(jnp.concatenate([state_mlstm_conv[e][:, 1:], qkb[:, None, :]], axis=1))
        else:
            o = layer // 2
            w_bf, gate_bias = _odd_weights(fox_w_in[o], fox_b_f[o])
            w_out_bf = fox_w_out[o].astype(BF16)
            proj = lambda x, tm: _rms_proj(x, nw[0], w_bf, None, gate_bias, ODD_SEGS, 0, (0, H_C), tm)
            q, k, v, lf = proj(yp, ROW_TILE)
            lf_t = lf[:, :H_C].T
            crow = _cumsum_time(lf_t, sp).reshape(H_C // 2, 2, tp)
            att = _fox_prompt(q, k, v, crow, bp, sp)
            yp = _proj_res([att], [w_out_bf], yp, nw[1], FFN_ROWS)
            outs["fk_p"].append(k.reshape(bp, sp, H_C, DH_C))
            outs["fv_p"].append(v.reshape(bp, sp, H_C, DH_C))
            outs["fl_p"].append(lf[:, :H_C].reshape(bp, sp, H_C))
            q, k, v, lf = proj(ys, bs)
            pool_k = cache_fox_k[o].reshape(n_pool, PAGE_SIZE, W_C)
            pool_v = cache_fox_v[o].reshape(n_pool, PAGE_SIZE, W_C)
            pool_lf_t = jnp.transpose(cache_fox_logf[o], (0, 2, 1))
            att = _fox_sample(q, k, v, lf, pool_k, pool_v, pool_lf_t, page_table)
            ys = _proj_res([att], [w_out_bf], ys, nw[1], bs)
            outs["fk_s"].append(k.reshape(bs, ss, H_C, DH_C))
            outs["fv_s"].append(v.reshape(bs, ss, H_C, DH_C))
            outs["fl_s"].append(lf[:, :H_C].reshape(bs, ss, H_C))
        w_in_bf = ffn_w_in[layer].astype(BF16)
        w_o_bf = ffn_w_out[layer].astype(BF16)
        yp, buf_p = _ffn(yp, nw[2], w_in_bf, ffn_conv_w[layer], ffn_conv_b[layer], w_o_bf, nw[3], FFN_ROWS, sp)
        ys, u_s = _ffn(ys, nw[2], w_in_bf, ffn_conv_w[layer], ffn_conv_b[layer], w_o_bf, nw[3], bs, 1,
                       state=state_ffn_conv[layer])
        outs["ff_p"].append(buf_p)
        outs["ff_s"].append(jnp.concatenate([state_ffn_conv[layer][:, 1:], u_s[:, None, :]], axis=1))
    st = {name: jnp.stack(vals) for name, vals in outs.items()}
    return (yp.reshape(bp, sp, d), ys.reshape(bs, ss, d), st["mk_p"], st["mv_p"], st["mk_s"], st["mv_s"],
            st["mc_p"], st["mn_p"], st["mm_p"], st["mconv_p"], st["mc_s"], st["mn_s"], st["mm_s"], st["mconv_s"],
            st["fk_p"], st["fv_p"], st["fl_p"], st["fk_s"], st["fv_s"], st["fl_s"], st["ff_p"], st["ff_s"])
```

```python
import functools

import jax
import jax.numpy as jnp
from jax import lax
from jax.experimental import pallas as pl
from jax.experimental.pallas import tpu as pltpu

F32 = jnp.float32
BF16 = jnp.bfloat16
NEG_INF = float("-inf")

D_MODEL = 1024
PAGE_SIZE = 128
H_A, DH_A = 8, 64
W_A = H_A * DH_A
MOBA_BLOCK = 256
MOBA_TOPK = 3
H_B, DH_B = 4, 128
W_B = H_B * DH_B
MLSTM_CONV = 4
H_C, DH_C = 16, 64
W_C = H_C * DH_C
D_FF = 2816
FFN_CONV = 3
RMS_EPS = 1e-6

LANES = 128
SUBLANES = 8
VMEM_LIMIT = 56 * 1024 * 1024

ROW_TILE = 256
ATT_TILE = 256
FOX_TILE = 512
KEY_TILE = 256
MOBA_GROUP = 4
FOX_GROUP = 2
MLSTM_TILE = 256
FFN_ROWS = 512
FFN_COLS = 1408
FFN_SLICE = 256
PAGES_PER_STEP = 8


def _params(*sem):
    return pltpu.CompilerParams(dimension_semantics=sem, vmem_limit_bytes=VMEM_LIMIT)


def _rms(x, w):
    return x * lax.rsqrt(jnp.mean(x * x, axis=-1, keepdims=True) + RMS_EPS) * w


def _sigmoid(x):
    return 1.0 / (1.0 + jnp.exp(-x))


def _log_sigmoid(x):
    return -(jnp.maximum(-x, 0.0) + jnp.log1p(jnp.exp(-jnp.abs(x))))


def _gelu_tanh(x):
    c = 0.7978845608028654
    return 0.5 * x * (1.0 + jnp.tanh(c * (x + 0.044715 * (x * x * x))))


def _dot(a, b):
    return jnp.dot(a, b, preferred_element_type=F32)


def _dot_nt(a, b):
    return lax.dot_general(a, b, (((1,), (1,)), ((), ())), preferred_element_type=F32)


def _split_bf16(x):
    hi = x.astype(BF16)
    lo = (x - hi.astype(F32)).astype(BF16)
    return hi, lo


def _pad_lanes(a, width=LANES):
    return jnp.pad(a, ((0, 0),) * (a.ndim - 1) + ((0, width - a.shape[-1]),))


def _rms_proj_kernel(*refs, segs, n_hi, gate_lf):
    x_ref, nw_ref, w_ref = refs[:3]
    pos = 3
    wlo_ref = None
    if n_hi:
        wlo_ref = refs[pos]
        pos += 1
    gb_ref = refs[pos]
    out_refs = refs[pos + 1:]
    h = _rms(x_ref[...], nw_ref[...])
    hb, hl = _split_bf16(h)
    lo_start = 0
    for i, (start, width) in enumerate(segs):
        w = w_ref[:, start:start + width]
        z = _dot(hb, w)
        if i < n_hi:
            z = z + _dot(hl, w) + _dot(hb, wlo_ref[:, lo_start:lo_start + width])
            lo_start += width
        if i == len(segs) - 1:
            z = z + gb_ref[...]
            lane = lax.broadcasted_iota(jnp.int32, z.shape, 1)
            z = jnp.where((lane >= gate_lf[0]) & (lane < gate_lf[1]), _log_sigmoid(z), z)
        out_refs[i][...] = z


def _rms_proj(x, nw, w_bf, w_lo, gate_bias, segs, n_hi, gate_lf, tm):
    rows, d = x.shape
    ncols = w_bf.shape[1]
    in_specs = [pl.BlockSpec((tm, d), lambda i: (i, 0)),
                pl.BlockSpec((1, d), lambda i: (0, 0)),
                pl.BlockSpec((d, ncols), lambda i: (0, 0))]
    args = [x, nw.reshape(1, d), w_bf]
    if n_hi:
        in_specs.append(pl.BlockSpec(w_lo.shape, lambda i: (0, 0)))
        args.append(w_lo)
    in_specs.append(pl.BlockSpec((1, LANES), lambda i: (0, 0)))
    args.append(gate_bias)
    return pl.pallas_call(
        functools.partial(_rms_proj_kernel, segs=segs, n_hi=n_hi, gate_lf=gate_lf),
        out_shape=[jax.ShapeDtypeStruct((rows, wd), F32) for _, wd in segs],
        grid=(rows // tm,),
        in_specs=in_specs,
        out_specs=[pl.BlockSpec((tm, wd), lambda i: (i, 0)) for _, wd in segs],
        compiler_params=_params("arbitrary"),
        name="rms_proj",
    )(*args)


def _proj_res_kernel(*refs, n_in):
    a_refs = refs[:n_in]
    w_refs = refs[n_in:2 * n_in]
    x_ref, nw_ref, o_ref = refs[2 * n_in:]
    acc = None
    for a_ref, w_ref in zip(a_refs, w_refs):
        z = _dot(a_ref[...].astype(BF16), w_ref[...])
        acc = z if acc is None else acc + z
    o_ref[...] = x_ref[...] + _rms(acc, nw_ref[...])


def _proj_res(acts, ws, x, nw, tm):
    rows, d = x.shape
    n_in = len(acts)
    in_specs = ([pl.BlockSpec((tm, a.shape[1]), lambda i: (i, 0)) for a in acts]
                + [pl.BlockSpec(w.shape, lambda i: (0, 0)) for w in ws]
                + [pl.BlockSpec((tm, d), lambda i: (i, 0)), pl.BlockSpec((1, d), lambda i: (0, 0))])
    return pl.pallas_call(
        functools.partial(_proj_res_kernel, n_in=n_in),
        out_shape=jax.ShapeDtypeStruct((rows, d), F32),
        grid=(rows // tm,),
        in_specs=in_specs,
        out_specs=pl.BlockSpec((tm, d), lambda i: (i, 0)),
        compiler_params=_params("arbitrary"),
        name="proj_res",
    )(*acts, *ws, x, nw.reshape(1, d))


def _ffn_kernel(*refs, tm, tf, nj, tiles_per_seq, stateful):
    (x_ref, nw_in_ref, wa_ref, wb_ref, cwa_ref, cwb_ref, cba_ref, cbb_ref, wo_ref, nw_out_ref) = refs[:10]
    pos = 10
    if stateful:
        s0a_ref, s0b_ref, s1a_ref, s1b_ref = refs[pos:pos + 4]
        pos += 4
    y_ref, oa_ref, ob_ref = refs[pos:pos + 3]
    pos += 3
    hb_ref, acc_ref = refs[pos:pos + 2]
    pos += 2
    if not stateful:
        ubuf_ref, carry_ref = refs[pos:pos + 2]
    i = pl.program_id(0)
    j = pl.program_id(1)

    @pl.when(j == 0)
    def _():
        hb_ref[...] = _rms(x_ref[...], nw_in_ref[...]).astype(BF16)
        acc_ref[...] = jnp.zeros_like(acc_ref)

    hb = hb_ref[...]
    if not stateful:
        first = i % tiles_per_seq == 0

        @pl.when(first)
        def _():
            ubuf_ref[:, 0:SUBLANES, :] = jnp.zeros((2, SUBLANES, tf), F32)

        @pl.when(jnp.logical_not(first))
        def _():
            ubuf_ref[:, 0:SUBLANES, :] = carry_ref[:, j]

    gs = []
    for c0 in range(0, tf, FFN_SLICE):
        cs = slice(c0, min(c0 + FFN_SLICE, tf))
        halves = []
        for half, (w_ref, cw_ref, cb_ref) in enumerate(((wa_ref, cwa_ref, cba_ref), (wb_ref, cwb_ref, cbb_ref))):
            u = _dot(hb, w_ref[:, cs])
            cw = cw_ref[:, cs]
            if stateful:
                s0 = (s0a_ref, s0b_ref)[half][:, cs]
                s1 = (s1a_ref, s1b_ref)[half][:, cs]
                y = cw[0:1] * s0 + cw[1:2] * s1 + cw[2:3] * u + cb_ref[:, cs]
                (oa_ref, ob_ref)[half][:, cs] = u
            else:
                ubuf_ref[half, SUBLANES:, cs] = u
                y = (cw[0:1] * ubuf_ref[half, SUBLANES - 2:SUBLANES - 2 + tm, cs]
                     + cw[1:2] * ubuf_ref[half, SUBLANES - 1:SUBLANES - 1 + tm, cs]
                     + cw[2:3] * u + cb_ref[:, cs])
            halves.append(y)
        gs.append((_gelu_tanh(halves[0]) * halves[1]).astype(BF16))
    if not stateful:
        carry_ref[:, j] = ubuf_ref[:, tm:tm + SUBLANES, :]
        oa_ref[0] = ubuf_ref[0, tm + SUBLANES - 2:tm + SUBLANES, :]
        ob_ref[0] = ubuf_ref[1, tm + SUBLANES - 2:tm + SUBLANES, :]
    acc_ref[...] += _dot(jnp.concatenate(gs, axis=1), wo_ref[...])

    @pl.when(j == nj - 1)
    def _():
        y_ref[...] = x_ref[...] + _rms(acc_ref[...], nw_out_ref[...])


def _ffn(x, nw_in, w_in_bf, conv_w, conv_b, w_out_bf, nw_out, tm, seq_rows, state=None):
    rows, d = x.shape
    tf = FFN_COLS
    nj = D_FF // tf
    stateful = state is not None
    cb2 = conv_b.reshape(1, 2 * D_FF)
    in_specs = [pl.BlockSpec((tm, d), lambda i, j: (i, 0)),
                pl.BlockSpec((1, d), lambda i, j: (0, 0)),
                pl.BlockSpec((d, tf), lambda i, j: (0, j)),
                pl.BlockSpec((d, tf), lambda i, j: (0, nj + j)),
                pl.BlockSpec((FFN_CONV, tf), lambda i, j: (0, j)),
                pl.BlockSpec((FFN_CONV, tf), lambda i, j: (0, nj + j)),
                pl.BlockSpec((1, tf), lambda i, j: (0, j)),
                pl.BlockSpec((1, tf), lambda i, j: (0, nj + j)),
                pl.BlockSpec((tf, d), lambda i, j: (j, 0)),
                pl.BlockSpec((1, d), lambda i, j: (0, 0))]
    args = [x, nw_in.reshape(1, d), w_in_bf, w_in_bf, conv_w, conv_w, cb2, cb2, w_out_bf, nw_out.reshape(1, d)]
    scratch = [pltpu.VMEM((tm, d), BF16), pltpu.VMEM((tm, d), F32)]
    if stateful:
        s0, s1 = state[:, 0, :], state[:, 1, :]
        in_specs += [pl.BlockSpec((tm, tf), lambda i, j: (i, j)), pl.BlockSpec((tm, tf), lambda i, j: (i, nj + j)),
                     pl.BlockSpec((tm, tf), lambda i, j: (i, j)), pl.BlockSpec((tm, tf), lambda i, j: (i, nj + j))]
        args += [s0, s0, s1, s1]
        out_shape = [jax.ShapeDtypeStruct((rows, d), F32), jax.ShapeDtypeStruct((rows, D_FF), F32),
                     jax.ShapeDtypeStruct((rows, D_FF), F32)]
        out_specs = [pl.BlockSpec((tm, d), lambda i, j: (i, 0)), pl.BlockSpec((tm, tf), lambda i, j: (i, j)),
                     pl.BlockSpec((tm, tf), lambda i, j: (i, j))]
        tiles_per_seq = 1
    else:
        tiles_per_seq = seq_rows // tm
        ntiles = rows // tm
        out_shape = [jax.ShapeDtypeStruct((rows, d), F32), jax.ShapeDtypeStruct((ntiles, FFN_CONV - 1, D_FF), F32),
                     jax.ShapeDtypeStruct((ntiles, FFN_CONV - 1, D_FF), F32)]
        out_specs = [pl.BlockSpec((tm, d), lambda i, j: (i, 0)),
                     pl.BlockSpec((1, FFN_CONV - 1, tf), lambda i, j: (i, 0, j)),
                     pl.BlockSpec((1, FFN_CONV - 1, tf), lambda i, j: (i, 0, j))]
        scratch += [pltpu.VMEM((2, tm + SUBLANES, tf), F32), pltpu.VMEM((2, nj, SUBLANES, tf), F32)]
    y, ua, ub = pl.pallas_call(
        functools.partial(_ffn_kernel, tm=tm, tf=tf, nj=nj, tiles_per_seq=tiles_per_seq, stateful=stateful),
        out_shape=out_shape,
        grid=(rows // tm, nj),
        in_specs=in_specs,
        out_specs=out_specs,
        scratch_shapes=scratch,
        compiler_params=_params("arbitrary", "arbitrary"),
        name="conv_ffn",
    )(*args)
    if not stateful:
        ua, ub = ua[tiles_per_seq - 1::tiles_per_seq], ub[tiles_per_seq - 1::tiles_per_seq]
    return y, jnp.concatenate([ua, ub], axis=-1)


def _scan_lanes(x, op, fill):
    n = x.shape[1]
    lane = lax.broadcasted_iota(jnp.int32, x.shape, 1)
    s = 1
    while s < n:
        x = op(x, jnp.where(lane >= s, pltpu.roll(x, s, axis=1), fill))
        s *= 2
    return x


def _mlstm_gate_scan_kernel(ig_ref, lf_ref, b_ref, m_ref, mrow_ref):
    a = _scan_lanes(lf_ref[...], jnp.add, 0.0)
    b = ig_ref[...] - a
    m = jnp.maximum(_scan_lanes(b, jnp.maximum, NEG_INF), 0.0)
    b_ref[...] = b
    m_ref[...] = m
    mrow_ref[...] = a + m


def _mlstm_gate_scan(ig_t, lf_t, seq):
    rows, total = ig_t.shape
    spec = pl.BlockSpec((rows, seq), lambda n: (0, n))
    return pl.pallas_call(
        _mlstm_gate_scan_kernel,
        out_shape=[jax.ShapeDtypeStruct((rows, total), F32)] * 3,
        grid=(total // seq,),
        in_specs=[spec, spec],
        out_specs=[spec, spec, spec],
        compiler_params=_params("arbitrary"),
        name="mlstm_gate_scan",
    )(ig_t, lf_t)


def _cumsum_kernel(x_ref, o_ref):
    o_ref[...] = _scan_lanes(x_ref[...], jnp.add, 0.0)


def _cumsum_time(x_t, seq):
    rows, total = x_t.shape
    spec = pl.BlockSpec((rows, seq), lambda n: (0, n))
    return pl.pallas_call(
        _cumsum_kernel,
        out_shape=jax.ShapeDtypeStruct((rows, total), F32),
        grid=(total // seq,),
        in_specs=[spec],
        out_specs=spec,
        compiler_params=_params("arbitrary"),
        name="logf_cumsum",
    )(x_t)


def _mlstm_prompt_kernel(qk_ref, v_ref, ob_ref, rows_ref, cols_ref, cw_ref, cb_ref, nw_ref,
                         g_ref, c_ref, n_ref, mfin_ref, xbuf_ref, mprev_ref, *, tl):
    c = pl.program_id(1)
    nc = pl.num_programs(1)

    @pl.when(c == 0)
    def _():
        xbuf_ref[0:SUBLANES, :] = jnp.zeros((SUBLANES, 2 * W_B), F32)
        mprev_ref[...] = jnp.zeros_like(mprev_ref)
        c_ref[...] = jnp.zeros_like(c_ref)
        n_ref[...] = jnp.zeros_like(n_ref)

    xbuf_ref[SUBLANES:, :] = qk_ref[...]
    cw = cw_ref[...]
    y = cb_ref[...]
    for t in range(MLSTM_CONV):
        off = SUBLANES - (MLSTM_CONV - 1) + t
        y = y + cw[t:t + 1] * xbuf_ref[off:off + tl, :]
    xbuf_ref[0:SUBLANES, :] = xbuf_ref[tl:tl + SUBLANES, :]
    qk = y * _sigmoid(y)

    cols = cols_ref[...]
    mprev = mprev_ref[...]
    t_idx = lax.broadcasted_iota(jnp.int32, (tl, tl), 0)
    s_idx = lax.broadcasted_iota(jnp.int32, (tl, tl), 1)
    causal = s_idx <= t_idx
    for h in range(H_B):
        q = qk[:, h * DH_B:(h + 1) * DH_B]
        k = qk[:, W_B + h * DH_B:W_B + (h + 1) * DH_B] * (DH_B ** -0.5)
        v = v_ref[:, h * DH_B:(h + 1) * DH_B]
        qb, kb, vb = q.astype(BF16), k.astype(BF16), v.astype(BF16)
        b_row = rows_ref[h:h + 1, :]
        b_col = cols[:, h:h + 1]
        m_col = cols[:, H_B + h:H_B + h + 1]
        mrow_col = cols[:, 2 * H_B + h:2 * H_B + h + 1]
        m_last = cols[tl - 1:tl, H_B + h:H_B + h + 1]
        m_prev = mprev[:, H_B + h:H_B + h + 1]
        dmat = jnp.exp(jnp.where(causal, b_row - m_col, NEG_INF))
        s = _dot_nt(qb, kb) * dmat
        w_inter = jnp.exp(m_prev - m_col)
        cmat = c_ref[0, h]
        nvec = n_ref[0, h:h + 1, :]
        num = _dot(s.astype(BF16), vb) + w_inter * _dot(qb, cmat.astype(BF16))
        den = jnp.sum(s, axis=1, keepdims=True) + w_inter * jnp.sum(q * nvec, axis=1, keepdims=True)
        hh = num / jnp.maximum(jnp.abs(den), jnp.exp(-mrow_col))
        w_s = jnp.exp(b_col - m_last)
        w_c = jnp.exp(m_prev - m_last)
        kw = k * w_s
        c_ref[0, h] = w_c * cmat + lax.dot_general(kw.astype(BF16), vb, (((0,), (0,)), ((), ())),
                                                    preferred_element_type=F32)
        n_ref[0, h:h + 1, :] = w_c * nvec + jnp.sum(kw, axis=0, keepdims=True)
        hn = _rms(hh, nw_ref[:, h * DH_B:(h + 1) * DH_B])
        g_ref[:, h * DH_B:(h + 1) * DH_B] = _sigmoid(ob_ref[:, h * DH_B:(h + 1) * DH_B]) * hn
    mprev_ref[...] = cols[tl - 1:tl, :]

    @pl.when(c == nc - 1)
    def _():
        mfin_ref[0] = cols[tl - 1:tl, :]


def _mlstm_prompt(qk_pre, v_pre, ob, rows_pack, cols_pack, conv_w, conv_b, norm_w, nseq, seq):
    tl = MLSTM_TILE
    nc = seq // tl
    total = nseq * seq
    rmap = lambda n, c: (n * nc + c, 0)
    return pl.pallas_call(
        functools.partial(_mlstm_prompt_kernel, tl=tl),
        out_shape=[jax.ShapeDtypeStruct((total, W_B), F32),
                   jax.ShapeDtypeStruct((nseq, H_B, DH_B, DH_B), F32),
                   jax.ShapeDtypeStruct((nseq, H_B, DH_B), F32),
                   jax.ShapeDtypeStruct((nseq, 1, LANES), F32)],
        grid=(nseq, nc),
        in_specs=[pl.BlockSpec((tl, 2 * W_B), rmap),
                  pl.BlockSpec((tl, W_B), rmap),
                  pl.BlockSpec((tl, W_B), rmap),
                  pl.BlockSpec((SUBLANES, tl), lambda n, c: (0, n * nc + c)),
                  pl.BlockSpec((tl, LANES), rmap),
                  pl.BlockSpec((MLSTM_CONV, 2 * W_B), lambda n, c: (0, 0)),
                  pl.BlockSpec((1, 2 * W_B), lambda n, c: (0, 0)),
                  pl.BlockSpec((1, W_B), lambda n, c: (0, 0))],
        out_specs=[pl.BlockSpec((tl, W_B), rmap),
                   pl.BlockSpec((1, H_B, DH_B, DH_B), lambda n, c: (n, 0, 0, 0)),
                   pl.BlockSpec((1, H_B, DH_B), lambda n, c: (n, 0, 0)),
                   pl.BlockSpec((1, 1, LANES), lambda n, c: (n, 0, 0))],
        scratch_shapes=[pltpu.VMEM((tl + SUBLANES, 2 * W_B), F32), pltpu.VMEM((1, LANES), F32)],
        compiler_params=_params("arbitrary", "arbitrary"),
        name="mlstm_prompt",
    )(qk_pre, v_pre, ob, rows_pack, cols_pack, conv_w, conv_b.reshape(1, -1), norm_w.reshape(1, -1))


HEAD_DIM = 64


def _head_masks(shape):
    head = lax.broadcasted_iota(jnp.int32, shape, 1) // HEAD_DIM
    return [head == g for g in range(shape[1] // HEAD_DIM)]


def _stage_kv(k_ref, v_ref, kb_ref, vt_ref):
    kb_ref[...] = k_ref[...].astype(BF16)
    for j in range(k_ref.shape[0] // KEY_TILE):
        for c in range(k_ref.shape[1] // LANES):
            vt_ref[j, c * LANES:(c + 1) * LANES, :] = jnp.transpose(
                v_ref[j * KEY_TILE:(j + 1) * KEY_TILE, c * LANES:(c + 1) * LANES]).astype(BF16)


def _flash_group(qbs, qi, kb_ref, vt_ref, tq, adjust):
    tk = KEY_TILE
    per = tq // tk
    ng = len(qbs)
    krow = lax.broadcasted_iota(jnp.int32, (tk, ng * tq), 0)
    qcol = lax.broadcasted_iota(jnp.int32, (tk, ng * tq), 1) % tq
    q_cat = jnp.concatenate(qbs, axis=0)

    def step(j, state, diagonal):
        m, l = state[:2]
        start = pl.multiple_of(j * tk, tk)
        s = adjust(_dot_nt(kb_ref[pl.ds(start, tk), :], q_cat), j, start, diagonal is not None)
        if diagonal is not None:
            s = jnp.where(krow + diagonal * tk <= qcol, s, NEG_INF)
        m_new = jnp.maximum(m, jnp.max(s, axis=0, keepdims=True))
        alpha = jnp.exp(m - m_new)
        p = jnp.exp(s - m_new)
        l = alpha * l + jnp.sum(p, axis=0, keepdims=True)
        pb = p.astype(BF16)
        accs = tuple(alpha[:, g * tq:(g + 1) * tq] * state[2 + g]
                     + _dot(vt_ref[j, g * HEAD_DIM:(g + 1) * HEAD_DIM, :], pb[:, g * tq:(g + 1) * tq])
                     for g in range(ng))
        return (m_new, l) + accs

    state = ((jnp.full((1, ng * tq), NEG_INF, F32), jnp.zeros((1, ng * tq), F32))
             + tuple(jnp.zeros((HEAD_DIM, tq), F32) for _ in range(ng)))
    for d in range(per):
        state = step(qi * per + d, state, d)
    state = lax.fori_loop(0, qi * per, lambda j, st: step(j, st, None), state)
    l = state[1]
    return jnp.concatenate([state[2 + g] / l[:, g * tq:(g + 1) * tq] for g in range(ng)], axis=0)


def _store_heads(o_ref, o_t):
    for r in range(o_t.shape[0] // LANES):
        for c in range(o_t.shape[1] // LANES):
            o_ref[c * LANES:(c + 1) * LANES, r * LANES:(r + 1) * LANES] = jnp.transpose(
                o_t[r * LANES:(r + 1) * LANES, c * LANES:(c + 1) * LANES])


def _fox_prompt_kernel(q_ref, k_ref, v_ref, crow_ref, o_ref, kb_ref, vt_ref, cb_ref, *, tq):
    qi = pl.program_id(2)

    @pl.when(qi == 0)
    def _():
        _stage_kv(k_ref, v_ref, kb_ref, vt_ref)
        for g in range(cb_ref.shape[0]):
            for j in range(k_ref.shape[0] // LANES):
                sl = slice(j * LANES, (j + 1) * LANES)
                cb_ref[g, sl, :] = jnp.transpose(jnp.broadcast_to(crow_ref[0, g:g + 1, sl], (LANES, LANES)))

    q = q_ref[...] * (DH_C ** -0.5)

    def adjust(s, j, start, diagonal):
        return s - jnp.concatenate([cb_ref[g, pl.ds(start, KEY_TILE), :] for g in range(cb_ref.shape[0])
                                    for _ in range(tq // LANES)], axis=1)

    qbs = [jnp.where(mask, q, 0.0).astype(BF16) for mask in _head_masks(q.shape)]
    _store_heads(o_ref, _flash_group(qbs, qi, kb_ref, vt_ref, tq, adjust))


def _fox_prompt(q, k, v, crow, nseq, seq):
    tq = FOX_TILE
    nq = seq // tq
    ng = FOX_GROUP
    width = ng * HEAD_DIM
    return pl.pallas_call(
        functools.partial(_fox_prompt_kernel, tq=tq),
        out_shape=jax.ShapeDtypeStruct(q.shape, F32),
        grid=(nseq, H_C // ng, nq),
        in_specs=[pl.BlockSpec((tq, width), lambda n, p, i: (n * nq + i, p)),
                  pl.BlockSpec((seq, width), lambda n, p, i: (n, p)),
                  pl.BlockSpec((seq, width), lambda n, p, i: (n, p)),
                  pl.BlockSpec((1, ng, seq), lambda n, p, i: (p, 0, n))],
        out_specs=pl.BlockSpec((tq, width), lambda n, p, i: (n * nq + i, p)),
        scratch_shapes=[pltpu.VMEM((seq, width), BF16), pltpu.VMEM((seq // KEY_TILE, width, KEY_TILE), BF16),
                        pltpu.VMEM((ng, seq, LANES), F32)],
        compiler_params=_params("arbitrary", "arbitrary", "arbitrary"),
        name="fox_prompt",
    )(q, k, v, crow)


def _rank_rows(g, n_valid):
    nb = g.shape[0]
    r = lax.broadcasted_iota(jnp.int32, g.shape, 0)
    g = jnp.where(r < n_valid, g, NEG_INF)
    rank = jnp.zeros(g.shape, jnp.int32)
    for i in range(nb):
        gi = g[i:i + 1, :]
        rank = rank + ((gi > g) | ((gi == g) & (i < r))).astype(jnp.int32)
    return rank, r


def _moba_prompt_kernel(q_ref, k_ref, v_ref, o_ref, kb_ref, vt_ref, kmean_ref, sel_ref, *, tq, nb):
    qi = pl.program_id(2)

    @pl.when(qi == 0)
    def _():
        _stage_kv(k_ref, v_ref, kb_ref, vt_ref)
        kmean_ref[...] = jnp.zeros_like(kmean_ref)
        for j in range(nb):
            kmean_ref[j:j + 1, :] = jnp.mean(k_ref[j * MOBA_BLOCK:(j + 1) * MOBA_BLOCK, :], axis=0, keepdims=True)

    q = q_ref[...]
    km_hi, km_lo = _split_bf16(kmean_ref[...])
    qbs = []
    for hh, mask in enumerate(_head_masks(q.shape)):
        qm = jnp.where(mask, q, 0.0)
        q_hi, q_lo = _split_bf16(qm)
        gate = _dot_nt(km_hi, q_hi) + _dot_nt(km_lo, q_hi) + _dot_nt(km_hi, q_lo)
        rank, r = _rank_rows(gate, qi)
        sel_ref[hh] = ((rank < MOBA_TOPK) & (r < qi)).astype(F32)
        qbs.append((qm * (DH_A ** -0.5)).astype(BF16))

    def adjust(s, j, start, diagonal):
        if diagonal:
            return s
        blk = (j * KEY_TILE) // MOBA_BLOCK
        chosen = jnp.concatenate([sel_ref[g, pl.ds(blk, 1), :] for g in range(sel_ref.shape[0])], axis=1)
        return jnp.where(chosen > 0.5, s, NEG_INF)

    _store_heads(o_ref, _flash_group(qbs, qi, kb_ref, vt_ref, tq, adjust))


def _moba_prompt(q, k, v, nseq, seq):
    tq = ATT_TILE
    assert tq == MOBA_BLOCK and seq % MOBA_BLOCK == 0 and seq // MOBA_BLOCK >= MOBA_TOPK
    nq = seq // tq
    nb = seq // MOBA_BLOCK
    nbp = -(-nb // SUBLANES) * SUBLANES
    ng = MOBA_GROUP
    width = ng * HEAD_DIM
    return pl.pallas_call(
        functools.partial(_moba_prompt_kernel, tq=tq, nb=nb),
        out_shape=jax.ShapeDtypeStruct(q.shape, F32),
        grid=(nseq, H_A // ng, nq),
        in_specs=[pl.BlockSpec((tq, width), lambda n, p, i: (n * nq + i, p)),
                  pl.BlockSpec((seq, width), lambda n, p, i: (n, p)),
                  pl.BlockSpec((seq, width), lambda n, p, i: (n, p))],
        out_specs=pl.BlockSpec((tq, width), lambda n, p, i: (n * nq + i, p)),
        scratch_shapes=[pltpu.VMEM((seq, width), BF16), pltpu.VMEM((seq // KEY_TILE, width, KEY_TILE), BF16),
                        pltpu.VMEM((nbp, width), F32), pltpu.VMEM((ng, nbp, tq), F32)],
        compiler_params=_params("arbitrary", "arbitrary", "arbitrary"),
        name="moba_prompt",
    )(q, k, v)


def _moba_sample_select_kernel(pt_ref, q_ref, *refs, group, nblk):
    k_refs = refs[:group]
    idx_ref = refs[group]
    kmean_ref = refs[group + 1]
    g = pl.program_id(1)

    @pl.when(g == 0)
    def _():
        kmean_ref[...] = jnp.zeros_like(kmean_ref)

    r = lax.broadcasted_iota(jnp.int32, kmean_ref.shape, 0)
    pages_per_block = MOBA_BLOCK // PAGE_SIZE
    upd = jnp.zeros(kmean_ref.shape, F32)
    for i in range(group):
        pm = jnp.mean(k_refs[i][0], axis=0, keepdims=True)
        blk = (g * group + i) // pages_per_block
        upd = upd + jnp.where(r == blk, pm * (1.0 / pages_per_block), 0.0)
    kmean_ref[...] += upd

    @pl.when(g == pl.num_programs(1) - 1)
    def _():
        prod = kmean_ref[...] * q_ref[0]
        seg = (lax.broadcasted_iota(jnp.int32, (W_A, LANES), 0) // DH_A
               == lax.broadcasted_iota(jnp.int32, (W_A, LANES), 1)).astype(BF16)
        p0 = prod.astype(BF16)
        r1 = prod - p0.astype(F32)
        p1 = r1.astype(BF16)
        p2 = (r1 - p1.astype(F32)).astype(BF16)
        gate = _dot(p0, seg) + _dot(p1, seg) + _dot(p2, seg)
        rank, rr = _rank_rows(gate, nblk)
        out = jnp.zeros((SUBLANES, LANES), jnp.int32)
        orow = lax.broadcasted_iota(jnp.int32, (SUBLANES, LANES), 0)
        for t in range(MOBA_TOPK):
            it = jnp.sum(jnp.where(rank == t, rr, 0), axis=0, keepdims=True)
            out = jnp.where(orow == t, it, out)
        idx_ref[0] = out


def _moba_sample_select(q, pool_k, page_table, nblk):
    nb = q.shape[0]
    group = PAGES_PER_STEP
    npg = nblk * (MOBA_BLOCK // PAGE_SIZE)
    assert npg % group == 0 and nblk % SUBLANES == 0
    in_specs = [pl.BlockSpec((1, 1, W_A), lambda n, g, pt: (n, 0, 0))]
    for i in range(group):
        in_specs.append(pl.BlockSpec((1, PAGE_SIZE, W_A), lambda n, g, pt, i=i: (pt[n, g * group + i], 0, 0)))
    return pl.pallas_call(
        functools.partial(_moba_sample_select_kernel, group=group, nblk=nblk),
        out_shape=jax.ShapeDtypeStruct((nb, SUBLANES, LANES), jnp.int32),
        grid_spec=pltpu.PrefetchScalarGridSpec(
            num_scalar_prefetch=1,
            grid=(nb, npg // group),
            in_specs=in_specs,
            out_specs=pl.BlockSpec((1, SUBLANES, LANES), lambda n, g, pt: (n, 0, 0)),
            scratch_shapes=[pltpu.VMEM((nblk, W_A), F32)]),
        compiler_params=_params("arbitrary", "arbitrary"),
        name="moba_sample_select",
    )(page_table, q.reshape(nb, 1, W_A), *([pool_k] * group))


def _moba_sample_attend_kernel(pt_ref, idx_ref, q_ref, kn_ref, vn_ref, *refs, n_sel, n_own, qblk):
    n_pages = n_sel + n_own
    k_refs = refs[:n_pages]
    v_refs = refs[n_pages:2 * n_pages]
    o_ref = refs[2 * n_pages]
    n = pl.program_id(0)
    h = pl.program_id(1)
    lane = lax.broadcasted_iota(jnp.int32, (1, LANES), 1)
    mine = (lane // DH_A) == (h % 2)
    q = jnp.where(mine, q_ref[0], 0.0)
    qb = jnp.broadcast_to(q * (DH_A ** -0.5), (SUBLANES, LANES)).astype(BF16)
    pages_per_block = MOBA_BLOCK // PAGE_SIZE
    logits = []
    for i in range(n_pages):
        s = _dot_nt(qb, k_refs[i][0].astype(BF16))[0:1]
        if i < n_sel:
            valid = idx_ref[n, h, i // pages_per_block] < qblk
            s = jnp.where(valid, s, NEG_INF)
        logits.append(s)
    s_new = jnp.sum(q * kn_ref[0], axis=1, keepdims=True) * (DH_A ** -0.5)
    m = s_new
    for s in logits:
        m = jnp.maximum(m, jnp.max(s, axis=1, keepdims=True))
    p_new = jnp.exp(s_new - m)
    l = p_new
    acc = p_new * vn_ref[0]
    for i, s in enumerate(logits):
        p = jnp.exp(s - m)
        l = l + jnp.sum(p, axis=1, keepdims=True)
        pb = jnp.broadcast_to(p, (SUBLANES, PAGE_SIZE)).astype(BF16)
        acc = acc + _dot(pb, v_refs[i][0].astype(BF16))[0:1]
    res = acc / l

    @pl.when(h % 2 == 0)
    def _():
        o_ref[0] = jnp.where(mine, res, 0.0)

    @pl.when(h % 2 == 1)
    def _():
        o_ref[0] = jnp.where(mine, res, o_ref[0])


def _moba_sample_attend(q, k_new, v_new, pool_k, pool_v, page_table, idx):
    nb = q.shape[0]
    n_pages_total = page_table.shape[1]
    pages_per_block = MOBA_BLOCK // PAGE_SIZE
    past = n_pages_total * PAGE_SIZE
    qblk = past // MOBA_BLOCK
    n_sel = MOBA_TOPK * pages_per_block
    n_own = 1 if (past - PAGE_SIZE) // MOBA_BLOCK == qblk else 0

    def sel_map(i):
        def index_map(n, h, pt, ix):
            lpage = jnp.minimum(ix[n, h, i // pages_per_block] * pages_per_block + i % pages_per_block,
                                n_pages_total - 1)
            return (pt[n, lpage], 0, h // 2)
        return index_map

    def own_map(n, h, pt, ix):
        return (pt[n, n_pages_total - 1], 0, h // 2)

    page_specs = [pl.BlockSpec((1, PAGE_SIZE, LANES), sel_map(i)) for i in range(n_sel)]
    page_specs += [pl.BlockSpec((1, PAGE_SIZE, LANES), own_map)] * n_own
    tok_spec = pl.BlockSpec((1, 1, LANES), lambda n, h, pt, ix: (n, 0, h // 2))
    tok = lambda a: a.reshape(nb, 1, W_A)
    n_pg = n_sel + n_own
    return pl.pallas_call(
        functools.partial(_moba_sample_attend_kernel, n_sel=n_sel, n_own=n_own, qblk=qblk),
        out_shape=jax.ShapeDtypeStruct((nb, 1, W_A), F32),
        grid_spec=pltpu.PrefetchScalarGridSpec(
            num_scalar_prefetch=2,
            grid=(nb, H_A),
            in_specs=[tok_spec, tok_spec, tok_spec] + page_specs + page_specs,
            out_specs=tok_spec),
        compiler_params=_params("arbitrary", "arbitrary"),
        name="moba_sample_attend",
    )(page_table, idx, tok(q), tok(k_new), tok(v_new), *([pool_k] * n_pg), *([pool_v] * n_pg)).reshape(nb, W_A)


def _mlstm_sample_kernel(qk_ref, st_ref, v_ref, ob_ref, gate_ref, c_ref, n_ref, m_ref, cw_ref, cb_ref, nw_ref,
                         g_ref, cn_ref, nn_ref, mn_ref):
    cw = cw_ref[...]
    st = st_ref[0]
    y = cb_ref[...] + cw[MLSTM_CONV - 1:MLSTM_CONV] * qk_ref[0]
    for t in range(MLSTM_CONV - 1):
        y = y + cw[t:t + 1] * st[t:t + 1]
    qk = y * _sigmoid(y)
    gates = gate_ref[0]
    m_in = m_ref[0]
    m_out = jnp.zeros((1, LANES), F32)
    lane = lax.broadcasted_iota(jnp.int32, (1, LANES), 1)
    for h in range(H_B):
        q = qk[:, h * DH_B:(h + 1) * DH_B]
        k = qk[:, W_B + h * DH_B:W_B + (h + 1) * DH_B] * (DH_B ** -0.5)
        v = v_ref[0][:, h * DH_B:(h + 1) * DH_B]
        ig = gates[:, h:h + 1]
        lf = gates[:, H_B + h:H_B + h + 1]
        m_old = m_in[:, h:h + 1]
        m_row = jnp.maximum(ig, lf + m_old)
        w_inter = jnp.exp(lf + m_old - m_row)
        w_s = jnp.exp(ig - m_row)
        s = jnp.sum(q * k, axis=1, keepdims=True) * w_s
        cmat = c_ref[0, h]
        nvec = n_ref[0, h:h + 1, :]
        qc = _dot(jnp.broadcast_to(q, (SUBLANES, DH_B)).astype(BF16), cmat.astype(BF16))[0:1]
        num = s * v + w_inter * qc
        den = s + w_inter * jnp.sum(q * nvec, axis=1, keepdims=True)
        hh = num / jnp.maximum(jnp.abs(den), jnp.exp(-m_row))
        k_col = jnp.transpose(jnp.broadcast_to(k, (DH_B, DH_B)))
        cn_ref[0, h] = w_inter * cmat + w_s * (k_col * v)
        nn_ref[0, h:h + 1, :] = w_inter * nvec + w_s * k
        m_out = jnp.where(lane == h, m_row, m_out)
        hn = _rms(hh, nw_ref[:, h * DH_B:(h + 1) * DH_B])
        g_ref[0, :, h * DH_B:(h + 1) * DH_B] = _sigmoid(ob_ref[0][:, h * DH_B:(h + 1) * DH_B]) * hn
    mn_ref[0] = m_out


def _mlstm_sample(qk_pre, conv_state, v_pre, ob, gates, c, nvec, m, conv_w, conv_b, norm_w):
    nb = qk_pre.shape[0]
    tok = lambda a: a.reshape(nb, 1, a.shape[-1])
    tspec = lambda w: pl.BlockSpec((1, 1, w), lambda n: (n, 0, 0))
    full = lambda shape: pl.BlockSpec(shape, lambda n: (0,) * len(shape))
    g, cn, nn, mn = pl.pallas_call(
        _mlstm_sample_kernel,
        out_shape=[jax.ShapeDtypeStruct((nb, 1, W_B), F32),
                   jax.ShapeDtypeStruct((nb, H_B, DH_B, DH_B), F32),
                   jax.ShapeDtypeStruct((nb, H_B, DH_B), F32),
                   jax.ShapeDtypeStruct((nb, 1, LANES), F32)],
        grid=(nb,),
        in_specs=[tspec(2 * W_B),
                  pl.BlockSpec((1, MLSTM_CONV - 1, 2 * W_B), lambda n: (n, 0, 0)),
                  tspec(W_B), tspec(W_B), tspec(LANES),
                  pl.BlockSpec((1, H_B, DH_B, DH_B), lambda n: (n, 0, 0, 0)),
                  pl.BlockSpec((1, H_B, DH_B), lambda n: (n, 0, 0)),
                  tspec(LANES),
                  full((MLSTM_CONV, 2 * W_B)), full((1, 2 * W_B)), full((1, W_B))],
        out_specs=[tspec(W_B),
                   pl.BlockSpec((1, H_B, DH_B, DH_B), lambda n: (n, 0, 0, 0)),
                   pl.BlockSpec((1, H_B, DH_B), lambda n: (n, 0, 0)),
                   tspec(LANES)],
        compiler_params=_params("arbitrary"),
        name="mlstm_sample",
    )(tok(qk_pre), conv_state, tok(v_pre), tok(ob), tok(gates), c, nvec, tok(_pad_lanes(m)),
      conv_w, conv_b.reshape(1, -1), norm_w.reshape(1, -1))
    return g.reshape(nb, W_B), cn, nn, mn[:, 0, :H_B]


def _fox_sample_kernel(pt_ref, q_ref, kn_ref, vn_ref, lfn_ref, *refs, group):
    k_refs = refs[:group]
    v_refs = refs[group:2 * group]
    lf_refs = refs[2 * group:3 * group]
    o_ref = refs[3 * group]
    m_ref, l_ref, run_ref, acc_ref = refs[3 * group + 1:]
    g = pl.program_id(1)

    @pl.when(g == 0)
    def _():
        m_ref[...] = jnp.full(m_ref.shape, NEG_INF, F32)
        l_ref[...] = jnp.zeros_like(l_ref)
        run_ref[...] = jnp.zeros_like(run_ref)
        acc_ref[...] = jnp.zeros_like(acc_ref)

    head = lax.broadcasted_iota(jnp.int32, (H_C, W_C), 0)
    mine = (lax.broadcasted_iota(jnp.int32, (H_C, W_C), 1) // DH_C) == head
    q_bd = jnp.where(mine, jnp.broadcast_to(q_ref[0], (H_C, W_C)), 0.0) * (DH_C ** -0.5)
    q_bf = q_bd.astype(BF16)
    m, l, run, acc = m_ref[...], l_ref[...], run_ref[...], acc_ref[...]
    for i in range(group):
        s = _dot_nt(q_bf, k_refs[i][0].astype(BF16))
        cum = run + _scan_lanes(lf_refs[i][0], jnp.add, 0.0)
        run = cum[:, PAGE_SIZE - 1:PAGE_SIZE]
        s = s - cum
        m_new = jnp.maximum(m, jnp.max(s, axis=1, keepdims=True))
        alpha = jnp.exp(m - m_new)
        p = jnp.exp(s - m_new)
        l = alpha * l + jnp.sum(p, axis=1, keepdims=True)
        acc = alpha * acc + _dot(p.astype(BF16), v_refs[i][0].astype(BF16))
        m = m_new
    m_ref[...], l_ref[...], run_ref[...], acc_ref[...] = m, l, run, acc

    @pl.when(g == pl.num_programs(1) - 1)
    def _():
        hrow = lax.broadcasted_iota(jnp.int32, (H_C, LANES), 0)
        hlane = lax.broadcasted_iota(jnp.int32, (H_C, LANES), 1)
        lf_new = jnp.sum(jnp.where(hrow == hlane, jnp.broadcast_to(lfn_ref[0], (H_C, LANES)), 0.0),
                         axis=1, keepdims=True)
        s_new = jnp.sum(q_bd * kn_ref[0], axis=1, keepdims=True) - (run + lf_new)
        m_f = jnp.maximum(m, s_new)
        alpha = jnp.exp(m - m_f)
        p_new = jnp.exp(s_new - m_f)
        l_f = alpha * l + p_new
        acc_f = alpha * acc + p_new * vn_ref[0]
        o_ref[0] = jnp.sum(jnp.where(mine, acc_f / l_f, 0.0), axis=0, keepdims=True)


def _fox_sample(q, k_new, v_new, lf_new, pool_k, pool_v, pool_lf_t, page_table):
    nb = q.shape[0]
    n_pages = page_table.shape[1]
    group = PAGES_PER_STEP
    assert n_pages % group == 0
    tok_spec = pl.BlockSpec((1, 1, W_C), lambda n, g, pt: (n, 0, 0))
    tok = lambda a: a.reshape(nb, 1, a.shape[-1])
    page = lambda i, shape: pl.BlockSpec(shape, lambda n, g, pt: (pt[n, g * group + i], 0, 0))
    in_specs = [tok_spec, tok_spec, tok_spec, pl.BlockSpec((1, 1, LANES), lambda n, g, pt: (n, 0, 0))]
    in_specs += [page(i, (1, PAGE_SIZE, W_C)) for i in range(group)]
    in_specs += [page(i, (1, PAGE_SIZE, W_C)) for i in range(group)]
    in_specs += [page(i, (1, H_C, PAGE_SIZE)) for i in range(group)]
    return pl.pallas_call(
        functools.partial(_fox_sample_kernel, group=group),
        out_shape=jax.ShapeDtypeStruct((nb, 1, W_C), F32),
        grid_spec=pltpu.PrefetchScalarGridSpec(
            num_scalar_prefetch=1,
            grid=(nb, n_pages // group),
            in_specs=in_specs,
            out_specs=tok_spec,
            scratch_shapes=[pltpu.VMEM((H_C, 1), F32), pltpu.VMEM((H_C, 1), F32), pltpu.VMEM((H_C, 1), F32),
                            pltpu.VMEM((H_C, W_C), F32)]),
        compiler_params=_params("arbitrary", "arbitrary"),
        name="fox_sample",
    )(page_table, tok(q), tok(k_new), tok(v_new), tok(lf_new),
      *([pool_k] * group), *([pool_v] * group), *([pool_lf_t] * group)).reshape(nb, W_C)


EVEN_SEGS = ((0, W_A), (W_A, W_A), (2 * W_A, W_A), (3 * W_A, 2 * W_B), (3 * W_A + 2 * W_B, W_B),
             (3 * W_A + 3 * W_B, W_B), (3 * W_A + 4 * W_B, LANES))
ODD_SEGS = ((0, W_C), (W_C, W_C), (2 * W_C, W_C), (3 * W_C, LANES))


def _even_weights(w_in, b_i, b_f):
    main = 3 * W_A + 4 * W_B
    w_pad = jnp.concatenate([w_in[:, :main], _pad_lanes(w_in[:, main:])], axis=1)
    w_bf = w_pad.astype(BF16)
    wq = w_in[:, :W_A]
    w_lo = (wq - wq.astype(BF16).astype(F32)).astype(BF16)
    gate_bias = _pad_lanes(jnp.concatenate([b_i, b_f]).reshape(1, -1))
    return w_bf, w_lo, gate_bias


def _odd_weights(w_in, b_f):
    w_pad = jnp.concatenate([w_in[:, :3 * W_C], _pad_lanes(w_in[:, 3 * W_C:])], axis=1)
    return w_pad.astype(BF16), _pad_lanes(b_f.reshape(1, -1))


def _rows_to_sublanes(a):
    return jnp.pad(a.T, ((0, SUBLANES - a.shape[1]), (0, 0)))


def kernel(x_prompt, x_sample, cache_moba_k, cache_moba_v, state_mlstm_c, state_mlstm_n, state_mlstm_m,
           state_mlstm_conv, cache_fox_k, cache_fox_v, cache_fox_logf, state_ffn_conv, page_table, norm_w,
           even_w_in, mlstm_conv_w, mlstm_conv_b, mlstm_b_i, mlstm_b_f, mlstm_norm_w, even_w_out, fox_w_in,
           fox_b_f, fox_w_out, ffn_w_in, ffn_conv_w, ffn_conv_b, ffn_w_out):
    bp, sp, d = x_prompt.shape
    bs, ss, _ = x_sample.shape
    assert ss == 1 and d == D_MODEL
    depth = norm_w.shape[0]
    n_pool = cache_fox_k.shape[1] if cache_fox_k.shape[0] else cache_moba_k.shape[1]
    n_pages = page_table.shape[1]
    tp = bp * sp
    yp = x_prompt.reshape(tp, d)
    ys = x_sample.reshape(bs, d)
    outs = {name: [] for name in (
        "mk_p", "mv_p", "mk_s", "mv_s", "mc_p", "mn_p", "mm_p", "mconv_p", "mc_s", "mn_s", "mm_s", "mconv_s",
        "fk_p", "fv_p", "fl_p", "fk_s", "fv_s", "fl_s", "ff_p", "ff_s")}
    for layer in range(depth):
        nw = norm_w[layer]
        if layer % 2 == 0:
            e = layer // 2
            w_bf, w_lo, gate_bias = _even_weights(even_w_in[e], mlstm_b_i[e], mlstm_b_f[e])
            w_out_bf = even_w_out[e].astype(BF16)
            proj = lambda x, tm: _rms_proj(x, nw[0], w_bf, w_lo, gate_bias, EVEN_SEGS, 1, (H_B, 2 * H_B), tm)
            qa, ka, va, qkb, vb, ob, gates = proj(yp, ROW_TILE)
            att = _moba_prompt(qa, ka, va, bp, sp)
            b_t, m_t, mrow_t = _mlstm_gate_scan(_rows_to_sublanes(gates[:, :H_B]),
                                                _rows_to_sublanes(gates[:, H_B:2 * H_B]), sp)
            cols = _pad_lanes(jnp.concatenate([b_t[:H_B].T, m_t[:H_B].T, mrow_t[:H_B].T], axis=1))
            gated, cc, nn, mfin = _mlstm_prompt(qkb, vb, ob, b_t, cols, mlstm_conv_w[e], mlstm_conv_b[e],
                                                mlstm_norm_w[e], bp, sp)
            yp = _proj_res([att, gated], [w_out_bf[:W_A], w_out_bf[W_A:]], yp, nw[1], FFN_ROWS)
            outs["mk_p"].append(ka.reshape(bp, sp, H_A, DH_A))
            outs["mv_p"].append(va.reshape(bp, sp, H_A, DH_A))
            outs["mc_p"].append(cc)
            outs["mn_p"].append(nn)
            outs["mm_p"].append(mfin[:, 0, 2 * H_B:3 * H_B])
            outs["mconv_p"].append(qkb.reshape(bp, sp, 2 * W_B)[:, sp - (MLSTM_CONV - 1):])
            qa, ka, va, qkb, vb, ob, gates = proj(ys, bs)
            pool_k = cache_moba_k[e].reshape(n_pool, PAGE_SIZE, W_A)
            pool_v = cache_moba_v[e].reshape(n_pool, PAGE_SIZE, W_A)
            nblk = (n_pages * PAGE_SIZE) // MOBA_BLOCK
            assert nblk >= MOBA_TOPK
            idx = _moba_sample_select(qa, pool_k, page_table, nblk)
            idx = jnp.transpose(idx[:, :MOBA_TOPK, :H_A], (0, 2, 1))
            att = _moba_sample_attend(qa, ka, va, pool_k, pool_v, page_table, idx)
            gated, cc, nn, mm = _mlstm_sample(qkb, state_mlstm_conv[e], vb, ob, gates, state_mlstm_c[e],
                                              state_mlstm_n[e], state_mlstm_m[e], mlstm_conv_w[e],
                                              mlstm_conv_b[e], mlstm_norm_w[e])
            ys = _proj_res([att, gated], [w_out_bf[:W_A], w_out_bf[W_A:]], ys, nw[1], bs)
            outs["mk_s"].append(ka.reshape(bs, ss, H_A, DH_A))
            outs["mv_s"].append(va.reshape(bs, ss, H_A, DH_A))
            outs["mc_s"].append(cc)
            outs["mn_s"].append(nn)
            outs["mm_s"].append(mm)
            outs["mconv_s"].append(jnp.concatenate([state_mlstm_conv[e][:, 1:], qkb[:, None, :]], axis=1))
        else:
            o = layer // 2
            w_bf, gate_bias = _odd_weights(fox_w_in[o], fox_b_f[o])
            w_out_bf = fox_w_out[o].astype(BF16)
            proj = lambda x, tm: _rms_proj(x, nw[0], w_bf, None, gate_bias, ODD_SEGS, 0, (0, H_C), tm)
            q, k, v, lf = proj(yp, ROW_TILE)
            lf_t = lf[:, :H_C].T
            crow = _cumsum_time(lf_t, sp).reshape(H_C // FOX_GROUP, FOX_GROUP, tp)
            att = _fox_prompt(q, k, v, crow, bp, sp)
            yp = _proj_res([att], [w_out_bf], yp, nw[1], FFN_ROWS)
            outs["fk_p"].append(k.reshape(bp, sp, H_C, DH_C))
            outs["fv_p"].append(v.reshape(bp, sp, H_C, DH_C))
            outs["fl_p"].append(lf[:, :H_C].reshape(bp, sp, H_C))
            q, k, v, lf = proj(ys, bs)
            pool_k = cache_fox_k[o].reshape(n_pool, PAGE_SIZE, W_C)
            pool_v = cache_fox_v[o].reshape(n_pool, PAGE_SIZE, W_C)
            pool_lf_t = jnp.transpose(cache_fox_logf[o], (0, 2, 1))
            att = _fox_sample(q, k, v, lf, pool_k, pool_v, pool_lf_t, page_table)
            ys = _proj_res([att], [w_out_bf], ys, nw[1], bs)
            outs["fk_s"].append(k.reshape(bs, ss, H_C, DH_C))
            outs["fv_s"].append(v.reshape(bs, ss, H_C, DH_C))
            outs["fl_s"].append(lf[:, :H_C].reshape(bs, ss, H_C))
        w_in_bf = ffn_w_in[layer].astype(BF16)
        w_o_bf = ffn_w_out[layer].astype(BF16)
        yp, buf_p = _ffn(yp, nw[2], w_in_bf, ffn_conv_w[layer], ffn_conv_b[layer], w_o_bf, nw[3], FFN_ROWS, sp)
        ys, u_s = _ffn(ys, nw[2], w_in_bf, ffn_conv_w[layer], ffn_conv_b[layer], w_o_bf, nw[3], bs, 1,
                       state=state_ffn_conv[layer])
        outs["ff_p"].append(buf_p)
        outs["ff_s"].append(jnp.concatenate([state_ffn_conv[layer][:, 1:], u_s[:, None, :]], axis=1))
    st = {name: jnp.stack(vals) for name, vals in outs.items()}
    return (yp.reshape(bp, sp, d), ys.reshape(bs, ss, d), st["mk_p"], st["mv_p"], st["mk_s"], st["mv_s"],
            st["mc_p"], st["mn_p"], st["mm_p"], st["mconv_p"], st["mc_s"], st["mn_s"], st["mm_s"], st["mconv_s"],
            st["fk_p"], st["fv_p"], st["fl_p"], st["fk_s"], st["fv_s"], st["fl_s"], st["ff_p"], st["ff_s"])
```

```python
import functools

import jax
import jax.numpy as jnp
from jax import lax
from jax.experimental import pallas as pl
from jax.experimental.pallas import tpu as pltpu

F32 = jnp.float32
BF16 = jnp.bfloat16
NEG_INF = float("-inf")

D_MODEL = 1024
PAGE_SIZE = 128
H_A, DH_A = 8, 64
W_A = H_A * DH_A
MOBA_BLOCK = 256
MOBA_TOPK = 3
H_B, DH_B = 4, 128
W_B = H_B * DH_B
MLSTM_CONV = 4
H_C, DH_C = 16, 64
W_C = H_C * DH_C
D_FF = 2816
FFN_CONV = 3
RMS_EPS = 1e-6

LANES = 128
SUBLANES = 8
VMEM_LIMIT = 56 * 1024 * 1024

ROW_TILE = 256
ATT_TILE = 256
FOX_TILE = 512
KEY_TILE = 256
MOBA_GROUP = 4
FOX_GROUP = 2
MLSTM_TILE = 256
FFN_ROWS = 512
FFN_COLS = 1408
FFN_SLICE = 256
PAGES_PER_STEP = 8


def _params(*sem):
    return pltpu.CompilerParams(dimension_semantics=sem, vmem_limit_bytes=VMEM_LIMIT)


def _rms(x, w):
    return x * lax.rsqrt(jnp.mean(x * x, axis=-1, keepdims=True) + RMS_EPS) * w


def _sigmoid(x):
    return 1.0 / (1.0 + jnp.exp(-x))


def _log_sigmoid(x):
    return -(jnp.maximum(-x, 0.0) + jnp.log1p(jnp.exp(-jnp.abs(x))))


def _gelu_tanh(x):
    c = 0.7978845608028654
    return 0.5 * x * (1.0 + jnp.tanh(c * (x + 0.044715 * (x * x * x))))


def _dot(a, b):
    return jnp.dot(a, b, preferred_element_type=F32)


def _dot_nt(a, b):
    return lax.dot_general(a, b, (((1,), (1,)), ((), ())), preferred_element_type=F32)


def _split_bf16(x):
    hi = x.astype(BF16)
    lo = (x - hi.astype(F32)).astype(BF16)
    return hi, lo


def _pad_lanes(a, width=LANES):
    return jnp.pad(a, ((0, 0),) * (a.ndim - 1) + ((0, width - a.shape[-1]),))


def _rms_proj_kernel(*refs, segs, n_hi, gate_lf):
    x_ref, nw_ref, w_ref = refs[:3]
    pos = 3
    wlo_ref = None
    if n_hi:
        wlo_ref = refs[pos]
        pos += 1
    gb_ref = refs[pos]
    out_refs = refs[pos + 1:]
    h = _rms(x_ref[...], nw_ref[...])
    hb, hl = _split_bf16(h)
    lo_start = 0
    for i, (start, width) in enumerate(segs):
        w = w_ref[:, start:start + width]
        z = _dot(hb, w)
        if i < n_hi:
            z = z + _dot(hl, w) + _dot(hb, wlo_ref[:, lo_start:lo_start + width])
            lo_start += width
        if i == len(segs) - 1:
            z = z + gb_ref[...]
            lane = lax.broadcasted_iota(jnp.int32, z.shape, 1)
            z = jnp.where((lane >= gate_lf[0]) & (lane < gate_lf[1]), _log_sigmoid(z), z)
        out_refs[i][...] = z


def _rms_proj(x, nw, w_bf, w_lo, gate_bias, segs, n_hi, gate_lf, tm):
    rows, d = x.shape
    ncols = w_bf.shape[1]
    in_specs = [pl.BlockSpec((tm, d), lambda i: (i, 0)),
                pl.BlockSpec((1, d), lambda i: (0, 0)),
                pl.BlockSpec((d, ncols), lambda i: (0, 0))]
    args = [x, nw.reshape(1, d), w_bf]
    if n_hi:
        in_specs.append(pl.BlockSpec(w_lo.shape, lambda i: (0, 0)))
        args.append(w_lo)
    in_specs.append(pl.BlockSpec((1, LANES), lambda i: (0, 0)))
    args.append(gate_bias)
    return pl.pallas_call(
        functools.partial(_rms_proj_kernel, segs=segs, n_hi=n_hi, gate_lf=gate_lf),
        out_shape=[jax.ShapeDtypeStruct((rows, wd), F32) for _, wd in segs],
        grid=(rows // tm,),
        in_specs=in_specs,
        out_specs=[pl.BlockSpec((tm, wd), lambda i: (i, 0)) for _, wd in segs],
        compiler_params=_params("arbitrary"),
        name="rms_proj",
    )(*args)


def _proj_res_kernel(*refs, n_in):
    a_refs = refs[:n_in]
    w_refs = refs[n_in:2 * n_in]
    x_ref, nw_ref, o_ref = refs[2 * n_in:]
    acc = None
    for a_ref, w_ref in zip(a_refs, w_refs):
        z = _dot(a_ref[...].astype(BF16), w_ref[...])
        acc = z if acc is None else acc + z
    o_ref[...] = x_ref[...] + _rms(acc, nw_ref[...])


def _proj_res(acts, ws, x, nw, tm):
    rows, d = x.shape
    n_in = len(acts)
    in_specs = ([pl.BlockSpec((tm, a.shape[1]), lambda i: (i, 0)) for a in acts]
                + [pl.BlockSpec(w.shape, lambda i: (0, 0)) for w in ws]
                + [pl.BlockSpec((tm, d), lambda i: (i, 0)), pl.BlockSpec((1, d), lambda i: (0, 0))])
    return pl.pallas_call(
        functools.partial(_proj_res_kernel, n_in=n_in),
        out_shape=jax.ShapeDtypeStruct((rows, d), F32),
        grid=(rows // tm,),
        in_specs=in_specs,
        out_specs=pl.BlockSpec((tm, d), lambda i: (i, 0)),
        compiler_params=_params("arbitrary"),
        name="proj_res",
    )(*acts, *ws, x, nw.reshape(1, d))


def _ffn_kernel(*refs, tm, tf, nj, tiles_per_seq, stateful):
    (x_ref, nw_in_ref, wa_ref, wb_ref, cwa_ref, cwb_ref, cba_ref, cbb_ref, wo_ref, nw_out_ref) = refs[:10]
    pos = 10
    if stateful:
        s0a_ref, s0b_ref, s1a_ref, s1b_ref = refs[pos:pos + 4]
        pos += 4
    y_ref, oa_ref, ob_ref = refs[pos:pos + 3]
    pos += 3
    hb_ref, acc_ref = refs[pos:pos + 2]
    pos += 2
    if not stateful:
        ubuf_ref, carry_ref = refs[pos:pos + 2]
    i = pl.program_id(0)
    j = pl.program_id(1)

    @pl.when(j == 0)
    def _():
        hb_ref[...] = _rms(x_ref[...], nw_in_ref[...]).astype(BF16)
        acc_ref[...] = jnp.zeros_like(acc_ref)

    hb = hb_ref[...]
    if not stateful:
        first = i % tiles_per_seq == 0

        @pl.when(first)
        def _():
            ubuf_ref[:, 0:SUBLANES, :] = jnp.zeros((2, SUBLANES, tf), F32)

        @pl.when(jnp.logical_not(first))
        def _():
            ubuf_ref[:, 0:SUBLANES, :] = carry_ref[:, j]

    gs = []
    for c0 in range(0, tf, FFN_SLICE):
        cs = slice(c0, min(c0 + FFN_SLICE, tf))
        halves = []
        for half, (w_ref, cw_ref, cb_ref) in enumerate(((wa_ref, cwa_ref, cba_ref), (wb_ref, cwb_ref, cbb_ref))):
            u = _dot(hb, w_ref[:, cs])
            cw = cw_ref[:, cs]
            if stateful:
                s0 = (s0a_ref, s0b_ref)[half][:, cs]
                s1 = (s1a_ref, s1b_ref)[half][:, cs]
                y = cw[0:1] * s0 + cw[1:2] * s1 + cw[2:3] * u + cb_ref[:, cs]
                (oa_ref, ob_ref)[half][:, cs] = u
            else:
                ubuf_ref[half, SUBLANES:, cs] = u
                y = (cw[0:1] * ubuf_ref[half, SUBLANES - 2:SUBLANES - 2 + tm, cs]
                     + cw[1:2] * ubuf_ref[half, SUBLANES - 1:SUBLANES - 1 + tm, cs]
                     + cw[2:3] * u + cb_ref[:, cs])
            halves.append(y)
        gs.append((_gelu_tanh(halves[0]) * halves[1]).astype(BF16))
    if not stateful:
        carry_ref[:, j] = ubuf_ref[:, tm:tm + SUBLANES, :]
        oa_ref[0] = ubuf_ref[0, tm + SUBLANES - 2:tm + SUBLANES, :]
        ob_ref[0] = ubuf_ref[1, tm + SUBLANES - 2:tm + SUBLANES, :]
    acc_ref[...] += _dot(jnp.concatenate(gs, axis=1), wo_ref[...])

    @pl.when(j == nj - 1)
    def _():
        y_ref[...] = x_ref[...] + _rms(acc_ref[...], nw_out_ref[...])


def _ffn(x, nw_in, w_in_bf, conv_w, conv_b, w_out_bf, nw_out, tm, seq_rows, state=None):
    rows, d = x.shape
    tf = FFN_COLS
    nj = D_FF // tf
    stateful = state is not None
    cb2 = conv_b.reshape(1, 2 * D_FF)
    in_specs = [pl.BlockSpec((tm, d), lambda i, j: (i, 0)),
                pl.BlockSpec((1, d), lambda i, j: (0, 0)),
                pl.BlockSpec((d, tf), lambda i, j: (0, j)),
                pl.BlockSpec((d, tf), lambda i, j: (0, nj + j)),
                pl.BlockSpec((FFN_CONV, tf), lambda i, j: (0, j)),
                pl.BlockSpec((FFN_CONV, tf), lambda i, j: (0, nj + j)),
                pl.BlockSpec((1, tf), lambda i, j: (0, j)),
                pl.BlockSpec((1, tf), lambda i, j: (0, nj + j)),
                pl.BlockSpec((tf, d), lambda i, j: (j, 0)),
                pl.BlockSpec((1, d), lambda i, j: (0, 0))]
    args = [x, nw_in.reshape(1, d), w_in_bf, w_in_bf, conv_w, conv_w, cb2, cb2, w_out_bf, nw_out.reshape(1, d)]
    scratch = [pltpu.VMEM((tm, d), BF16), pltpu.VMEM((tm, d), F32)]
    if stateful:
        s0, s1 = state[:, 0, :], state[:, 1, :]
        in_specs += [pl.BlockSpec((tm, tf), lambda i, j: (i, j)), pl.BlockSpec((tm, tf), lambda i, j: (i, nj + j)),
                     pl.BlockSpec((tm, tf), lambda i, j: (i, j)), pl.BlockSpec((tm, tf), lambda i, j: (i, nj + j))]
        args += [s0, s0, s1, s1]
        out_shape = [jax.ShapeDtypeStruct((rows, d), F32), jax.ShapeDtypeStruct((rows, D_FF), F32),
                     jax.ShapeDtypeStruct((rows, D_FF), F32)]
        out_specs = [pl.BlockSpec((tm, d), lambda i, j: (i, 0)), pl.BlockSpec((tm, tf), lambda i, j: (i, j)),
                     pl.BlockSpec((tm, tf), lambda i, j: (i, j))]
        tiles_per_seq = 1
    else:
        tiles_per_seq = seq_rows // tm
        ntiles = rows // tm
        out_shape = [jax.ShapeDtypeStruct((rows, d), F32), jax.ShapeDtypeStruct((ntiles, FFN_CONV - 1, D_FF), F32),
                     jax.ShapeDtypeStruct((ntiles, FFN_CONV - 1, D_FF), F32)]
        out_specs = [pl.BlockSpec((tm, d), lambda i, j: (i, 0)),
                     pl.BlockSpec((1, FFN_CONV - 1, tf), lambda i, j: (i, 0, j)),
                     pl.BlockSpec((1, FFN_CONV - 1, tf), lambda i, j: (i, 0, j))]
        scratch += [pltpu.VMEM((2, tm + SUBLANES, tf), F32), pltpu.VMEM((2, nj, SUBLANES, tf), F32)]
    y, ua, ub = pl.pallas_call(
        functools.partial(_ffn_kernel, tm=tm, tf=tf, nj=nj, tiles_per_seq=tiles_per_seq, stateful=stateful),
        out_shape=out_shape,
        grid=(rows // tm, nj),
        in_specs=in_specs,
        out_specs=out_specs,
        scratch_shapes=scratch,
        compiler_params=_params("arbitrary", "arbitrary"),
        name="conv_ffn",
    )(*args)
    if not stateful:
        ua, ub = ua[tiles_per_seq - 1::tiles_per_seq], ub[tiles_per_seq - 1::tiles_per_seq]
    return y, jnp.concatenate([ua, ub], axis=-1)


def _scan_lanes(x, op, fill):
    n = x.shape[1]
    lane = lax.broadcasted_iota(jnp.int32, x.shape, 1)
    s = 1
    while s < n:
        x = op(x, jnp.where(lane >= s, pltpu.roll(x, s, axis=1), fill))
        s *= 2
    return x


def _mlstm_gate_scan_kernel(ig_ref, lf_ref, b_ref, m_ref, mrow_ref):
    a = _scan_lanes(lf_ref[...], jnp.add, 0.0)
    b = ig_ref[...] - a
    m = jnp.maximum(_scan_lanes(b, jnp.maximum, NEG_INF), 0.0)
    b_ref[...] = b
    m_ref[...] = m
    mrow_ref[...] = a + m


def _mlstm_gate_scan(ig_t, lf_t, seq):
    rows, total = ig_t.shape
    spec = pl.BlockSpec((rows, seq), lambda n: (0, n))
    return pl.pallas_call(
        _mlstm_gate_scan_kernel,
        out_shape=[jax.ShapeDtypeStruct((rows, total), F32)] * 3,
        grid=(total // seq,),
        in_specs=[spec, spec],
        out_specs=[spec, spec, spec],
        compiler_params=_params("arbitrary"),
        name="mlstm_gate_scan",
    )(ig_t, lf_t)


def _cumsum_kernel(x_ref, o_ref):
    o_ref[...] = _scan_lanes(x_ref[...], jnp.add, 0.0)


def _cumsum_time(x_t, seq):
    rows, total = x_t.shape
    spec = pl.BlockSpec((rows, seq), lambda n: (0, n))
    return pl.pallas_call(
        _cumsum_kernel,
        out_shape=jax.ShapeDtypeStruct((rows, total), F32),
        grid=(total // seq,),
        in_specs=[spec],
        out_specs=spec,
        compiler_params=_params("arbitrary"),
        name="logf_cumsum",
    )(x_t)


def _mlstm_prompt_kernel(qk_ref, v_ref, ob_ref, rows_ref, cols_ref, cw_ref, cb_ref, nw_ref,
                         g_ref, c_ref, n_ref, mfin_ref, xbuf_ref, mprev_ref, *, tl):
    c = pl.program_id(1)
    nc = pl.num_programs(1)

    @pl.when(c == 0)
    def _():
        xbuf_ref[0:SUBLANES, :] = jnp.zeros((SUBLANES, 2 * W_B), F32)
        mprev_ref[...] = jnp.zeros_like(mprev_ref)
        c_ref[...] = jnp.zeros_like(c_ref)
        n_ref[...] = jnp.zeros_like(n_ref)

    xbuf_ref[SUBLANES:, :] = qk_ref[...]
    cw = cw_ref[...]
    y = cb_ref[...]
    for t in range(MLSTM_CONV):
        off = SUBLANES - (MLSTM_CONV - 1) + t
        y = y + cw[t:t + 1] * xbuf_ref[off:off + tl, :]
    xbuf_ref[0:SUBLANES, :] = xbuf_ref[tl:tl + SUBLANES, :]
    qk = y * _sigmoid(y)

    cols = cols_ref[...]
    mprev = mprev_ref[...]
    t_idx = lax.broadcasted_iota(jnp.int32, (tl, tl), 0)
    s_idx = lax.broadcasted_iota(jnp.int32, (tl, tl), 1)
    causal = s_idx <= t_idx
    for h in range(H_B):
        q = qk[:, h * DH_B:(h + 1) * DH_B]
        k = qk[:, W_B + h * DH_B:W_B + (h + 1) * DH_B] * (DH_B ** -0.5)
        v = v_ref[:, h * DH_B:(h + 1) * DH_B]
        qb, kb, vb = q.astype(BF16), k.astype(BF16), v.astype(BF16)
        b_row = rows_ref[h:h + 1, :]
        b_col = cols[:, h:h + 1]
        m_col = cols[:, H_B + h:H_B + h + 1]
        mrow_col = cols[:, 2 * H_B + h:2 * H_B + h + 1]
        m_last = cols[tl - 1:tl, H_B + h:H_B + h + 1]
        m_prev = mprev[:, H_B + h:H_B + h + 1]
        dmat = jnp.exp(jnp.where(causal, b_row - m_col, NEG_INF))
        s = _dot_nt(qb, kb) * dmat
        w_inter = jnp.exp(m_prev - m_col)
        cmat = c_ref[0, h]
        nvec = n_ref[0, h:h + 1, :]
        num = _dot(s.astype(BF16), vb) + w_inter * _dot(qb, cmat.astype(BF16))
        den = jnp.sum(s, axis=1, keepdims=True) + w_inter * jnp.sum(q * nvec, axis=1, keepdims=True)
        hh = num / jnp.maximum(jnp.abs(den), jnp.exp(-mrow_col))
        w_s = jnp.exp(b_col - m_last)
        w_c = jnp.exp(m_prev - m_last)
        kw = k * w_s
        c_ref[0, h] = w_c * cmat + lax.dot_general(kw.astype(BF16), vb, (((0,), (0,)), ((), ())),
                                                    preferred_element_type=F32)
        n_ref[0, h:h + 1, :] = w_c * nvec + jnp.sum(kw, axis=0, keepdims=True)
        hn = _rms(hh, nw_ref[:, h * DH_B:(h + 1) * DH_B])
        g_ref[:, h * DH_B:(h + 1) * DH_B] = _sigmoid(ob_ref[:, h * DH_B:(h + 1) * DH_B]) * hn
    mprev_ref[...] = cols[tl - 1:tl, :]

    @pl.when(c == nc - 1)
    def _():
        mfin_ref[0] = cols[tl - 1:tl, :]


def _mlstm_prompt(qk_pre, v_pre, ob, rows_pack, cols_pack, conv_w, conv_b, norm_w, nseq, seq):
    tl = MLSTM_TILE
    nc = seq // tl
    total = nseq * seq
    rmap = lambda n, c: (n * nc + c, 0)
    return pl.pallas_call(
        functools.partial(_mlstm_prompt_kernel, tl=tl),
        out_shape=[jax.ShapeDtypeStruct((total, W_B), F32),
                   jax.ShapeDtypeStruct((nseq, H_B, DH_B, DH_B), F32),
                   jax.ShapeDtypeStruct((nseq, H_B, DH_B), F32),
                   jax.ShapeDtypeStruct((nseq, 1, LANES), F32)],
        grid=(nseq, nc),
        in_specs=[pl.BlockSpec((tl, 2 * W_B), rmap),
                  pl.BlockSpec((tl, W_B), rmap),
                  pl.BlockSpec((tl, W_B), rmap),
                  pl.BlockSpec((SUBLANES, tl), lambda n, c: (0, n * nc + c)),
                  pl.BlockSpec((tl, LANES), rmap),
                  pl.BlockSpec((MLSTM_CONV, 2 * W_B), lambda n, c: (0, 0)),
                  pl.BlockSpec((1, 2 * W_B), lambda n, c: (0, 0)),
                  pl.BlockSpec((1, W_B), lambda n, c: (0, 0))],
        out_specs=[pl.BlockSpec((tl, W_B), rmap),
                   pl.BlockSpec((1, H_B, DH_B, DH_B), lambda n, c: (n, 0, 0, 0)),
                   pl.BlockSpec((1, H_B, DH_B), lambda n, c: (n, 0, 0)),
                   pl.BlockSpec((1, 1, LANES), lambda n, c: (n, 0, 0))],
        scratch_shapes=[pltpu.VMEM((tl + SUBLANES, 2 * W_B), F32), pltpu.VMEM((1, LANES), F32)],
        compiler_params=_params("arbitrary", "arbitrary"),
        name="mlstm_prompt",
    )(qk_pre, v_pre, ob, rows_pack, cols_pack, conv_w, conv_b.reshape(1, -1), norm_w.reshape(1, -1))


HEAD_DIM = 64


def _head_masks(shape):
    head = lax.broadcasted_iota(jnp.int32, shape, 1) // HEAD_DIM
    return [head == g for g in range(shape[1] // HEAD_DIM)]


def _stage_kv(k_ref, v_ref, kb_ref, vt_ref):
    kb_ref[...] = k_ref[...].astype(BF16)
    for j in range(k_ref.shape[0] // KEY_TILE):
        for c in range(k_ref.shape[1] // LANES):
            vt_ref[j, c * LANES:(c + 1) * LANES, :] = jnp.transpose(
                v_ref[j * KEY_TILE:(j + 1) * KEY_TILE, c * LANES:(c + 1) * LANES]).astype(BF16)


def _flash_group(qbs, qi, kb_ref, vt_ref, tq, adjust):
    tk = KEY_TILE
    per = tq // tk
    ng = len(qbs)
    krow = lax.broadcasted_iota(jnp.int32, (tk, ng * tq), 0)
    qcol = lax.broadcasted_iota(jnp.int32, (tk, ng * tq), 1) % tq
    q_cat = jnp.concatenate(qbs, axis=0)

    def step(j, state, diagonal):
        m, l = state[:2]
        start = pl.multiple_of(j * tk, tk)
        s = adjust(_dot_nt(kb_ref[pl.ds(start, tk), :], q_cat), j, start, diagonal is not None)
        if diagonal is not None:
            s = jnp.where(krow + diagonal * tk <= qcol, s, NEG_INF)
        m_new = jnp.maximum(m, jnp.max(s, axis=0, keepdims=True))
        alpha = jnp.exp(m - m_new)
        p = jnp.exp(s - m_new)
        l = alpha * l + jnp.sum(p, axis=0, keepdims=True)
        pb = p.astype(BF16)
        accs = tuple(alpha[:, g * tq:(g + 1) * tq] * state[2 + g]
                     + _dot(vt_ref[j, g * HEAD_DIM:(g + 1) * HEAD_DIM, :], pb[:, g * tq:(g + 1) * tq])
                     for g in range(ng))
        return (m_new, l) + accs

    state = ((jnp.full((1, ng * tq), NEG_INF, F32), jnp.zeros((1, ng * tq), F32))
             + tuple(jnp.zeros((HEAD_DIM, tq), F32) for _ in range(ng)))
    for d in range(per):
        state = step(qi * per + d, state, d)
    state = lax.fori_loop(0, qi * per, lambda j, st: step(j, st, None), state)
    l = state[1]
    return jnp.concatenate([state[2 + g] / l[:, g * tq:(g + 1) * tq] for g in range(ng)], axis=0)


def _store_heads(o_ref, o_t):
    for r in range(o_t.shape[0] // LANES):
        for c in range(o_t.shape[1] // LANES):
            o_ref[c * LANES:(c + 1) * LANES, r * LANES:(r + 1) * LANES] = jnp.transpose(
                o_t[r * LANES:(r + 1) * LANES, c * LANES:(c + 1) * LANES])


def _fox_prompt_kernel(q_ref, k_ref, v_ref, crow_ref, o_ref, kb_ref, vt_ref, cb_ref, *, tq):
    qi = pl.program_id(2)

    @pl.when(qi == 0)
    def _():
        _stage_kv(k_ref, v_ref, kb_ref, vt_ref)
        for g in range(cb_ref.shape[0]):
            for j in range(k_ref.shape[0] // LANES):
                sl = slice(j * LANES, (j + 1) * LANES)
                cb_ref[g, sl, :] = jnp.transpose(jnp.broadcast_to(crow_ref[0, g:g + 1, sl], (LANES, LANES)))

    q = q_ref[...] * (DH_C ** -0.5)

    def adjust(s, j, start, diagonal):
        return s - jnp.concatenate([cb_ref[g, pl.ds(start, KEY_TILE), :] for g in range(cb_ref.shape[0])
                                    for _ in range(tq // LANES)], axis=1)

    qbs = [jnp.where(mask, q, 0.0).astype(BF16) for mask in _head_masks(q.shape)]
    _store_heads(o_ref, _flash_group(qbs, qi, kb_ref, vt_ref, tq, adjust))


def _fox_prompt(q, k, v, crow, nseq, seq):
    tq = FOX_TILE
    nq = seq // tq
    ng = FOX_GROUP
    width = ng * HEAD_DIM
    return pl.pallas_call(
        functools.partial(_fox_prompt_kernel, tq=tq),
        out_shape=jax.ShapeDtypeStruct(q.shape, F32),
        grid=(nseq, H_C // ng, nq),
        in_specs=[pl.BlockSpec((tq, width), lambda n, p, i: (n * nq + i, p)),
                  pl.BlockSpec((seq, width), lambda n, p, i: (n, p)),
                  pl.BlockSpec((seq, width), lambda n, p, i: (n, p)),
                  pl.BlockSpec((1, ng, seq), lambda n, p, i: (p, 0, n))],
        out_specs=pl.BlockSpec((tq, width), lambda n, p, i: (n * nq + i, p)),
        scratch_shapes=[pltpu.VMEM((seq, width), BF16), pltpu.VMEM((seq // KEY_TILE, width, KEY_TILE), BF16),
                        pltpu.VMEM((ng, seq, LANES), F32)],
        compiler_params=_params("arbitrary", "arbitrary", "arbitrary"),
        name="fox_prompt",
    )(q, k, v, crow)


def _rank_rows(g, n_valid):
    nb = g.shape[0]
    r = lax.broadcasted_iota(jnp.int32, g.shape, 0)
    g = jnp.where(r < n_valid, g, NEG_INF)
    rank = jnp.zeros(g.shape, jnp.int32)
    for i in range(nb):
        gi = g[i:i + 1, :]
        rank = rank + ((gi > g) | ((gi == g) & (i < r))).astype(jnp.int32)
    return rank, r


def _moba_prompt_kernel(q_ref, k_ref, v_ref, o_ref, kb_ref, vt_ref, kmean_ref, sel_ref, *, tq, nb):
    qi = pl.program_id(2)

    @pl.when(qi == 0)
    def _():
        _stage_kv(k_ref, v_ref, kb_ref, vt_ref)
        kmean_ref[...] = jnp.zeros_like(kmean_ref)
        for j in range(nb):
            kmean_ref[j:j + 1, :] = jnp.mean(k_ref[j * MOBA_BLOCK:(j + 1) * MOBA_BLOCK, :], axis=0, keepdims=True)

    q = q_ref[...]
    km_hi, km_lo = _split_bf16(kmean_ref[...])
    qbs = []
    for hh, mask in enumerate(_head_masks(q.shape)):
        qm = jnp.where(mask, q, 0.0)
        q_hi, q_lo = _split_bf16(qm)
        gate = _dot_nt(km_hi, q_hi) + _dot_nt(km_lo, q_hi) + _dot_nt(km_hi, q_lo)
        rank, r = _rank_rows(gate, qi)
        sel_ref[hh] = ((rank < MOBA_TOPK) & (r < qi)).astype(F32)
        qbs.append((qm * (DH_A ** -0.5)).astype(BF16))

    def adjust(s, j, start, diagonal):
        if diagonal:
            return s
        blk = (j * KEY_TILE) // MOBA_BLOCK
        chosen = jnp.concatenate([sel_ref[g, pl.ds(blk, 1), :] for g in range(sel_ref.shape[0])], axis=1)
        return jnp.where(chosen > 0.5, s, NEG_INF)

    _store_heads(o_ref, _flash_group(qbs, qi, kb_ref, vt_ref, tq, adjust))


def _moba_prompt(q, k, v, nseq, seq):
    tq = ATT_TILE
    assert tq == MOBA_BLOCK and seq % MOBA_BLOCK == 0 and seq // MOBA_BLOCK >= MOBA_TOPK
    nq = seq // tq
    nb = seq // MOBA_BLOCK
    nbp = -(-nb // SUBLANES) * SUBLANES
    ng = MOBA_GROUP
    width = ng * HEAD_DIM
    return pl.pallas_call(
        functools.partial(_moba_prompt_kernel, tq=tq, nb=nb),
        out_shape=jax.ShapeDtypeStruct(q.shape, F32),
        grid=(nseq, H_A // ng, nq),
        in_specs=[pl.BlockSpec((tq, width), lambda n, p, i: (n * nq + i, p)),
                  pl.BlockSpec((seq, width), lambda n, p, i: (n, p)),
                  pl.BlockSpec((seq, width), lambda n, p, i: (n, p))],
        out_specs=pl.BlockSpec((tq, width), lambda n, p, i: (n * nq + i, p)),
        scratch_shapes=[pltpu.VMEM((seq, width), BF16), pltpu.VMEM((seq // KEY_TILE, width, KEY_TILE), BF16),
                        pltpu.VMEM((nbp, width), F32), pltpu.VMEM((ng, nbp, tq), F32)],
        compiler_params=_params("arbitrary", "arbitrary", "arbitrary"),
        name="moba_prompt",
    )(q, k, v)


def _column_replicated(row):
    return jnp.concatenate(
        [jnp.transpose(jnp.broadcast_to(row[:, c * LANES:(c + 1) * LANES], (LANES, LANES)))
         for c in range(row.shape[1] // LANES)], axis=0)


def _head_sums(x, dh):
    return jnp.concatenate([jnp.sum(x[h * dh:(h + 1) * dh, :], axis=0, keepdims=True)
                            for h in range(x.shape[0] // dh)], axis=0)


def _moba_sample_select_kernel(pt_ref, q_ref, *refs, group, nblk):
    k_refs = refs[:group]
    idx_ref = refs[group]
    qcol_ref, gate_ref = refs[group + 1:]
    g = pl.program_id(1)

    @pl.when(g == 0)
    def _():
        qcol_ref[...] = _column_replicated(q_ref[0])
        gate_ref[...] = jnp.zeros_like(gate_ref)

    lane = lax.broadcasted_iota(jnp.int32, gate_ref.shape, 1)
    pages_per_block = MOBA_BLOCK // PAGE_SIZE
    qcol = qcol_ref[...]
    upd = jnp.zeros(gate_ref.shape, F32)
    for i in range(group):
        qk = _head_sums(k_refs[i][0] * qcol, DH_A)
        blk = (g * group + i) // pages_per_block
        val = jnp.sum(qk, axis=1, keepdims=True) * (1.0 / (PAGE_SIZE * pages_per_block))
        upd = upd + jnp.where(lane == blk, val, 0.0)
    gate_ref[...] += upd

    @pl.when(g == pl.num_programs(1) - 1)
    def _():
        gate = jnp.where(lane < nblk, gate_ref[...], NEG_INF)
        rank = jnp.zeros(gate.shape, jnp.int32)
        for i in range(nblk):
            gi = gate[:, i:i + 1]
            rank = rank + ((gi > gate) | ((gi == gate) & (i < lane))).astype(jnp.int32)
        out = jnp.zeros(gate.shape, F32)
        for t in range(MOBA_TOPK):
            it = jnp.sum(jnp.where((rank == t) & (lane < nblk), lane.astype(F32), 0.0), axis=1, keepdims=True)
            out = jnp.where(lane == t, it, out)
        idx_ref[0] = out.astype(jnp.int32)


def _moba_sample_select(q, pool_kt, page_table, nblk):
    nb = q.shape[0]
    group = PAGES_PER_STEP
    npg = nblk * (MOBA_BLOCK // PAGE_SIZE)
    assert npg % group == 0 and MOBA_TOPK <= nblk <= LANES
    in_specs = [pl.BlockSpec((1, 1, W_A), lambda n, g, pt: (n, 0, 0))]
    for i in range(group):
        in_specs.append(pl.BlockSpec((1, W_A, PAGE_SIZE), lambda n, g, pt, i=i: (pt[n, g * group + i], 0, 0)))
    return pl.pallas_call(
        functools.partial(_moba_sample_select_kernel, group=group, nblk=nblk),
        out_shape=jax.ShapeDtypeStruct((nb, H_A, LANES), jnp.int32),
        grid_spec=pltpu.PrefetchScalarGridSpec(
            num_scalar_prefetch=1,
            grid=(nb, npg // group),
            in_specs=in_specs,
            out_specs=pl.BlockSpec((1, H_A, LANES), lambda n, g, pt: (n, 0, 0)),
            scratch_shapes=[pltpu.VMEM((W_A, LANES), F32), pltpu.VMEM((H_A, LANES), F32)]),
        compiler_params=_params("arbitrary", "arbitrary"),
        name="moba_sample_select",
    )(page_table, q.reshape(nb, 1, W_A), *([pool_kt] * group))


def _moba_sample_attend_kernel(pt_ref, idx_ref, q_ref, kn_ref, vn_ref, *refs, n_sel, n_own, qblk):
    n_pages = n_sel + n_own
    k_refs = refs[:n_pages]
    v_refs = refs[n_pages:2 * n_pages]
    o_ref = refs[2 * n_pages]
    n = pl.program_id(0)
    h = pl.program_id(1)
    q = q_ref[0, pl.ds(h, 1), :] * (DH_A ** -0.5)
    qcol = _column_replicated(q)[0:DH_A]
    pages_per_block = MOBA_BLOCK // PAGE_SIZE
    logits = []
    for i in range(n_pages):
        s = jnp.sum(k_refs[i][0] * qcol, axis=0, keepdims=True)
        if i < n_sel:
            valid = idx_ref[n, h, i // pages_per_block] < qblk
            s = jnp.where(valid, s, NEG_INF)
        logits.append(s)
    s_new = jnp.sum(q * kn_ref[0, pl.ds(h, 1), :], axis=1, keepdims=True)
    m = s_new
    for s in logits:
        m = jnp.maximum(m, jnp.max(s, axis=1, keepdims=True))
    p_new = jnp.exp(s_new - m)
    l = p_new
    acc_t = jnp.zeros((DH_A, PAGE_SIZE), F32)
    for i, s in enumerate(logits):
        p = jnp.exp(s - m)
        l = l + jnp.sum(p, axis=1, keepdims=True)
        acc_t = acc_t + v_refs[i][0] * p
    acc = jnp.sum(jnp.transpose(jnp.concatenate([acc_t, jnp.zeros((LANES - DH_A, PAGE_SIZE), F32)], axis=0)),
                  axis=0, keepdims=True)
    res = (acc + p_new * vn_ref[0, pl.ds(h, 1), :]) / l
    row = lax.broadcasted_iota(jnp.int32, (H_A, LANES), 0)

    @pl.when(h == 0)
    def _():
        o_ref[0] = jnp.where(row == 0, res, 0.0)

    @pl.when(h > 0)
    def _():
        o_ref[0] = jnp.where(row == h, res, o_ref[0])


def _moba_sample_attend(q, k_new, v_new, pool_kt, pool_vt, page_table, idx):
    nb = q.shape[0]
    n_pages_total = page_table.shape[1]
    pages_per_block = MOBA_BLOCK // PAGE_SIZE
    past = n_pages_total * PAGE_SIZE
    qblk = past // MOBA_BLOCK
    n_sel = MOBA_TOPK * pages_per_block
    n_own = 1 if (past - PAGE_SIZE) // MOBA_BLOCK == qblk else 0

    def sel_map(i):
        def index_map(n, h, pt, ix):
            lpage = jnp.minimum(ix[n, h, i // pages_per_block] * pages_per_block + i % pages_per_block,
                                n_pages_total - 1)
            return (pt[n, lpage], h, 0)
        return index_map

    def own_map(n, h, pt, ix):
        return (pt[n, n_pages_total - 1], h, 0)

    page_specs = [pl.BlockSpec((1, DH_A, PAGE_SIZE), sel_map(i)) for i in range(n_sel)]
    page_specs += [pl.BlockSpec((1, DH_A, PAGE_SIZE), own_map)] * n_own
    tok_spec = pl.BlockSpec((1, H_A, LANES), lambda n, h, pt, ix: (n, 0, 0))
    tok = lambda a: _pad_lanes(a.reshape(nb, H_A, DH_A))
    n_pg = n_sel + n_own
    att = pl.pallas_call(
        functools.partial(_moba_sample_attend_kernel, n_sel=n_sel, n_own=n_own, qblk=qblk),
        out_shape=jax.ShapeDtypeStruct((nb, H_A, LANES), F32),
        grid_spec=pltpu.PrefetchScalarGridSpec(
            num_scalar_prefetch=2,
            grid=(nb, H_A),
            in_specs=[tok_spec, tok_spec, tok_spec] + page_specs + page_specs,
            out_specs=tok_spec),
        compiler_params=_params("arbitrary", "arbitrary"),
        name="moba_sample_attend",
    )(page_table, idx, tok(q), tok(k_new), tok(v_new), *([pool_kt] * n_pg), *([pool_vt] * n_pg))
    return att[:, :, :DH_A].reshape(nb, W_A)


def _mlstm_sample_kernel(qk_ref, st_ref, v_ref, ob_ref, gate_ref, c_ref, n_ref, m_ref, cw_ref, cb_ref, nw_ref,
                         g_ref, cn_ref, nn_ref, mn_ref):
    cw = cw_ref[...]
    st = st_ref[0]
    y = cb_ref[...] + cw[MLSTM_CONV - 1:MLSTM_CONV] * qk_ref[0]
    for t in range(MLSTM_CONV - 1):
        y = y + cw[t:t + 1] * st[t:t + 1]
    qk = y * _sigmoid(y)
    gates = gate_ref[0]
    m_in = m_ref[0]
    m_out = jnp.zeros((1, LANES), F32)
    lane = lax.broadcasted_iota(jnp.int32, (1, LANES), 1)
    for h in range(H_B):
        q = qk[:, h * DH_B:(h + 1) * DH_B]
        k = qk[:, W_B + h * DH_B:W_B + (h + 1) * DH_B] * (DH_B ** -0.5)
        v = v_ref[0][:, h * DH_B:(h + 1) * DH_B]
        ig = gates[:, h:h + 1]
        lf = gates[:, H_B + h:H_B + h + 1]
        m_old = m_in[:, h:h + 1]
        m_row = jnp.maximum(ig, lf + m_old)
        w_inter = jnp.exp(lf + m_old - m_row)
        w_s = jnp.exp(ig - m_row)
        s = jnp.sum(q * k, axis=1, keepdims=True) * w_s
        cmat = c_ref[0, h]
        nvec = n_ref[0, h:h + 1, :]
        qc = _dot(jnp.broadcast_to(q, (SUBLANES, DH_B)).astype(BF16), cmat.astype(BF16))[0:1]
        num = s * v + w_inter * qc
        den = s + w_inter * jnp.sum(q * nvec, axis=1, keepdims=True)
        hh = num / jnp.maximum(jnp.abs(den), jnp.exp(-m_row))
        k_col = jnp.transpose(jnp.broadcast_to(k, (DH_B, DH_B)))
        cn_ref[0, h] = w_inter * cmat + w_s * (k_col * v)
        nn_ref[0, h:h + 1, :] = w_inter * nvec + w_s * k
        m_out = jnp.where(lane == h, m_row, m_out)
        hn = _rms(hh, nw_ref[:, h * DH_B:(h + 1) * DH_B])
        g_ref[0, :, h * DH_B:(h + 1) * DH_B] = _sigmoid(ob_ref[0][:, h * DH_B:(h + 1) * DH_B]) * hn
    mn_ref[0] = m_out


def _mlstm_sample(qk_pre, conv_state, v_pre, ob, gates, c, nvec, m, conv_w, conv_b, norm_w):
    nb = qk_pre.shape[0]
    tok = lambda a: a.reshape(nb, 1, a.shape[-1])
    tspec = lambda w: pl.BlockSpec((1, 1, w), lambda n: (n, 0, 0))
    full = lambda shape: pl.BlockSpec(shape, lambda n: (0,) * len(shape))
    g, cn, nn, mn = pl.pallas_call(
        _mlstm_sample_kernel,
        out_shape=[jax.ShapeDtypeStruct((nb, 1, W_B), F32),
                   jax.ShapeDtypeStruct((nb, H_B, DH_B, DH_B), F32),
                   jax.ShapeDtypeStruct((nb, H_B, DH_B), F32),
                   jax.ShapeDtypeStruct((nb, 1, LANES), F32)],
        grid=(nb,),
        in_specs=[tspec(2 * W_B),
                  pl.BlockSpec((1, MLSTM_CONV - 1, 2 * W_B), lambda n: (n, 0, 0)),
                  tspec(W_B), tspec(W_B), tspec(LANES),
                  pl.BlockSpec((1, H_B, DH_B, DH_B), lambda n: (n, 0, 0, 0)),
                  pl.BlockSpec((1, H_B, DH_B), lambda n: (n, 0, 0)),
                  tspec(LANES),
                  full((MLSTM_CONV, 2 * W_B)), full((1, 2 * W_B)), full((1, W_B))],
        out_specs=[tspec(W_B),
                   pl.BlockSpec((1, H_B, DH_B, DH_B), lambda n: (n, 0, 0, 0)),
                   pl.BlockSpec((1, H_B, DH_B), lambda n: (n, 0, 0)),
                   tspec(LANES)],
        compiler_params=_params("arbitrary"),
        name="mlstm_sample",
    )(tok(qk_pre), conv_state, tok(v_pre), tok(ob), tok(gates), c, nvec, tok(_pad_lanes(m)),
      conv_w, conv_b.reshape(1, -1), norm_w.reshape(1, -1))
    return g.reshape(nb, W_B), cn, nn, mn[:, 0, :H_B]


def _fox_sample_kernel(pt_ref, q_ref, kn_ref, vn_ref, lfn_ref, *refs, group):
    k_refs = refs[:group]
    v_refs = refs[group:2 * group]
    lf_refs = refs[2 * group:3 * group]
    o_ref = refs[3 * group]
    qcol_ref, m_ref, l_ref, run_ref, acc_ref = refs[3 * group + 1:]
    g = pl.program_id(1)

    @pl.when(g == 0)
    def _():
        qcol_ref[...] = _column_replicated(q_ref[0] * (DH_C ** -0.5))
        m_ref[...] = jnp.full(m_ref.shape, NEG_INF, F32)
        l_ref[...] = jnp.zeros_like(l_ref)
        run_ref[...] = jnp.zeros_like(run_ref)
        acc_ref[...] = jnp.zeros_like(acc_ref)

    qcol = qcol_ref[...]
    run = run_ref[...]
    logits = []
    for i in range(group):
        qk = _head_sums(k_refs[i][0] * qcol, DH_C)
        cum = run + _scan_lanes(lf_refs[i][0], jnp.add, 0.0)
        run = cum[:, PAGE_SIZE - 1:PAGE_SIZE]
        logits.append(qk - cum)
    s = jnp.concatenate(logits, axis=1)
    m = m_ref[...]
    m_new = jnp.maximum(m, jnp.max(s, axis=1, keepdims=True))
    alpha = jnp.exp(m - m_new)
    p = jnp.exp(s - m_new)
    l = alpha * l_ref[...] + jnp.sum(p, axis=1, keepdims=True)
    m_ref[...], l_ref[...], run_ref[...] = m_new, l, run
    for h in range(H_C):
        rows = slice(h * DH_C, (h + 1) * DH_C)
        a = acc_ref[rows, :] * alpha[h:h + 1, :]
        for i in range(group):
            a = a + v_refs[i][0, rows, :] * p[h:h + 1, i * PAGE_SIZE:(i + 1) * PAGE_SIZE]
        acc_ref[rows, :] = a

    @pl.when(g == pl.num_programs(1) - 1)
    def _():
        head = lax.broadcasted_iota(jnp.int32, (H_C, W_C), 0)
        mine = (lax.broadcasted_iota(jnp.int32, (H_C, W_C), 1) // DH_C) == head
        q_bd = jnp.where(mine, jnp.broadcast_to(q_ref[0], (H_C, W_C)), 0.0) * (DH_C ** -0.5)
        hrow = lax.broadcasted_iota(jnp.int32, (H_C, LANES), 0)
        hlane = lax.broadcasted_iota(jnp.int32, (H_C, LANES), 1)
        lf_new = jnp.sum(jnp.where(hrow == hlane, jnp.broadcast_to(lfn_ref[0], (H_C, LANES)), 0.0),
                         axis=1, keepdims=True)
        s_new = jnp.sum(q_bd * kn_ref[0], axis=1, keepdims=True) - (run + lf_new)
        m_f = jnp.maximum(m_new, s_new)
        alpha_f = jnp.exp(m_new - m_f)
        p_new = jnp.exp(s_new - m_f)
        l_f = alpha_f * l + p_new
        past = jnp.concatenate([jnp.sum(jnp.transpose(acc_ref[c * LANES:(c + 1) * LANES, :]), axis=0, keepdims=True)
                                for c in range(W_C // LANES)], axis=1)
        per_lane = lambda col: jnp.sum(jnp.where(mine, col, 0.0), axis=0, keepdims=True)
        o_ref[0] = (past * per_lane(alpha_f) + per_lane(p_new) * vn_ref[0]) / per_lane(l_f)


def _fox_sample(q, k_new, v_new, lf_new, pool_k, pool_v, pool_lf_t, page_table):
    nb = q.shape[0]
    n_pages = page_table.shape[1]
    group = PAGES_PER_STEP
    assert n_pages % group == 0
    tok_spec = pl.BlockSpec((1, 1, W_C), lambda n, g, pt: (n, 0, 0))
    tok = lambda a: a.reshape(nb, 1, a.shape[-1])
    page = lambda i, shape: pl.BlockSpec(shape, lambda n, g, pt: (pt[n, g * group + i], 0, 0))
    in_specs = [tok_spec, tok_spec, tok_spec, pl.BlockSpec((1, 1, LANES), lambda n, g, pt: (n, 0, 0))]
    in_specs += [page(i, (1, W_C, PAGE_SIZE)) for i in range(group)]
    in_specs += [page(i, (1, W_C, PAGE_SIZE)) for i in range(group)]
    in_specs += [page(i, (1, H_C, PAGE_SIZE)) for i in range(group)]
    return pl.pallas_call(
        functools.partial(_fox_sample_kernel, group=group),
        out_shape=jax.ShapeDtypeStruct((nb, 1, W_C), F32),
        grid_spec=pltpu.PrefetchScalarGridSpec(
            num_scalar_prefetch=1,
            grid=(nb, n_pages // group),
            in_specs=in_specs,
            out_specs=tok_spec,
            scratch_shapes=[pltpu.VMEM((W_C, LANES), F32), pltpu.VMEM((H_C, 1), F32), pltpu.VMEM((H_C, 1), F32),
                            pltpu.VMEM((H_C, 1), F32), pltpu.VMEM((W_C, PAGE_SIZE), F32)]),
        compiler_params=_params("arbitrary", "arbitrary"),
        name="fox_sample",
    )(page_table, tok(q), tok(k_new), tok(v_new), tok(lf_new),
      *([pool_k] * group), *([pool_v] * group), *([pool_lf_t] * group)).reshape(nb, W_C)


EVEN_SEGS = ((0, W_A), (W_A, W_A), (2 * W_A, W_A), (3 * W_A, 2 * W_B), (3 * W_A + 2 * W_B, W_B),
             (3 * W_A + 3 * W_B, W_B), (3 * W_A + 4 * W_B, LANES))
ODD_SEGS = ((0, W_C), (W_C, W_C), (2 * W_C, W_C), (3 * W_C, LANES))


def _even_weights(w_in, b_i, b_f):
    main = 3 * W_A + 4 * W_B
    w_pad = jnp.concatenate([w_in[:, :main], _pad_lanes(w_in[:, main:])], axis=1)
    w_bf = w_pad.astype(BF16)
    wq = w_in[:, :W_A]
    w_lo = (wq - wq.astype(BF16).astype(F32)).astype(BF16)
    gate_bias = _pad_lanes(jnp.concatenate([b_i, b_f]).reshape(1, -1))
    return w_bf, w_lo, gate_bias


def _odd_weights(w_in, b_f):
    w_pad = jnp.concatenate([w_in[:, :3 * W_C], _pad_lanes(w_in[:, 3 * W_C:])], axis=1)
    return w_pad.astype(BF16), _pad_lanes(b_f.reshape(1, -1))


def _feature_major(pool):
    n_pool, page, heads, dh = pool.shape
    return jnp.transpose(pool, (0, 2, 3, 1)).reshape(n_pool, heads * dh, page)


def _rows_to_sublanes(a):
    return jnp.pad(a.T, ((0, SUBLANES - a.shape[1]), (0, 0)))


def kernel(x_prompt, x_sample, cache_moba_k, cache_moba_v, state_mlstm_c, state_mlstm_n, state_mlstm_m,
           state_mlstm_conv, cache_fox_k, cache_fox_v, cache_fox_logf, state_ffn_conv, page_table, norm_w,
           even_w_in, mlstm_conv_w, mlstm_conv_b, mlstm_b_i, mlstm_b_f, mlstm_norm_w, even_w_out, fox_w_in,
           fox_b_f, fox_w_out, ffn_w_in, ffn_conv_w, ffn_conv_b, ffn_w_out):
    bp, sp, d = x_prompt.shape
    bs, ss, _ = x_sample.shape
    assert ss == 1 and d == D_MODEL
    depth = norm_w.shape[0]
    n_pages = page_table.shape[1]
    tp = bp * sp
    yp = x_prompt.reshape(tp, d)
    ys = x_sample.reshape(bs, d)
    outs = {name: [] for name in (
        "mk_p", "mv_p", "mk_s", "mv_s", "mc_p", "mn_p", "mm_p", "mconv_p", "mc_s", "mn_s", "mm_s", "mconv_s",
        "fk_p", "fv_p", "fl_p", "fk_s", "fv_s", "fl_s", "ff_p", "ff_s")}
    for layer in range(depth):
        nw = norm_w[layer]
        if layer % 2 == 0:
            e = layer // 2
            w_bf, w_lo, gate_bias = _even_weights(even_w_in[e], mlstm_b_i[e], mlstm_b_f[e])
            w_out_bf = even_w_out[e].astype(BF16)
            proj = lambda x, tm: _rms_proj(x, nw[0], w_bf, w_lo, gate_bias, EVEN_SEGS, 1, (H_B, 2 * H_B), tm)
            qa, ka, va, qkb, vb, ob, gates = proj(yp, ROW_TILE)
            att = _moba_prompt(qa, ka, va, bp, sp)
            b_t, m_t, mrow_t = _mlstm_gate_scan(_rows_to_sublanes(gates[:, :H_B]),
                                                _rows_to_sublanes(gates[:, H_B:2 * H_B]), sp)
            cols = _pad_lanes(jnp.concatenate([b_t[:H_B].T, m_t[:H_B].T, mrow_t[:H_B].T], axis=1))
            gated, cc, nn, mfin = _mlstm_prompt(qkb, vb, ob, b_t, cols, mlstm_conv_w[e], mlstm_conv_b[e],
                                                mlstm_norm_w[e], bp, sp)
            yp = _proj_res([att, gated], [w_out_bf[:W_A], w_out_bf[W_A:]], yp, nw[1], FFN_ROWS)
            outs["mk_p"].append(ka.reshape(bp, sp, H_A, DH_A))
            outs["mv_p"].append(va.reshape(bp, sp, H_A, DH_A))
            outs["mc_p"].append(cc)
            outs["mn_p"].append(nn)
            outs["mm_p"].append(mfin[:, 0, 2 * H_B:3 * H_B])
            outs["mconv_p"].append(qkb.reshape(bp, sp, 2 * W_B)[:, sp - (MLSTM_CONV - 1):])
            qa, ka, va, qkb, vb, ob, gates = proj(ys, bs)
            pool_kt = _feature_major(cache_moba_k[e])
            pool_vt = _feature_major(cache_moba_v[e])
            nblk = (n_pages * PAGE_SIZE) // MOBA_BLOCK
            idx = _moba_sample_select(qa, pool_kt, page_table, nblk)[:, :, :MOBA_TOPK]
            att = _moba_sample_attend(qa, ka, va, pool_kt, pool_vt, page_table, idx)
            gated, cc, nn, mm = _mlstm_sample(qkb, state_mlstm_conv[e], vb, ob, gates, state_mlstm_c[e],
                                              state_mlstm_n[e], state_mlstm_m[e], mlstm_conv_w[e],
                                              mlstm_conv_b[e], mlstm_norm_w[e])
            ys = _proj_res([att, gated], [w_out_bf[:W_A], w_out_bf[W_A:]], ys, nw[1], bs)
            outs["mk_s"].append(ka.reshape(bs, ss, H_A, DH_A))
            outs["mv_s"].append(va.reshape(bs, ss, H_A, DH_A))
            outs["mc_s"].append(cc)
            outs["mn_s"].append(nn)
            outs["mm_s"].append(mm)
            outs["mconv_s"].append(jnp.concatenate([state_mlstm_conv[e][:, 1:], qkb[:, None, :]], axis=1))
        else:
            o = layer // 2
            w_bf, gate_bias = _odd_weights(fox_w_in[o], fox_b_f[o])
            w_out_bf = fox_w_out[o].astype(BF16)
            proj = lambda x, tm: _rms_proj(x, nw[0], w_bf, None, gate_bias, ODD_SEGS, 0, (0, H_C), tm)
            q, k, v, lf = proj(yp, ROW_TILE)
            lf_t = lf[:, :H_C].T
            crow = _cumsum_time(lf_t, sp).reshape(H_C // FOX_GROUP, FOX_GROUP, tp)
            att = _fox_prompt(q, k, v, crow, bp, sp)
            yp = _proj_res([att], [w_out_bf], yp, nw[1], FFN_ROWS)
            outs["fk_p"].append(k.reshape(bp, sp, H_C, DH_C))
            outs["fv_p"].append(v.reshape(bp, sp, H_C, DH_C))
            outs["fl_p"].append(lf[:, :H_C].reshape(bp, sp, H_C))
            q, k, v, lf = proj(ys, bs)
            pool_lf_t = jnp.transpose(cache_fox_logf[o], (0, 2, 1))
            att = _fox_sample(q, k, v, lf, _feature_major(cache_fox_k[o]), _feature_major(cache_fox_v[o]),
                              pool_lf_t, page_table)
            ys = _proj_res([att], [w_out_bf], ys, nw[1], bs)
            outs["fk_s"].append(k.reshape(bs, ss, H_C, DH_C))
            outs["fv_s"].append(v.reshape(bs, ss, H_C, DH_C))
            outs["fl_s"].append(lf[:, :H_C].reshape(bs, ss, H_C))
        w_in_bf = ffn_w_in[layer].astype(BF16)
        w_o_bf = ffn_w_out[layer].astype(BF16)
        yp, buf_p = _ffn(yp, nw[2], w_in_bf, ffn_conv_w[layer], ffn_conv_b[layer], w_o_bf, nw[3], FFN_ROWS, sp)
        ys, u_s = _ffn(ys, nw[2], w_in_bf, ffn_conv_w[layer], ffn_conv_b[layer], w_o_bf, nw[3], bs, 1,
                       state=state_ffn_conv[layer])
        outs["ff_p"].append(buf_p)
        outs["ff_s"].append(jnp.concatenate([state_ffn_conv[layer][:, 1:], u_s[:, None, :]], axis=1))
    st = {name: jnp.stack(vals) for name, vals in outs.items()}
    return (yp.reshape(bp, sp, d), ys.reshape(bs, ss, d), st["mk_p"], st["mv_p"], st["mk_s"], st["mv_s"],
            st["mc_p"], st["mn_p"], st["mm_p"], st["mconv_p"], st["mc_s"], st["mn_s"], st["mm_s"], st["mconv_s"],
            st["fk_p"], st["fv_p"], st["fl_p"], st["fk_s"], st["fv_s"], st["fl_s"], st["ff_p"], st["ff_s"])
```

```python
import functools

import jax
import jax.numpy as jnp
from jax import lax
from jax.experimental import pallas as pl
from jax.experimental.pallas import tpu as pltpu

F32 = jnp.float32
BF16 = jnp.bfloat16
NEG_INF = float("-inf")

D_MODEL = 1024
PAGE_SIZE = 128
H_A, DH_A = 8, 64
W_A = H_A * DH_A
MOBA_BLOCK = 256
MOBA_TOPK = 3
H_B, DH_B = 4, 128
W_B = H_B * DH_B
MLSTM_CONV = 4
H_C, DH_C = 16, 64
W_C = H_C * DH_C
D_FF = 2816
FFN_CONV = 3
RMS_EPS = 1e-6

LANES = 128
SUBLANES = 8
VMEM_LIMIT = 56 * 1024 * 1024

ROW_TILE = 256
ATT_TILE = 256
FOX_TILE = 512
KEY_TILE = 256
MOBA_GROUP = 8
FOX_GROUP = 4
MLSTM_TILE = 256
FFN_ROWS = 512
FFN_COLS = 1408
FFN_SLICE = 256
PAGES_PER_STEP = 8


def _params(*sem):
    return pltpu.CompilerParams(dimension_semantics=sem, vmem_limit_bytes=VMEM_LIMIT)


def _rms(x, w):
    return x * lax.rsqrt(jnp.mean(x * x, axis=-1, keepdims=True) + RMS_EPS) * w


def _sigmoid(x):
    return 1.0 / (1.0 + jnp.exp(-x))


def _log_sigmoid(x):
    return -(jnp.maximum(-x, 0.0) + jnp.log1p(jnp.exp(-jnp.abs(x))))


def _gelu_tanh(x):
    c = 0.7978845608028654
    return 0.5 * x * (1.0 + jnp.tanh(c * (x + 0.044715 * (x * x * x))))


def _dot(a, b):
    return jnp.dot(a, b, preferred_element_type=F32)


def _dot_nt(a, b):
    return lax.dot_general(a, b, (((1,), (1,)), ((), ())), preferred_element_type=F32)


def _split_bf16(x):
    hi = x.astype(BF16)
    lo = (x - hi.astype(F32)).astype(BF16)
    return hi, lo


def _pad_lanes(a, width=LANES):
    return jnp.pad(a, ((0, 0),) * (a.ndim - 1) + ((0, width - a.shape[-1]),))


def _rms_proj_kernel(*refs, segs, n_hi, gate_lf, cols):
    x_ref, nw_ref, w_ref = refs[:3]
    pos = 3
    wlo_ref = None
    if n_hi:
        wlo_ref = refs[pos]
        pos += 1
    gb_ref = refs[pos]
    out_refs = list(refs[pos + 1:])
    h = _rms(x_ref[...], nw_ref[...])
    hb, hl = _split_bf16(h)
    lo_start = 0
    for i, (start, width) in enumerate(segs):
        w = w_ref[:, start:start + width]
        z = _dot(hb, w)
        if i < n_hi:
            z = z + _dot(hl, w) + _dot(hb, wlo_ref[:, lo_start:lo_start + width])
            lo_start += width
        if i == len(segs) - 1:
            z = z + gb_ref[...]
            lane = lax.broadcasted_iota(jnp.int32, z.shape, 1)
            z = jnp.where((lane >= gate_lf[0]) & (lane < gate_lf[1]), _log_sigmoid(z), z)
        if cols.get(i, True):
            out_refs.pop(0)[...] = z
        if i in cols:
            zt_ref = out_refs.pop(0)
            for r in range(z.shape[0] // LANES):
                for c in range(width // LANES):
                    zt_ref[0, c * LANES:(c + 1) * LANES, r * LANES:(r + 1) * LANES] = jnp.transpose(
                        z[r * LANES:(r + 1) * LANES, c * LANES:(c + 1) * LANES])


def _rms_proj(x, nw, w_bf, w_lo, gate_bias, segs, n_hi, gate_lf, tm, cols=None, seq_rows=None):
    rows, d = x.shape
    ncols = w_bf.shape[1]
    cols = cols or {}
    in_specs = [pl.BlockSpec((tm, d), lambda i: (i, 0)),
                pl.BlockSpec((1, d), lambda i: (0, 0)),
                pl.BlockSpec((d, ncols), lambda i: (0, 0))]
    args = [x, nw.reshape(1, d), w_bf]
    if n_hi:
        in_specs.append(pl.BlockSpec(w_lo.shape, lambda i: (0, 0)))
        args.append(w_lo)
    in_specs.append(pl.BlockSpec((1, LANES), lambda i: (0, 0)))
    args.append(gate_bias)
    out_shape, out_specs = [], []
    for i, (_, wd) in enumerate(segs):
        if cols.get(i, True):
            out_shape.append(jax.ShapeDtypeStruct((rows, wd), F32))
            out_specs.append(pl.BlockSpec((tm, wd), lambda i: (i, 0)))
        if i in cols:
            tps = seq_rows // tm
            out_shape.append(jax.ShapeDtypeStruct((rows // seq_rows, wd, seq_rows), F32))
            out_specs.append(pl.BlockSpec((1, wd, tm), lambda i: (i // tps, 0, i % tps)))
    return pl.pallas_call(
        functools.partial(_rms_proj_kernel, segs=segs, n_hi=n_hi, gate_lf=gate_lf, cols=cols),
        out_shape=out_shape,
        grid=(rows // tm,),
        in_specs=in_specs,
        out_specs=out_specs,
        compiler_params=_params("arbitrary"),
        name="rms_proj",
    )(*args)


def _proj_res_kernel(*refs, n_in):
    a_refs = refs[:n_in]
    w_refs = refs[n_in:2 * n_in]
    x_ref, nw_ref, o_ref = refs[2 * n_in:]
    acc = None
    for a_ref, w_ref in zip(a_refs, w_refs):
        z = _dot(a_ref[...].astype(BF16), w_ref[...])
        acc = z if acc is None else acc + z
    o_ref[...] = x_ref[...] + _rms(acc, nw_ref[...])


def _proj_res(acts, ws, x, nw, tm):
    rows, d = x.shape
    n_in = len(acts)
    in_specs = ([pl.BlockSpec((tm, a.shape[1]), lambda i: (i, 0)) for a in acts]
                + [pl.BlockSpec(w.shape, lambda i: (0, 0)) for w in ws]
                + [pl.BlockSpec((tm, d), lambda i: (i, 0)), pl.BlockSpec((1, d), lambda i: (0, 0))])
    return pl.pallas_call(
        functools.partial(_proj_res_kernel, n_in=n_in),
        out_shape=jax.ShapeDtypeStruct((rows, d), F32),
        grid=(rows // tm,),
        in_specs=in_specs,
        out_specs=pl.BlockSpec((tm, d), lambda i: (i, 0)),
        compiler_params=_params("arbitrary"),
        name="proj_res",
    )(*acts, *ws, x, nw.reshape(1, d))


def _ffn_kernel(*refs, tm, tf, nj, tiles_per_seq, stateful):
    (x_ref, nw_in_ref, wa_ref, wb_ref, cwa_ref, cwb_ref, cba_ref, cbb_ref, wo_ref, nw_out_ref) = refs[:10]
    pos = 10
    if stateful:
        s0a_ref, s0b_ref, s1a_ref, s1b_ref = refs[pos:pos + 4]
        pos += 4
    y_ref, oa_ref, ob_ref = refs[pos:pos + 3]
    pos += 3
    hb_ref, acc_ref = refs[pos:pos + 2]
    pos += 2
    if not stateful:
        ubuf_ref, carry_ref = refs[pos:pos + 2]
    i = pl.program_id(0)
    j = pl.program_id(1)

    @pl.when(j == 0)
    def _():
        hb_ref[...] = _rms(x_ref[...], nw_in_ref[...]).astype(BF16)
        acc_ref[...] = jnp.zeros_like(acc_ref)

    hb = hb_ref[...]
    if not stateful:
        first = i % tiles_per_seq == 0

        @pl.when(first)
        def _():
            ubuf_ref[:, 0:SUBLANES, :] = jnp.zeros((2, SUBLANES, tf), F32)

        @pl.when(jnp.logical_not(first))
        def _():
            ubuf_ref[:, 0:SUBLANES, :] = carry_ref[:, j]

    gs = []
    for c0 in range(0, tf, FFN_SLICE):
        cs = slice(c0, min(c0 + FFN_SLICE, tf))
        halves = []
        for half, (w_ref, cw_ref, cb_ref) in enumerate(((wa_ref, cwa_ref, cba_ref), (wb_ref, cwb_ref, cbb_ref))):
            u = _dot(hb, w_ref[:, cs])
            cw = cw_ref[:, cs]
            if stateful:
                s0 = (s0a_ref, s0b_ref)[half][:, cs]
                s1 = (s1a_ref, s1b_ref)[half][:, cs]
                y = cw[0:1] * s0 + cw[1:2] * s1 + cw[2:3] * u + cb_ref[:, cs]
                (oa_ref, ob_ref)[half][:, cs] = u
            else:
                ubuf_ref[half, SUBLANES:, cs] = u
                y = (cw[0:1] * ubuf_ref[half, SUBLANES - 2:SUBLANES - 2 + tm, cs]
                     + cw[1:2] * ubuf_ref[half, SUBLANES - 1:SUBLANES - 1 + tm, cs]
                     + cw[2:3] * u + cb_ref[:, cs])
            halves.append(y)
        gs.append((_gelu_tanh(halves[0]) * halves[1]).astype(BF16))
    if not stateful:
        carry_ref[:, j] = ubuf_ref[:, tm:tm + SUBLANES, :]
        oa_ref[0] = ubuf_ref[0, tm + SUBLANES - 2:tm + SUBLANES, :]
        ob_ref[0] = ubuf_ref[1, tm + SUBLANES - 2:tm + SUBLANES, :]
    acc_ref[...] += _dot(jnp.concatenate(gs, axis=1), wo_ref[...])

    @pl.when(j == nj - 1)
    def _():
        y_ref[...] = x_ref[...] + _rms(acc_ref[...], nw_out_ref[...])


def _ffn(x, nw_in, w_in_bf, conv_w, conv_b, w_out_bf, nw_out, tm, seq_rows, state=None):
    rows, d = x.shape
    tf = FFN_COLS
    nj = D_FF // tf
    stateful = state is not None
    cb2 = conv_b.reshape(1, 2 * D_FF)
    in_specs = [pl.BlockSpec((tm, d), lambda i, j: (i, 0)),
                pl.BlockSpec((1, d), lambda i, j: (0, 0)),
                pl.BlockSpec((d, tf), lambda i, j: (0, j)),
                pl.BlockSpec((d, tf), lambda i, j: (0, nj + j)),
                pl.BlockSpec((FFN_CONV, tf), lambda i, j: (0, j)),
                pl.BlockSpec((FFN_CONV, tf), lambda i, j: (0, nj + j)),
                pl.BlockSpec((1, tf), lambda i, j: (0, j)),
                pl.BlockSpec((1, tf), lambda i, j: (0, nj + j)),
                pl.BlockSpec((tf, d), lambda i, j: (j, 0)),
                pl.BlockSpec((1, d), lambda i, j: (0, 0))]
    args = [x, nw_in.reshape(1, d), w_in_bf, w_in_bf, conv_w, conv_w, cb2, cb2, w_out_bf, nw_out.reshape(1, d)]
    scratch = [pltpu.VMEM((tm, d), BF16), pltpu.VMEM((tm, d), F32)]
    if stateful:
        s0, s1 = state[:, 0, :], state[:, 1, :]
        in_specs += [pl.BlockSpec((tm, tf), lambda i, j: (i, j)), pl.BlockSpec((tm, tf), lambda i, j: (i, nj + j)),
                     pl.BlockSpec((tm, tf), lambda i, j: (i, j)), pl.BlockSpec((tm, tf), lambda i, j: (i, nj + j))]
        args += [s0, s0, s1, s1]
        out_shape = [jax.ShapeDtypeStruct((rows, d), F32), jax.ShapeDtypeStruct((rows, D_FF), F32),
                     jax.ShapeDtypeStruct((rows, D_FF), F32)]
        out_specs = [pl.BlockSpec((tm, d), lambda i, j: (i, 0)), pl.BlockSpec((tm, tf), lambda i, j: (i, j)),
                     pl.BlockSpec((tm, tf), lambda i, j: (i, j))]
        tiles_per_seq = 1
    else:
        tiles_per_seq = seq_rows // tm
        ntiles = rows // tm
        out_shape = [jax.ShapeDtypeStruct((rows, d), F32), jax.ShapeDtypeStruct((ntiles, FFN_CONV - 1, D_FF), F32),
                     jax.ShapeDtypeStruct((ntiles, FFN_CONV - 1, D_FF), F32)]
        out_specs = [pl.BlockSpec((tm, d), lambda i, j: (i, 0)),
                     pl.BlockSpec((1, FFN_CONV - 1, tf), lambda i, j: (i, 0, j)),
                     pl.BlockSpec((1, FFN_CONV - 1, tf), lambda i, j: (i, 0, j))]
        scratch += [pltpu.VMEM((2, tm + SUBLANES, tf), F32), pltpu.VMEM((2, nj, SUBLANES, tf), F32)]
    y, ua, ub = pl.pallas_call(
        functools.partial(_ffn_kernel, tm=tm, tf=tf, nj=nj, tiles_per_seq=tiles_per_seq, stateful=stateful),
        out_shape=out_shape,
        grid=(rows // tm, nj),
        in_specs=in_specs,
        out_specs=out_specs,
        scratch_shapes=scratch,
        compiler_params=_params("arbitrary", "arbitrary"),
        name="conv_ffn",
    )(*args)
    if not stateful:
        ua, ub = ua[tiles_per_seq - 1::tiles_per_seq], ub[tiles_per_seq - 1::tiles_per_seq]
    return y, jnp.concatenate([ua, ub], axis=-1)


def _scan_lanes(x, op, fill):
    n = x.shape[1]
    lane = lax.broadcasted_iota(jnp.int32, x.shape, 1)
    s = 1
    while s < n:
        x = op(x, jnp.where(lane >= s, pltpu.roll(x, s, axis=1), fill))
        s *= 2
    return x


def _mlstm_gate_scan_kernel(ig_ref, lf_ref, b_ref, m_ref, mrow_ref):
    a = _scan_lanes(lf_ref[...], jnp.add, 0.0)
    b = ig_ref[...] - a
    m = jnp.maximum(_scan_lanes(b, jnp.maximum, NEG_INF), 0.0)
    b_ref[...] = b
    m_ref[...] = m
    mrow_ref[...] = a + m


def _mlstm_gate_scan(ig_t, lf_t, seq):
    rows, total = ig_t.shape
    spec = pl.BlockSpec((rows, seq), lambda n: (0, n))
    return pl.pallas_call(
        _mlstm_gate_scan_kernel,
        out_shape=[jax.ShapeDtypeStruct((rows, total), F32)] * 3,
        grid=(total // seq,),
        in_specs=[spec, spec],
        out_specs=[spec, spec, spec],
        compiler_params=_params("arbitrary"),
        name="mlstm_gate_scan",
    )(ig_t, lf_t)


def _cumsum_kernel(x_ref, o_ref):
    o_ref[...] = _scan_lanes(x_ref[...], jnp.add, 0.0)


def _cumsum_time(x_t, seq):
    rows, total = x_t.shape
    spec = pl.BlockSpec((rows, seq), lambda n: (0, n))
    return pl.pallas_call(
        _cumsum_kernel,
        out_shape=jax.ShapeDtypeStruct((rows, total), F32),
        grid=(total // seq,),
        in_specs=[spec],
        out_specs=spec,
        compiler_params=_params("arbitrary"),
        name="logf_cumsum",
    )(x_t)


def _mlstm_prompt_kernel(qk_ref, v_ref, ob_ref, rows_ref, cols_ref, cw_ref, cb_ref, nw_ref,
                         g_ref, c_ref, n_ref, mfin_ref, xbuf_ref, mprev_ref, *, tl):
    c = pl.program_id(1)
    nc = pl.num_programs(1)

    @pl.when(c == 0)
    def _():
        xbuf_ref[0:SUBLANES, :] = jnp.zeros((SUBLANES, 2 * W_B), F32)
        mprev_ref[...] = jnp.zeros_like(mprev_ref)
        c_ref[...] = jnp.zeros_like(c_ref)
        n_ref[...] = jnp.zeros_like(n_ref)

    xbuf_ref[SUBLANES:, :] = qk_ref[...]
    cw = cw_ref[...]
    y = cb_ref[...]
    for t in range(MLSTM_CONV):
        off = SUBLANES - (MLSTM_CONV - 1) + t
        y = y + cw[t:t + 1] * xbuf_ref[off:off + tl, :]
    xbuf_ref[0:SUBLANES, :] = xbuf_ref[tl:tl + SUBLANES, :]
    qk = y * _sigmoid(y)

    cols = cols_ref[...]
    mprev = mprev_ref[...]
    t_idx = lax.broadcasted_iota(jnp.int32, (tl, tl), 0)
    s_idx = lax.broadcasted_iota(jnp.int32, (tl, tl), 1)
    causal = s_idx <= t_idx
    for h in range(H_B):
        q = qk[:, h * DH_B:(h + 1) * DH_B]
        k = qk[:, W_B + h * DH_B:W_B + (h + 1) * DH_B] * (DH_B ** -0.5)
        v = v_ref[:, h * DH_B:(h + 1) * DH_B]
        qb, kb, vb = q.astype(BF16), k.astype(BF16), v.astype(BF16)
        b_row = rows_ref[h:h + 1, :]
        b_col = cols[:, h:h + 1]
        m_col = cols[:, H_B + h:H_B + h + 1]
        mrow_col = cols[:, 2 * H_B + h:2 * H_B + h + 1]
        m_last = cols[tl - 1:tl, H_B + h:H_B + h + 1]
        m_prev = mprev[:, H_B + h:H_B + h + 1]
        dmat = jnp.exp(jnp.where(causal, b_row - m_col, NEG_INF))
        s = _dot_nt(qb, kb) * dmat
        w_inter = jnp.exp(m_prev - m_col)
        cmat = c_ref[0, h]
        nvec = n_ref[0, h:h + 1, :]
        num = _dot(s.astype(BF16), vb) + w_inter * _dot(qb, cmat.astype(BF16))
        den = jnp.sum(s, axis=1, keepdims=True) + w_inter * jnp.sum(q * nvec, axis=1, keepdims=True)
        hh = num / jnp.maximum(jnp.abs(den), jnp.exp(-mrow_col))
        w_s = jnp.exp(b_col - m_last)
        w_c = jnp.exp(m_prev - m_last)
        kw = k * w_s
        c_ref[0, h] = w_c * cmat + lax.dot_general(kw.astype(BF16), vb, (((0,), (0,)), ((), ())),
                                                    preferred_element_type=F32)
        n_ref[0, h:h + 1, :] = w_c * nvec + jnp.sum(kw, axis=0, keepdims=True)
        hn = _rms(hh, nw_ref[:, h * DH_B:(h + 1) * DH_B])
        g_ref[:, h * DH_B:(h + 1) * DH_B] = _sigmoid(ob_ref[:, h * DH_B:(h + 1) * DH_B]) * hn
    mprev_ref[...] = cols[tl - 1:tl, :]

    @pl.when(c == nc - 1)
    def _():
        mfin_ref[0] = cols[tl - 1:tl, :]


def _mlstm_prompt(qk_pre, v_pre, ob, rows_pack, cols_pack, conv_w, conv_b, norm_w, nseq, seq):
    tl = MLSTM_TILE
    nc = seq // tl
    total = nseq * seq
    rmap = lambda n, c: (n * nc + c, 0)
    return pl.pallas_call(
        functools.partial(_mlstm_prompt_kernel, tl=tl),
        out_shape=[jax.ShapeDtypeStruct((total, W_B), F32),
                   jax.ShapeDtypeStruct((nseq, H_B, DH_B, DH_B), F32),
                   jax.ShapeDtypeStruct((nseq, H_B, DH_B), F32),
                   jax.ShapeDtypeStruct((nseq, 1, LANES), F32)],
        grid=(nseq, nc),
        in_specs=[pl.BlockSpec((tl, 2 * W_B), rmap),
                  pl.BlockSpec((tl, W_B), rmap),
                  pl.BlockSpec((tl, W_B), rmap),
                  pl.BlockSpec((SUBLANES, tl), lambda n, c: (0, n * nc + c)),
                  pl.BlockSpec((tl, LANES), rmap),
                  pl.BlockSpec((MLSTM_CONV, 2 * W_B), lambda n, c: (0, 0)),
                  pl.BlockSpec((1, 2 * W_B), lambda n, c: (0, 0)),
                  pl.BlockSpec((1, W_B), lambda n, c: (0, 0))],
        out_specs=[pl.BlockSpec((tl, W_B), rmap),
                   pl.BlockSpec((1, H_B, DH_B, DH_B), lambda n, c: (n, 0, 0, 0)),
                   pl.BlockSpec((1, H_B, DH_B), lambda n, c: (n, 0, 0)),
                   pl.BlockSpec((1, 1, LANES), lambda n, c: (n, 0, 0))],
        scratch_shapes=[pltpu.VMEM((tl + SUBLANES, 2 * W_B), F32), pltpu.VMEM((1, LANES), F32)],
        compiler_params=_params("arbitrary", "arbitrary"),
        name="mlstm_prompt",
    )(qk_pre, v_pre, ob, rows_pack, cols_pack, conv_w, conv_b.reshape(1, -1), norm_w.reshape(1, -1))


HEAD_DIM = 64
LOG2_E = 1.4426950408889634


def _head_masks(shape):
    head = lax.broadcasted_iota(jnp.int32, shape, 1) // HEAD_DIM
    return [head == g for g in range(shape[1] // HEAD_DIM)]


def _stage_kv(k_ref, vt_in_ref, kb_ref, vt_ref):
    kb_ref[...] = k_ref[...].astype(BF16)
    for j in range(k_ref.shape[0] // KEY_TILE):
        vt_ref[j] = vt_in_ref[0, :, j * KEY_TILE:(j + 1) * KEY_TILE].astype(BF16)


def _flash_group(qbs, qi, kb_ref, vt_ref, tq, adjust):
    tk = KEY_TILE
    per = tq // tk
    ng = len(qbs)
    krow = lax.broadcasted_iota(jnp.int32, (tk, ng * tq), 0)
    qcol = lax.broadcasted_iota(jnp.int32, (tk, ng * tq), 1) % tq
    q_cat = jnp.concatenate(qbs, axis=0)

    def logits(j, diagonal):
        start = pl.multiple_of(j * tk, tk)
        s = adjust(_dot_nt(kb_ref[pl.ds(start, tk), :], q_cat), j, start, diagonal is not None)
        if diagonal is not None:
            s = jnp.where(krow + diagonal * tk <= qcol, s, NEG_INF)
        return s

    def softmax(m, l, s):
        m_new = jnp.maximum(m, jnp.max(s, axis=0, keepdims=True))
        alpha = jnp.exp2(m - m_new)
        p = jnp.exp2(s - m_new)
        return m_new, alpha * l + jnp.sum(p, axis=0, keepdims=True), alpha, p.astype(BF16)

    def accumulate(accs, alpha, pb, j):
        return tuple(alpha[:, g * tq:(g + 1) * tq] * accs[g]
                     + _dot(vt_ref[j, g * HEAD_DIM:(g + 1) * HEAD_DIM, :], pb[:, g * tq:(g + 1) * tq])
                     for g in range(ng))

    m = jnp.full((1, ng * tq), NEG_INF, F32)
    l = jnp.zeros((1, ng * tq), F32)
    accs = tuple(jnp.zeros((HEAD_DIM, tq), F32) for _ in range(ng))
    for d in range(per):
        m, l, alpha, pb = softmax(m, l, logits(qi * per + d, d))
        accs = accumulate(accs, alpha, pb, qi * per + d)
    def body(j, carry):
        m, l, accs = carry
        m, l, alpha, pb = softmax(m, l, logits(j, None))
        return m, l, accumulate(accs, alpha, pb, j)

    m, l, accs = lax.fori_loop(0, qi * per, body, (m, l, accs))
    return jnp.concatenate([accs[g] / l[:, g * tq:(g + 1) * tq] for g in range(ng)], axis=0)


def _store_heads(o_ref, o_t):
    for r in range(o_t.shape[0] // LANES):
        for c in range(o_t.shape[1] // LANES):
            o_ref[c * LANES:(c + 1) * LANES, r * LANES:(r + 1) * LANES] = jnp.transpose(
                o_t[r * LANES:(r + 1) * LANES, c * LANES:(c + 1) * LANES])


def _fox_prompt_kernel(q_ref, k_ref, v_ref, crow_ref, o_ref, kb_ref, vt_ref, cb_ref, *, tq):
    qi = pl.program_id(2)

    @pl.when(qi == 0)
    def _():
        _stage_kv(k_ref, v_ref, kb_ref, vt_ref)
        for g in range(cb_ref.shape[0]):
            for j in range(k_ref.shape[0] // LANES):
                sl = slice(j * LANES, (j + 1) * LANES)
                cb_ref[g, sl, :] = jnp.transpose(
                    jnp.broadcast_to(crow_ref[0, g:g + 1, sl] * LOG2_E, (LANES, LANES)))

    q = q_ref[...] * (DH_C ** -0.5 * LOG2_E)

    def adjust(s, j, start, diagonal):
        return s - jnp.concatenate([cb_ref[g, pl.ds(start, KEY_TILE), :] for g in range(cb_ref.shape[0])
                                    for _ in range(tq // LANES)], axis=1)

    qbs = [jnp.where(mask, q, 0.0).astype(BF16) for mask in _head_masks(q.shape)]
    _store_heads(o_ref, _flash_group(qbs, qi, kb_ref, vt_ref, tq, adjust))


def _fox_prompt(q, k, v, crow, nseq, seq):
    tq = FOX_TILE
    nq = seq // tq
    ng = FOX_GROUP
    width = ng * HEAD_DIM
    return pl.pallas_call(
        functools.partial(_fox_prompt_kernel, tq=tq),
        out_shape=jax.ShapeDtypeStruct(q.shape, F32),
        grid=(nseq, H_C // ng, nq),
        in_specs=[pl.BlockSpec((tq, width), lambda n, p, i: (n * nq + i, p)),
                  pl.BlockSpec((seq, width), lambda n, p, i: (n, p)),
                  pl.BlockSpec((1, width, seq), lambda n, p, i: (n, p, 0)),
                  pl.BlockSpec((1, ng, seq), lambda n, p, i: (p, 0, n))],
        out_specs=pl.BlockSpec((tq, width), lambda n, p, i: (n * nq + i, p)),
        scratch_shapes=[pltpu.VMEM((seq, width), BF16), pltpu.VMEM((seq // KEY_TILE, width, KEY_TILE), BF16),
                        pltpu.VMEM((ng, seq, LANES), F32)],
        compiler_params=_params("arbitrary", "arbitrary", "arbitrary"),
        name="fox_prompt",
    )(q, k, v, crow)


def _rank_rows(g, n_valid):
    nb = g.shape[0]
    r = lax.broadcasted_iota(jnp.int32, g.shape, 0)
    g = jnp.where(r < n_valid, g, NEG_INF)
    rank = jnp.zeros(g.shape, jnp.int32)
    for i in range(nb):
        gi = g[i:i + 1, :]
        rank = rank + ((gi > g) | ((gi == g) & (i < r))).astype(jnp.int32)
    return rank, r


def _moba_prompt_kernel(q_ref, k_ref, v_ref, o_ref, kb_ref, vt_ref, kmean_ref, sel_ref, *, tq, nb):
    qi = pl.program_id(2)

    @pl.when(qi == 0)
    def _():
        _stage_kv(k_ref, v_ref, kb_ref, vt_ref)
        kmean_ref[...] = jnp.zeros_like(kmean_ref)
        for j in range(nb):
            kmean_ref[j:j + 1, :] = jnp.mean(k_ref[j * MOBA_BLOCK:(j + 1) * MOBA_BLOCK, :], axis=0, keepdims=True)

    q = q_ref[...]
    km_hi, km_lo = _split_bf16(kmean_ref[...])
    qbs = []
    for hh, mask in enumerate(_head_masks(q.shape)):
        qm = jnp.where(mask, q, 0.0)
        q_hi, q_lo = _split_bf16(qm)
        gate = _dot_nt(km_hi, q_hi) + _dot_nt(km_lo, q_hi) + _dot_nt(km_hi, q_lo)
        rank, r = _rank_rows(gate, qi)
        sel_ref[hh] = ((rank < MOBA_TOPK) & (r < qi)).astype(F32)
        qbs.append((qm * (DH_A ** -0.5 * LOG2_E)).astype(BF16))

    def adjust(s, j, start, diagonal):
        if diagonal:
            return s
        blk = (j * KEY_TILE) // MOBA_BLOCK
        chosen = jnp.concatenate([sel_ref[g, pl.ds(blk, 1), :] for g in range(sel_ref.shape[0])], axis=1)
        return jnp.where(chosen > 0.5, s, NEG_INF)

    _store_heads(o_ref, _flash_group(qbs, qi, kb_ref, vt_ref, tq, adjust))


def _moba_prompt(q, k, v, nseq, seq):
    tq = ATT_TILE
    assert tq == MOBA_BLOCK and seq % MOBA_BLOCK == 0 and seq // MOBA_BLOCK >= MOBA_TOPK
    nq = seq // tq
    nb = seq // MOBA_BLOCK
    nbp = -(-nb // SUBLANES) * SUBLANES
    ng = MOBA_GROUP
    width = ng * HEAD_DIM
    return pl.pallas_call(
        functools.partial(_moba_prompt_kernel, tq=tq, nb=nb),
        out_shape=jax.ShapeDtypeStruct(q.shape, F32),
        grid=(nseq, H_A // ng, nq),
        in_specs=[pl.BlockSpec((tq, width), lambda n, p, i: (n * nq + i, p)),
                  pl.BlockSpec((seq, width), lambda n, p, i: (n, p)),
                  pl.BlockSpec((1, width, seq), lambda n, p, i: (n, p, 0))],
        out_specs=pl.BlockSpec((tq, width), lambda n, p, i: (n * nq + i, p)),
        scratch_shapes=[pltpu.VMEM((seq, width), BF16), pltpu.VMEM((seq // KEY_TILE, width, KEY_TILE), BF16),
                        pltpu.VMEM((nbp, width), F32), pltpu.VMEM((ng, nbp, tq), F32)],
        compiler_params=_params("arbitrary", "arbitrary", "arbitrary"),
        name="moba_prompt",
    )(q, k, v)


def _column_replicated(row):
    return jnp.concatenate(
        [jnp.transpose(jnp.broadcast_to(row[:, c * LANES:(c + 1) * LANES], (LANES, LANES)))
         for c in range(row.shape[1] // LANES)], axis=0)


def _head_sums(x, dh):
    return jnp.concatenate([jnp.sum(x[h * dh:(h + 1) * dh, :], axis=0, keepdims=True)
                            for h in range(x.shape[0] // dh)], axis=0)


def _moba_sample_select_kernel(pt_ref, q_ref, *refs, group, nblk):
    k_refs = refs[:group]
    idx_ref = refs[group]
    qcol_ref, gate_ref = refs[group + 1:]
    g = pl.program_id(1)

    @pl.when(g == 0)
    def _():
        qcol_ref[...] = _column_replicated(q_ref[0])
        gate_ref[...] = jnp.zeros_like(gate_ref)

    lane = lax.broadcasted_iota(jnp.int32, gate_ref.shape, 1)
    pages_per_block = MOBA_BLOCK // PAGE_SIZE
    qcol = qcol_ref[...]
    upd = jnp.zeros(gate_ref.shape, F32)
    for i in range(group):
        qk = _head_sums(k_refs[i][0] * qcol, DH_A)
        blk = (g * group + i) // pages_per_block
        val = jnp.sum(qk, axis=1, keepdims=True) * (1.0 / (PAGE_SIZE * pages_per_block))
        upd = upd + jnp.where(lane == blk, val, 0.0)
    gate_ref[...] += upd

    @pl.when(g == pl.num_programs(1) - 1)
    def _():
        gate = jnp.where(lane < nblk, gate_ref[...], NEG_INF)
        rank = jnp.zeros(gate.shape, jnp.int32)
        for i in range(nblk):
            gi = gate[:, i:i + 1]
            rank = rank + ((gi > gate) | ((gi == gate) & (i < lane))).astype(jnp.int32)
        out = jnp.zeros(gate.shape, F32)
        for t in range(MOBA_TOPK):
            it = jnp.sum(jnp.where((rank == t) & (lane < nblk), lane.astype(F32), 0.0), axis=1, keepdims=True)
            out = jnp.where(lane == t, it, out)
        idx_ref[0] = out.astype(jnp.int32)


def _moba_sample_select(q, pool_kt, page_table, nblk):
    nb = q.shape[0]
    group = PAGES_PER_STEP
    npg = nblk * (MOBA_BLOCK // PAGE_SIZE)
    assert npg % group == 0 and MOBA_TOPK <= nblk <= LANES
    in_specs = [pl.BlockSpec((1, 1, W_A), lambda n, g, pt: (n, 0, 0))]
    for i in range(group):
        in_specs.append(pl.BlockSpec((1, W_A, PAGE_SIZE), lambda n, g, pt, i=i: (pt[n, g * group + i], 0, 0)))
    return pl.pallas_call(
        functools.partial(_moba_sample_select_kernel, group=group, nblk=nblk),
        out_shape=jax.ShapeDtypeStruct((nb, H_A, LANES), jnp.int32),
        grid_spec=pltpu.PrefetchScalarGridSpec(
            num_scalar_prefetch=1,
            grid=(nb, npg // group),
            in_specs=in_specs,
            out_specs=pl.BlockSpec((1, H_A, LANES), lambda n, g, pt: (n, 0, 0)),
            scratch_shapes=[pltpu.VMEM((W_A, LANES), F32), pltpu.VMEM((H_A, LANES), F32)]),
        compiler_params=_params("arbitrary", "arbitrary"),
        name="moba_sample_select",
    )(page_table, q.reshape(nb, 1, W_A), *([pool_kt] * group))


def _moba_sample_attend_kernel(pt_ref, idx_ref, q_ref, kn_ref, vn_ref, *refs, n_sel, n_own, qblk):
    n_pages = n_sel + n_own
    k_refs = refs[:n_pages]
    v_refs = refs[n_pages:2 * n_pages]
    o_ref = refs[2 * n_pages]
    n = pl.program_id(0)
    h = pl.program_id(1)
    q = q_ref[0, pl.ds(h, 1), :] * (DH_A ** -0.5)
    qcol = _column_replicated(q)[0:DH_A]
    pages_per_block = MOBA_BLOCK // PAGE_SIZE
    logits = []
    for i in range(n_pages):
        s = jnp.sum(k_refs[i][0] * qcol, axis=0, keepdims=True)
        if i < n_sel:
            valid = idx_ref[n, h, i // pages_per_block] < qblk
            s = jnp.where(valid, s, NEG_INF)
        logits.append(s)
    s_new = jnp.sum(q * kn_ref[0, pl.ds(h, 1), :], axis=1, keepdims=True)
    m = s_new
    for s in logits:
        m = jnp.maximum(m, jnp.max(s, axis=1, keepdims=True))
    p_new = jnp.exp(s_new - m)
    l = p_new
    acc_t = jnp.zeros((DH_A, PAGE_SIZE), F32)
    for i, s in enumerate(logits):
        p = jnp.exp(s - m)
        l = l + jnp.sum(p, axis=1, keepdims=True)
        acc_t = acc_t + v_refs[i][0] * p
    acc = jnp.sum(jnp.transpose(jnp.concatenate([acc_t, jnp.zeros((LANES - DH_A, PAGE_SIZE), F32)], axis=0)),
                  axis=0, keepdims=True)
    res = (acc + p_new * vn_ref[0, pl.ds(h, 1), :]) / l
    row = lax.broadcasted_iota(jnp.int32, (H_A, LANES), 0)

    @pl.when(h == 0)
    def _():
        o_ref[0] = jnp.where(row == 0, res, 0.0)

    @pl.when(h > 0)
    def _():
        o_ref[0] = jnp.where(row == h, res, o_ref[0])


def _moba_sample_attend(q, k_new, v_new, pool_kt, pool_vt, page_table, idx):
    nb = q.shape[0]
    n_pages_total = page_table.shape[1]
    pages_per_block = MOBA_BLOCK // PAGE_SIZE
    past = n_pages_total * PAGE_SIZE
    qblk = past // MOBA_BLOCK
    n_sel = MOBA_TOPK * pages_per_block
    n_own = 1 if (past - PAGE_SIZE) // MOBA_BLOCK == qblk else 0

    def sel_map(i):
        def index_map(n, h, pt, ix):
            lpage = jnp.minimum(ix[n, h, i // pages_per_block] * pages_per_block + i % pages_per_block,
                                n_pages_total - 1)
            return (pt[n, lpage], h, 0)
        return index_map

    def own_map(n, h, pt, ix):
        return (pt[n, n_pages_total - 1], h, 0)

    page_specs = [pl.BlockSpec((1, DH_A, PAGE_SIZE), sel_map(i)) for i in range(n_sel)]
    page_specs += [pl.BlockSpec((1, DH_A, PAGE_SIZE), own_map)] * n_own
    tok_spec = pl.BlockSpec((1, H_A, LANES), lambda n, h, pt, ix: (n, 0, 0))
    tok = lambda a: _pad_lanes(a.reshape(nb, H_A, DH_A))
    n_pg = n_sel + n_own
    att = pl.pallas_call(
        functools.partial(_moba_sample_attend_kernel, n_sel=n_sel, n_own=n_own, qblk=qblk),
        out_shape=jax.ShapeDtypeStruct((nb, H_A, LANES), F32),
        grid_spec=pltpu.PrefetchScalarGridSpec(
            num_scalar_prefetch=2,
            grid=(nb, H_A),
            in_specs=[tok_spec, tok_spec, tok_spec] + page_specs + page_specs,
            out_specs=tok_spec),
        compiler_params=_params("arbitrary", "arbitrary"),
        name="moba_sample_attend",
    )(page_table, idx, tok(q), tok(k_new), tok(v_new), *([pool_kt] * n_pg), *([pool_vt] * n_pg))
    return att[:, :, :DH_A].reshape(nb, W_A)


def _mlstm_sample_kernel(qk_ref, st_ref, v_ref, ob_ref, gate_ref, c_ref, n_ref, m_ref, cw_ref, cb_ref, nw_ref,
                         g_ref, cn_ref, nn_ref, mn_ref):
    cw = cw_ref[...]
    st = st_ref[0]
    y = cb_ref[...] + cw[MLSTM_CONV - 1:MLSTM_CONV] * qk_ref[0]
    for t in range(MLSTM_CONV - 1):
        y = y + cw[t:t + 1] * st[t:t + 1]
    qk = y * _sigmoid(y)
    gates = gate_ref[0]
    m_in = m_ref[0]
    m_out = jnp.zeros((1, LANES), F32)
    lane = lax.broadcasted_iota(jnp.int32, (1, LANES), 1)
    for h in range(H_B):
        q = qk[:, h * DH_B:(h + 1) * DH_B]
        k = qk[:, W_B + h * DH_B:W_B + (h + 1) * DH_B] * (DH_B ** -0.5)
        v = v_ref[0][:, h * DH_B:(h + 1) * DH_B]
        ig = gates[:, h:h + 1]
        lf = gates[:, H_B + h:H_B + h + 1]
        m_old = m_in[:, h:h + 1]
        m_row = jnp.maximum(ig, lf + m_old)
        w_inter = jnp.exp(lf + m_old - m_row)
        w_s = jnp.exp(ig - m_row)
        s = jnp.sum(q * k, axis=1, keepdims=True) * w_s
        cmat = c_ref[0, h]
        nvec = n_ref[0, h:h + 1, :]
        qc = _dot(jnp.broadcast_to(q, (SUBLANES, DH_B)).astype(BF16), cmat.astype(BF16))[0:1]
        num = s * v + w_inter * qc
        den = s + w_inter * jnp.sum(q * nvec, axis=1, keepdims=True)
        hh = num / jnp.maximum(jnp.abs(den), jnp.exp(-m_row))
        k_col = jnp.transpose(jnp.broadcast_to(k, (DH_B, DH_B)))
        cn_ref[0, h] = w_inter * cmat + w_s * (k_col * v)
        nn_ref[0, h:h + 1, :] = w_inter * nvec + w_s * k
        m_out = jnp.where(lane == h, m_row, m_out)
        hn = _rms(hh, nw_ref[:, h * DH_B:(h + 1) * DH_B])
        g_ref[0, :, h * DH_B:(h + 1) * DH_B] = _sigmoid(ob_ref[0][:, h * DH_B:(h + 1) * DH_B]) * hn
    mn_ref[0] = m_out


def _mlstm_sample(qk_pre, conv_state, v_pre, ob, gates, c, nvec, m, conv_w, conv_b, norm_w):
    nb = qk_pre.shape[0]
    tok = lambda a: a.reshape(nb, 1, a.shape[-1])
    tspec = lambda w: pl.BlockSpec((1, 1, w), lambda n: (n, 0, 0))
    full = lambda shape: pl.BlockSpec(shape, lambda n: (0,) * len(shape))
    g, cn, nn, mn = pl.pallas_call(
        _mlstm_sample_kernel,
        out_shape=[jax.ShapeDtypeStruct((nb, 1, W_B), F32),
                   jax.ShapeDtypeStruct((nb, H_B, DH_B, DH_B), F32),
                   jax.ShapeDtypeStruct((nb, H_B, DH_B), F32),
                   jax.ShapeDtypeStruct((nb, 1, LANES), F32)],
        grid=(nb,),
        in_specs=[tspec(2 * W_B),
                  pl.BlockSpec((1, MLSTM_CONV - 1, 2 * W_B), lambda n: (n, 0, 0)),
                  tspec(W_B), tspec(W_B), tspec(LANES),
                  pl.BlockSpec((1, H_B, DH_B, DH_B), lambda n: (n, 0, 0, 0)),
                  pl.BlockSpec((1, H_B, DH_B), lambda n: (n, 0, 0)),
                  tspec(LANES),
                  full((MLSTM_CONV, 2 * W_B)), full((1, 2 * W_B)), full((1, W_B))],
        out_specs=[tspec(W_B),
                   pl.BlockSpec((1, H_B, DH_B, DH_B), lambda n: (n, 0, 0, 0)),
                   pl.BlockSpec((1, H_B, DH_B), lambda n: (n, 0, 0)),
                   tspec(LANES)],
        compiler_params=_params("arbitrary"),
        name="mlstm_sample",
    )(tok(qk_pre), conv_state, tok(v_pre), tok(ob), tok(gates), c, nvec, tok(_pad_lanes(m)),
      conv_w, conv_b.reshape(1, -1), norm_w.reshape(1, -1))
    return g.reshape(nb, W_B), cn, nn, mn[:, 0, :H_B]


def _fox_sample_kernel(pt_ref, q_ref, kn_ref, vn_ref, lfn_ref, *refs, group):
    k_refs = refs[:group]
    v_refs = refs[group:2 * group]
    lf_refs = refs[2 * group:3 * group]
    o_ref = refs[3 * group]
    qcol_ref, m_ref, l_ref, run_ref, acc_ref = refs[3 * group + 1:]
    g = pl.program_id(1)

    @pl.when(g == 0)
    def _():
        qcol_ref[...] = _column_replicated(q_ref[0] * (DH_C ** -0.5))
        m_ref[...] = jnp.full(m_ref.shape, NEG_INF, F32)
        l_ref[...] = jnp.zeros_like(l_ref)
        run_ref[...] = jnp.zeros_like(run_ref)
        acc_ref[...] = jnp.zeros_like(acc_ref)

    qcol = qcol_ref[...]
    run = run_ref[...]
    logits = []
    for i in range(group):
        qk = _head_sums(k_refs[i][0] * qcol, DH_C)
        cum = run + _scan_lanes(lf_refs[i][0], jnp.add, 0.0)
        run = cum[:, PAGE_SIZE - 1:PAGE_SIZE]
        logits.append(qk - cum)
    s = jnp.concatenate(logits, axis=1)
    m = m_ref[...]
    m_new = jnp.maximum(m, jnp.max(s, axis=1, keepdims=True))
    alpha = jnp.exp(m - m_new)
    p = jnp.exp(s - m_new)
    l = alpha * l_ref[...] + jnp.sum(p, axis=1, keepdims=True)
    m_ref[...], l_ref[...], run_ref[...] = m_new, l, run
    for h in range(H_C):
        rows = slice(h * DH_C, (h + 1) * DH_C)
        a = acc_ref[rows, :] * alpha[h:h + 1, :]
        for i in range(group):
            a = a + v_refs[i][0, rows, :] * p[h:h + 1, i * PAGE_SIZE:(i + 1) * PAGE_SIZE]
        acc_ref[rows, :] = a

    @pl.when(g == pl.num_programs(1) - 1)
    def _():
        head = lax.broadcasted_iota(jnp.int32, (H_C, W_C), 0)
        mine = (lax.broadcasted_iota(jnp.int32, (H_C, W_C), 1) // DH_C) == head
        q_bd = jnp.where(mine, jnp.broadcast_to(q_ref[0], (H_C, W_C)), 0.0) * (DH_C ** -0.5)
        hrow = lax.broadcasted_iota(jnp.int32, (H_C, LANES), 0)
        hlane = lax.broadcasted_iota(jnp.int32, (H_C, LANES), 1)
        lf_new = jnp.sum(jnp.where(hrow == hlane, jnp.broadcast_to(lfn_ref[0], (H_C, LANES)), 0.0),
                         axis=1, keepdims=True)
        s_new = jnp.sum(q_bd * kn_ref[0], axis=1, keepdims=True) - (run + lf_new)
        m_f = jnp.maximum(m_new, s_new)
        alpha_f = jnp.exp(m_new - m_f)
        p_new = jnp.exp(s_new - m_f)
        l_f = alpha_f * l + p_new
        past = jnp.concatenate([jnp.sum(jnp.transpose(acc_ref[c * LANES:(c + 1) * LANES, :]), axis=0, keepdims=True)
                                for c in range(W_C // LANES)], axis=1)
        per_lane = lambda col: jnp.sum(jnp.where(mine, col, 0.0), axis=0, keepdims=True)
        o_ref[0] = (past * per_lane(alpha_f) + per_lane(p_new) * vn_ref[0]) / per_lane(l_f)


def _fox_sample(q, k_new, v_new, lf_new, pool_k, pool_v, pool_lf_t, page_table):
    nb = q.shape[0]
    n_pages = page_table.shape[1]
    group = PAGES_PER_STEP
    assert n_pages % group == 0
    tok_spec = pl.BlockSpec((1, 1, W_C), lambda n, g, pt: (n, 0, 0))
    tok = lambda a: a.reshape(nb, 1, a.shape[-1])
    page = lambda i, shape: pl.BlockSpec(shape, lambda n, g, pt: (pt[n, g * group + i], 0, 0))
    in_specs = [tok_spec, tok_spec, tok_spec, pl.BlockSpec((1, 1, LANES), lambda n, g, pt: (n, 0, 0))]
    in_specs += [page(i, (1, W_C, PAGE_SIZE)) for i in range(group)]
    in_specs += [page(i, (1, W_C, PAGE_SIZE)) for i in range(group)]
    in_specs += [page(i, (1, H_C, PAGE_SIZE)) for i in range(group)]
    return pl.pallas_call(
        functools.partial(_fox_sample_kernel, group=group),
        out_shape=jax.ShapeDtypeStruct((nb, 1, W_C), F32),
        grid_spec=pltpu.PrefetchScalarGridSpec(
            num_scalar_prefetch=1,
            grid=(nb, n_pages // group),
            in_specs=in_specs,
            out_specs=tok_spec,
            scratch_shapes=[pltpu.VMEM((W_C, LANES), F32), pltpu.VMEM((H_C, 1), F32), pltpu.VMEM((H_C, 1), F32),
                            pltpu.VMEM((H_C, 1), F32), pltpu.VMEM((W_C, PAGE_SIZE), F32)]),
        compiler_params=_params("arbitrary", "arbitrary"),
        name="fox_sample",
    )(page_table, tok(q), tok(k_new), tok(v_new), tok(lf_new),
      *([pool_k] * group), *([pool_v] * group), *([pool_lf_t] * group)).reshape(nb, W_C)


EVEN_SEGS = ((0, W_A), (W_A, W_A), (2 * W_A, W_A), (3 * W_A, 2 * W_B), (3 * W_A + 2 * W_B, W_B),
             (3 * W_A + 3 * W_B, W_B), (3 * W_A + 4 * W_B, LANES))
ODD_SEGS = ((0, W_C), (W_C, W_C), (2 * W_C, W_C), (3 * W_C, LANES))


def _even_weights(w_in, b_i, b_f):
    main = 3 * W_A + 4 * W_B
    w_pad = jnp.concatenate([w_in[:, :main], _pad_lanes(w_in[:, main:])], axis=1)
    w_bf = w_pad.astype(BF16)
    wq = w_in[:, :W_A]
    w_lo = (wq - wq.astype(BF16).astype(F32)).astype(BF16)
    gate_bias = _pad_lanes(jnp.concatenate([b_i, b_f]).reshape(1, -1))
    return w_bf, w_lo, gate_bias


def _odd_weights(w_in, b_f):
    w_pad = jnp.concatenate([w_in[:, :3 * W_C], _pad_lanes(w_in[:, 3 * W_C:])], axis=1)
    return w_pad.astype(BF16), _pad_lanes(b_f.reshape(1, -1))


def _feature_major(pool):
    n_pool, page, heads, dh = pool.shape
    return jnp.transpose(pool, (0, 2, 3, 1)).reshape(n_pool, heads * dh, page)


def _time_minor_to_cache(a_t, heads):
    n, width, seq = a_t.shape
    return jnp.transpose(a_t.reshape(n, heads, width // heads, seq), (0, 3, 1, 2))


def _rows_to_sublanes(a):
    return jnp.pad(a.T, ((0, SUBLANES - a.shape[1]), (0, 0)))


def kernel(x_prompt, x_sample, cache_moba_k, cache_moba_v, state_mlstm_c, state_mlstm_n, state_mlstm_m,
           state_mlstm_conv, cache_fox_k, cache_fox_v, cache_fox_logf, state_ffn_conv, page_table, norm_w,
           even_w_in, mlstm_conv_w, mlstm_conv_b, mlstm_b_i, mlstm_b_f, mlstm_norm_w, even_w_out, fox_w_in,
           fox_b_f, fox_w_out, ffn_w_in, ffn_conv_w, ffn_conv_b, ffn_w_out):
    bp, sp, d = x_prompt.shape
    bs, ss, _ = x_sample.shape
    assert ss == 1 and d == D_MODEL
    depth = norm_w.shape[0]
    n_pages = page_table.shape[1]
    tp = bp * sp
    yp = x_prompt.reshape(tp, d)
    ys = x_sample.reshape(bs, d)
    outs = {name: [] for name in (
        "mk_p", "mv_p", "mk_s", "mv_s", "mc_p", "mn_p", "mm_p", "mconv_p", "mc_s", "mn_s", "mm_s", "mconv_s",
        "fk_p", "fv_p", "fl_p", "fk_s", "fv_s", "fl_s", "ff_p", "ff_s")}
    for layer in range(depth):
        nw = norm_w[layer]
        if layer % 2 == 0:
            e = layer // 2
            w_bf, w_lo, gate_bias = _even_weights(even_w_in[e], mlstm_b_i[e], mlstm_b_f[e])
            w_out_bf = even_w_out[e].astype(BF16)
            proj = lambda x, tm, **kw: _rms_proj(x, nw[0], w_bf, w_lo, gate_bias, EVEN_SEGS, 1, (H_B, 2 * H_B), tm,
                                                 **kw)
            qa, ka, ka_t, va_t, qkb, vb, ob, gates = proj(yp, ROW_TILE, cols={1: True, 2: False}, seq_rows=sp)
            att = _moba_prompt(qa, ka, va_t, bp, sp)
            b_t, m_t, mrow_t = _mlstm_gate_scan(_rows_to_sublanes(gates[:, :H_B]),
                                                _rows_to_sublanes(gates[:, H_B:2 * H_B]), sp)
            cols = _pad_lanes(jnp.concatenate([b_t[:H_B].T, m_t[:H_B].T, mrow_t[:H_B].T], axis=1))
            gated, cc, nn, mfin = _mlstm_prompt(qkb, vb, ob, b_t, cols, mlstm_conv_w[e], mlstm_conv_b[e],
                                                mlstm_norm_w[e], bp, sp)
            yp = _proj_res([att, gated], [w_out_bf[:W_A], w_out_bf[W_A:]], yp, nw[1], FFN_ROWS)
            outs["mk_p"].append(_time_minor_to_cache(ka_t, H_A))
            outs["mv_p"].append(_time_minor_to_cache(va_t, H_A))
            outs["mc_p"].append(cc)
            outs["mn_p"].append(nn)
            outs["mm_p"].append(mfin[:, 0, 2 * H_B:3 * H_B])
            outs["mconv_p"].append(qkb.reshape(bp, sp, 2 * W_B)[:, sp - (MLSTM_CONV - 1):])
            qa, ka, va, qkb, vb, ob, gates = proj(ys, bs)
            pool_kt = _feature_major(cache_moba_k[e])
            pool_vt = _feature_major(cache_moba_v[e])
            nblk = (n_pages * PAGE_SIZE) // MOBA_BLOCK
            idx = _moba_sample_select(qa, pool_kt, page_table, nblk)[:, :, :MOBA_TOPK]
            att = _moba_sample_attend(qa, ka, va, pool_kt, pool_vt, page_table, idx)
            gated, cc, nn, mm = _mlstm_sample(qkb, state_mlstm_conv[e], vb, ob, gates, state_mlstm_c[e],
                                              state_mlstm_n[e], state_mlstm_m[e], mlstm_conv_w[e],
                                              mlstm_conv_b[e], mlstm_norm_w[e])
            ys = _proj_res([att, gated], [w_out_bf[:W_A], w_out_bf[W_A:]], ys, nw[1], bs)
            outs["mk_s"].append(ka.reshape(bs, ss, H_A, DH_A))
            outs["mv_s"].append(va.reshape(bs, ss, H_A, DH_A))
            outs["mc_s"].append(cc)
            outs["mn_s"].append(nn)
            outs["mm_s"].append(mm)
            outs["mconv_s"].append(jnp.concatenate([state_mlstm_conv[e][:, 1:], qkb[:, None, :]], axis=1))
        else:
            o = layer // 2
            w_bf, gate_bias = _odd_weights(fox_w_in[o], fox_b_f[o])
            w_out_bf = fox_w_out[o].astype(BF16)
            proj = lambda x, tm, **kw: _rms_proj(x, nw[0], w_bf, None, gate_bias, ODD_SEGS, 0, (0, H_C), tm, **kw)
            q, k, k_t, v_t, lf = proj(yp, ROW_TILE, cols={1: True, 2: False}, seq_rows=sp)
            lf_t = lf[:, :H_C].T
            crow = _cumsum_time(lf_t, sp).reshape(H_C // FOX_GROUP, FOX_GROUP, tp)
            att = _fox_prompt(q, k, v_t, crow, bp, sp)
            yp = _proj_res([att], [w_out_bf], yp, nw[1], FFN_ROWS)
            outs["fk_p"].append(_time_minor_to_cache(k_t, H_C))
            outs["fv_p"].append(_time_minor_to_cache(v_t, H_C))
            outs["fl_p"].append(lf[:, :H_C].reshape(bp, sp, H_C))
            q, k, v, lf = proj(ys, bs)
            pool_lf_t = jnp.transpose(cache_fox_logf[o], (0, 2, 1))
            att = _fox_sample(q, k, v, lf, _feature_major(cache_fox_k[o]), _feature_major(cache_fox_v[o]),
                              pool_lf_t, page_table)
            ys = _proj_res([att], [w_out_bf], ys, nw[1], bs)
            outs["fk_s"].append(k.reshape(bs, ss, H_C, DH_C))
            outs["fv_s"].append(v.reshape(bs, ss, H_C, DH_C))
            outs["fl_s"].append(lf[:, :H_C].reshape(bs, ss, H_C))
        w_in_bf = ffn_w_in[layer].astype(BF16)
        w_o_bf = ffn_w_out[layer].astype(BF16)
        yp, buf_p = _ffn(yp, nw[2], w_in_bf, ffn_conv_w[layer], ffn_conv_b[layer], w_o_bf, nw[3], FFN_ROWS, sp)
        ys, u_s = _ffn(ys, nw[2], w_in_bf, ffn_conv_w[layer], ffn_conv_b[layer], w_o_bf, nw[3], bs, 1,
                       state=state_ffn_conv[layer])
        outs["ff_p"].append(buf_p)
        outs["ff_s"].append(jnp.concatenate([state_ffn_conv[layer][:, 1:], u_s[:, None, :]], axis=1))
    st = {name: jnp.stack(vals) for name, vals in outs.items()}
    return (yp.reshape(bp, sp, d), ys.reshape(bs, ss, d), st["mk_p"], st["mv_p"], st["mk_s"], st["mv_s"],
            st["mc_p"], st["mn_p"], st["mm_p"], st["mconv_p"], st["mc_s"], st["mn_s"], st["mm_s"], st["mconv_s"],
            st["fk_p"], st["fv_p"], st["fl_p"], st["fk_s"], st["fv_s"], st["fl_s"], st["ff_p"], st["ff_s"])
```

```python
import functools

import jax
import jax.numpy as jnp
from jax import lax
from jax.experimental import pallas as pl
from jax.experimental.pallas import tpu as pltpu

F32 = jnp.float32
BF16 = jnp.bfloat16
NEG_INF = float("-inf")

D_MODEL = 1024
PAGE_SIZE = 128
H_A, DH_A = 8, 64
W_A = H_A * DH_A
MOBA_BLOCK = 256
MOBA_TOPK = 3
H_B, DH_B = 4, 128
W_B = H_B * DH_B
MLSTM_CONV = 4
H_C, DH_C = 16, 64
W_C = H_C * DH_C
D_FF = 2816
FFN_CONV = 3
RMS_EPS = 1e-6

LANES = 128
SUBLANES = 8
VMEM_LIMIT = 56 * 1024 * 1024

ROW_TILE = 256
ATT_TILE = 256
FOX_TILE = 512
KEY_TILE = 256
KEY_UNROLL = 4
MOBA_GROUP = 8
FOX_GROUP = 4
MLSTM_TILE = 256
FFN_ROWS = 512
FFN_COLS = 1408
FFN_SLICE = 256
PAGES_PER_STEP = 8


def _params(*sem):
    return pltpu.CompilerParams(dimension_semantics=sem, vmem_limit_bytes=VMEM_LIMIT)


def _rms(x, w):
    return x * lax.rsqrt(jnp.mean(x * x, axis=-1, keepdims=True) + RMS_EPS) * w


def _sigmoid(x):
    return 1.0 / (1.0 + jnp.exp(-x))


def _log_sigmoid(x):
    return -(jnp.maximum(-x, 0.0) + jnp.log1p(jnp.exp(-jnp.abs(x))))


def _gelu_tanh(x):
    c = 0.7978845608028654
    return 0.5 * x * (1.0 + jnp.tanh(c * (x + 0.044715 * (x * x * x))))


def _dot(a, b):
    return jnp.dot(a, b, preferred_element_type=F32)


def _dot_nt(a, b):
    return lax.dot_general(a, b, (((1,), (1,)), ((), ())), preferred_element_type=F32)


def _split_bf16(x):
    hi = x.astype(BF16)
    lo = (x - hi.astype(F32)).astype(BF16)
    return hi, lo


def _pad_lanes(a, width=LANES):
    return jnp.pad(a, ((0, 0),) * (a.ndim - 1) + ((0, width - a.shape[-1]),))


def _rms_proj_kernel(*refs, segs, n_hi, gate_lf, cols):
    x_ref, nw_ref, w_ref = refs[:3]
    pos = 3
    wlo_ref = None
    if n_hi:
        wlo_ref = refs[pos]
        pos += 1
    gb_ref = refs[pos]
    out_refs = list(refs[pos + 1:])
    h = _rms(x_ref[...], nw_ref[...])
    hb, hl = _split_bf16(h)
    lo_start = 0
    for i, (start, width) in enumerate(segs):
        w = w_ref[:, start:start + width]
        z = _dot(hb, w)
        if i < n_hi:
            z = z + _dot(hl, w) + _dot(hb, wlo_ref[:, lo_start:lo_start + width])
            lo_start += width
        if i == len(segs) - 1:
            z = z + gb_ref[...]
            lane = lax.broadcasted_iota(jnp.int32, z.shape, 1)
            z = jnp.where((lane >= gate_lf[0]) & (lane < gate_lf[1]), _log_sigmoid(z), z)
        if cols.get(i, True):
            out_refs.pop(0)[...] = z
        if i in cols:
            zt_ref = out_refs.pop(0)
            for r in range(z.shape[0] // LANES):
                for c in range(width // LANES):
                    zt_ref[0, c * LANES:(c + 1) * LANES, r * LANES:(r + 1) * LANES] = jnp.transpose(
                        z[r * LANES:(r + 1) * LANES, c * LANES:(c + 1) * LANES])


def _rms_proj(x, nw, w_bf, w_lo, gate_bias, segs, n_hi, gate_lf, tm, cols=None, seq_rows=None):
    rows, d = x.shape
    ncols = w_bf.shape[1]
    cols = cols or {}
    in_specs = [pl.BlockSpec((tm, d), lambda i: (i, 0)),
                pl.BlockSpec((1, d), lambda i: (0, 0)),
                pl.BlockSpec((d, ncols), lambda i: (0, 0))]
    args = [x, nw.reshape(1, d), w_bf]
    if n_hi:
        in_specs.append(pl.BlockSpec(w_lo.shape, lambda i: (0, 0)))
        args.append(w_lo)
    in_specs.append(pl.BlockSpec((1, LANES), lambda i: (0, 0)))
    args.append(gate_bias)
    out_shape, out_specs = [], []
    for i, (_, wd) in enumerate(segs):
        if cols.get(i, True):
            out_shape.append(jax.ShapeDtypeStruct((rows, wd), F32))
            out_specs.append(pl.BlockSpec((tm, wd), lambda i: (i, 0)))
        if i in cols:
            tps = seq_rows // tm
            out_shape.append(jax.ShapeDtypeStruct((rows // seq_rows, wd, seq_rows), F32))
            out_specs.append(pl.BlockSpec((1, wd, tm), lambda i: (i // tps, 0, i % tps)))
    return pl.pallas_call(
        functools.partial(_rms_proj_kernel, segs=segs, n_hi=n_hi, gate_lf=gate_lf, cols=cols),
        out_shape=out_shape,
        grid=(rows // tm,),
        in_specs=in_specs,
        out_specs=out_specs,
        compiler_params=_params("arbitrary"),
        name="rms_proj",
    )(*args)


def _proj_res_kernel(*refs, n_in):
    a_refs = refs[:n_in]
    w_refs = refs[n_in:2 * n_in]
    x_ref, nw_ref, o_ref = refs[2 * n_in:]
    acc = None
    for a_ref, w_ref in zip(a_refs, w_refs):
        z = _dot(a_ref[...].astype(BF16), w_ref[...])
        acc = z if acc is None else acc + z
    o_ref[...] = x_ref[...] + _rms(acc, nw_ref[...])


def _proj_res(acts, ws, x, nw, tm):
    rows, d = x.shape
    n_in = len(acts)
    in_specs = ([pl.BlockSpec((tm, a.shape[1]), lambda i: (i, 0)) for a in acts]
                + [pl.BlockSpec(w.shape, lambda i: (0, 0)) for w in ws]
                + [pl.BlockSpec((tm, d), lambda i: (i, 0)), pl.BlockSpec((1, d), lambda i: (0, 0))])
    return pl.pallas_call(
        functools.partial(_proj_res_kernel, n_in=n_in),
        out_shape=jax.ShapeDtypeStruct((rows, d), F32),
        grid=(rows // tm,),
        in_specs=in_specs,
        out_specs=pl.BlockSpec((tm, d), lambda i: (i, 0)),
        compiler_params=_params("arbitrary"),
        name="proj_res",
    )(*acts, *ws, x, nw.reshape(1, d))


def _ffn_kernel(*refs, tm, tf, nj, tiles_per_seq, stateful):
    (x_ref, nw_in_ref, wa_ref, wb_ref, cwa_ref, cwb_ref, cba_ref, cbb_ref, wo_ref, nw_out_ref) = refs[:10]
    pos = 10
    if stateful:
        s0a_ref, s0b_ref, s1a_ref, s1b_ref = refs[pos:pos + 4]
        pos += 4
    y_ref, oa_ref, ob_ref = refs[pos:pos + 3]
    pos += 3
    hb_ref, acc_ref = refs[pos:pos + 2]
    pos += 2
    if not stateful:
        ubuf_ref, carry_ref = refs[pos:pos + 2]
    i = pl.program_id(0)
    j = pl.program_id(1)

    @pl.when(j == 0)
    def _():
        hb_ref[...] = _rms(x_ref[...], nw_in_ref[...]).astype(BF16)
        acc_ref[...] = jnp.zeros_like(acc_ref)

    hb = hb_ref[...]
    if not stateful:
        first = i % tiles_per_seq == 0

        @pl.when(first)
        def _():
            ubuf_ref[:, 0:SUBLANES, :] = jnp.zeros((2, SUBLANES, tf), F32)

        @pl.when(jnp.logical_not(first))
        def _():
            ubuf_ref[:, 0:SUBLANES, :] = carry_ref[:, j]

    gs = []
    for c0 in range(0, tf, FFN_SLICE):
        cs = slice(c0, min(c0 + FFN_SLICE, tf))
        halves = []
        for half, (w_ref, cw_ref, cb_ref) in enumerate(((wa_ref, cwa_ref, cba_ref), (wb_ref, cwb_ref, cbb_ref))):
            u = _dot(hb, w_ref[:, cs])
            cw = cw_ref[:, cs]
            if stateful:
                s0 = (s0a_ref, s0b_ref)[half][:, cs]
                s1 = (s1a_ref, s1b_ref)[half][:, cs]
                y = cw[0:1] * s0 + cw[1:2] * s1 + cw[2:3] * u + cb_ref[:, cs]
                (oa_ref, ob_ref)[half][:, cs] = u
            else:
                ubuf_ref[half, SUBLANES:, cs] = u
                y = (cw[0:1] * ubuf_ref[half, SUBLANES - 2:SUBLANES - 2 + tm, cs]
                     + cw[1:2] * ubuf_ref[half, SUBLANES - 1:SUBLANES - 1 + tm, cs]
                     + cw[2:3] * u + cb_ref[:, cs])
            halves.append(y)
        gs.append((_gelu_tanh(halves[0]) * halves[1]).astype(BF16))
    if not stateful:
        carry_ref[:, j] = ubuf_ref[:, tm:tm + SUBLANES, :]
        oa_ref[0] = ubuf_ref[0, tm + SUBLANES - 2:tm + SUBLANES, :]
        ob_ref[0] = ubuf_ref[1, tm + SUBLANES - 2:tm + SUBLANES, :]
    acc_ref[...] += _dot(jnp.concatenate(gs, axis=1), wo_ref[...])

    @pl.when(j == nj - 1)
    def _():
        y_ref[...] = x_ref[...] + _rms(acc_ref[...], nw_out_ref[...])


def _ffn(x, nw_in, w_in_bf, conv_w, conv_b, w_out_bf, nw_out, tm, seq_rows, state=None):
    rows, d = x.shape
    tf = FFN_COLS
    nj = D_FF // tf
    stateful = state is not None
    cb2 = conv_b.reshape(1, 2 * D_FF)
    in_specs = [pl.BlockSpec((tm, d), lambda i, j: (i, 0)),
                pl.BlockSpec((1, d), lambda i, j: (0, 0)),
                pl.BlockSpec((d, tf), lambda i, j: (0, j)),
                pl.BlockSpec((d, tf), lambda i, j: (0, nj + j)),
                pl.BlockSpec((FFN_CONV, tf), lambda i, j: (0, j)),
                pl.BlockSpec((FFN_CONV, tf), lambda i, j: (0, nj + j)),
                pl.BlockSpec((1, tf), lambda i, j: (0, j)),
                pl.BlockSpec((1, tf), lambda i, j: (0, nj + j)),
                pl.BlockSpec((tf, d), lambda i, j: (j, 0)),
                pl.BlockSpec((1, d), lambda i, j: (0, 0))]
    args = [x, nw_in.reshape(1, d), w_in_bf, w_in_bf, conv_w, conv_w, cb2, cb2, w_out_bf, nw_out.reshape(1, d)]
    scratch = [pltpu.VMEM((tm, d), BF16), pltpu.VMEM((tm, d), F32)]
    if stateful:
        s0, s1 = state[:, 0, :], state[:, 1, :]
        in_specs += [pl.BlockSpec((tm, tf), lambda i, j: (i, j)), pl.BlockSpec((tm, tf), lambda i, j: (i, nj + j)),
                     pl.BlockSpec((tm, tf), lambda i, j: (i, j)), pl.BlockSpec((tm, tf), lambda i, j: (i, nj + j))]
        args += [s0, s0, s1, s1]
        out_shape = [jax.ShapeDtypeStruct((rows, d), F32), jax.ShapeDtypeStruct((rows, D_FF), F32),
                     jax.ShapeDtypeStruct((rows, D_FF), F32)]
        out_specs = [pl.BlockSpec((tm, d), lambda i, j: (i, 0)), pl.BlockSpec((tm, tf), lambda i, j: (i, j)),
                     pl.BlockSpec((tm, tf), lambda i, j: (i, j))]
        tiles_per_seq = 1
    else:
        tiles_per_seq = seq_rows // tm
        ntiles = rows // tm
        out_shape = [jax.ShapeDtypeStruct((rows, d), F32), jax.ShapeDtypeStruct((ntiles, FFN_CONV - 1, D_FF), F32),
                     jax.ShapeDtypeStruct((ntiles, FFN_CONV - 1, D_FF), F32)]
        out_specs = [pl.BlockSpec((tm, d), lambda i, j: (i, 0)),
                     pl.BlockSpec((1, FFN_CONV - 1, tf), lambda i, j: (i, 0, j)),
                     pl.BlockSpec((1, FFN_CONV - 1, tf), lambda i, j: (i, 0, j))]
        scratch += [pltpu.VMEM((2, tm + SUBLANES, tf), F32), pltpu.VMEM((2, nj, SUBLANES, tf), F32)]
    y, ua, ub = pl.pallas_call(
        functools.partial(_ffn_kernel, tm=tm, tf=tf, nj=nj, tiles_per_seq=tiles_per_seq, stateful=stateful),
        out_shape=out_shape,
        grid=(rows // tm, nj),
        in_specs=in_specs,
        out_specs=out_specs,
        scratch_shapes=scratch,
        compiler_params=_params("arbitrary", "arbitrary"),
        name="conv_ffn",
    )(*args)
    if not stateful:
        ua, ub = ua[tiles_per_seq - 1::tiles_per_seq], ub[tiles_per_seq - 1::tiles_per_seq]
    return y, jnp.concatenate([ua, ub], axis=-1)


def _scan_lanes(x, op, fill):
    n = x.shape[1]
    lane = lax.broadcasted_iota(jnp.int32, x.shape, 1)
    s = 1
    while s < n:
        x = op(x, jnp.where(lane >= s, pltpu.roll(x, s, axis=1), fill))
        s *= 2
    return x


def _mlstm_gate_scan_kernel(ig_ref, lf_ref, b_ref, m_ref, mrow_ref):
    a = _scan_lanes(lf_ref[...], jnp.add, 0.0)
    b = ig_ref[...] - a
    m = jnp.maximum(_scan_lanes(b, jnp.maximum, NEG_INF), 0.0)
    b_ref[...] = b
    m_ref[...] = m
    mrow_ref[...] = a + m


def _mlstm_gate_scan(ig_t, lf_t, seq):
    rows, total = ig_t.shape
    spec = pl.BlockSpec((rows, seq), lambda n: (0, n))
    return pl.pallas_call(
        _mlstm_gate_scan_kernel,
        out_shape=[jax.ShapeDtypeStruct((rows, total), F32)] * 3,
        grid=(total // seq,),
        in_specs=[spec, spec],
        out_specs=[spec, spec, spec],
        compiler_params=_params("arbitrary"),
        name="mlstm_gate_scan",
    )(ig_t, lf_t)


def _cumsum_kernel(x_ref, o_ref):
    o_ref[...] = _scan_lanes(x_ref[...], jnp.add, 0.0)


def _cumsum_time(x_t, seq):
    rows, total = x_t.shape
    spec = pl.BlockSpec((rows, seq), lambda n: (0, n))
    return pl.pallas_call(
        _cumsum_kernel,
        out_shape=jax.ShapeDtypeStruct((rows, total), F32),
        grid=(total // seq,),
        in_specs=[spec],
        out_specs=spec,
        compiler_params=_params("arbitrary"),
        name="logf_cumsum",
    )(x_t)


def _mlstm_prompt_kernel(qk_ref, v_ref, ob_ref, rows_ref, cols_ref, cw_ref, cb_ref, nw_ref,
                         g_ref, c_ref, n_ref, mfin_ref, xbuf_ref, mprev_ref, *, tl):
    c = pl.program_id(1)
    nc = pl.num_programs(1)

    @pl.when(c == 0)
    def _():
        xbuf_ref[0:SUBLANES, :] = jnp.zeros((SUBLANES, 2 * W_B), F32)
        mprev_ref[...] = jnp.zeros_like(mprev_ref)
        c_ref[...] = jnp.zeros_like(c_ref)
        n_ref[...] = jnp.zeros_like(n_ref)

    xbuf_ref[SUBLANES:, :] = qk_ref[...]
    cw = cw_ref[...]
    y = cb_ref[...]
    for t in range(MLSTM_CONV):
        off = SUBLANES - (MLSTM_CONV - 1) + t
        y = y + cw[t:t + 1] * xbuf_ref[off:off + tl, :]
    xbuf_ref[0:SUBLANES, :] = xbuf_ref[tl:tl + SUBLANES, :]
    qk = y * _sigmoid(y)

    cols = cols_ref[...]
    mprev = mprev_ref[...]
    t_idx = lax.broadcasted_iota(jnp.int32, (tl, tl), 0)
    s_idx = lax.broadcasted_iota(jnp.int32, (tl, tl), 1)
    causal = s_idx <= t_idx
    for h in range(H_B):
        q = qk[:, h * DH_B:(h + 1) * DH_B]
        k = qk[:, W_B + h * DH_B:W_B + (h + 1) * DH_B] * (DH_B ** -0.5)
        v = v_ref[:, h * DH_B:(h + 1) * DH_B]
        qb, kb, vb = q.astype(BF16), k.astype(BF16), v.astype(BF16)
        b_row = rows_ref[h:h + 1, :]
        b_col = cols[:, h:h + 1]
        m_col = cols[:, H_B + h:H_B + h + 1]
        mrow_col = cols[:, 2 * H_B + h:2 * H_B + h + 1]
        m_last = cols[tl - 1:tl, H_B + h:H_B + h + 1]
        m_prev = mprev[:, H_B + h:H_B + h + 1]
        dmat = jnp.exp(jnp.where(causal, b_row - m_col, NEG_INF))
        s = _dot_nt(qb, kb) * dmat
        w_inter = jnp.exp(m_prev - m_col)
        cmat = c_ref[0, h]
        nvec = n_ref[0, h:h + 1, :]
        num = _dot(s.astype(BF16), vb) + w_inter * _dot(qb, cmat.astype(BF16))
        den = jnp.sum(s, axis=1, keepdims=True) + w_inter * jnp.sum(q * nvec, axis=1, keepdims=True)
        hh = num / jnp.maximum(jnp.abs(den), jnp.exp(-mrow_col))
        w_s = jnp.exp(b_col - m_last)
        w_c = jnp.exp(m_prev - m_last)
        kw = k * w_s
        c_ref[0, h] = w_c * cmat + lax.dot_general(kw.astype(BF16), vb, (((0,), (0,)), ((), ())),
                                                    preferred_element_type=F32)
        n_ref[0, h:h + 1, :] = w_c * nvec + jnp.sum(kw, axis=0, keepdims=True)
        hn = _rms(hh, nw_ref[:, h * DH_B:(h + 1) * DH_B])
        g_ref[:, h * DH_B:(h + 1) * DH_B] = _sigmoid(ob_ref[:, h * DH_B:(h + 1) * DH_B]) * hn
    mprev_ref[...] = cols[tl - 1:tl, :]

    @pl.when(c == nc - 1)
    def _():
        mfin_ref[0] = cols[tl - 1:tl, :]


def _mlstm_prompt(qk_pre, v_pre, ob, rows_pack, cols_pack, conv_w, conv_b, norm_w, nseq, seq):
    tl = MLSTM_TILE
    nc = seq // tl
    total = nseq * seq
    rmap = lambda n, c: (n * nc + c, 0)
    return pl.pallas_call(
        functools.partial(_mlstm_prompt_kernel, tl=tl),
        out_shape=[jax.ShapeDtypeStruct((total, W_B), F32),
                   jax.ShapeDtypeStruct((nseq, H_B, DH_B, DH_B), F32),
                   jax.ShapeDtypeStruct((nseq, H_B, DH_B), F32),
                   jax.ShapeDtypeStruct((nseq, 1, LANES), F32)],
        grid=(nseq, nc),
        in_specs=[pl.BlockSpec((tl, 2 * W_B), rmap),
                  pl.BlockSpec((tl, W_B), rmap),
                  pl.BlockSpec((tl, W_B), rmap),
                  pl.BlockSpec((SUBLANES, tl), lambda n, c: (0, n * nc + c)),
                  pl.BlockSpec((tl, LANES), rmap),
                  pl.BlockSpec((MLSTM_CONV, 2 * W_B), lambda n, c: (0, 0)),
                  pl.BlockSpec((1, 2 * W_B), lambda n, c: (0, 0)),
                  pl.BlockSpec((1, W_B), lambda n, c: (0, 0))],
        out_specs=[pl.BlockSpec((tl, W_B), rmap),
                   pl.BlockSpec((1, H_B, DH_B, DH_B), lambda n, c: (n, 0, 0, 0)),
                   pl.BlockSpec((1, H_B, DH_B), lambda n, c: (n, 0, 0)),
                   pl.BlockSpec((1, 1, LANES), lambda n, c: (n, 0, 0))],
        scratch_shapes=[pltpu.VMEM((tl + SUBLANES, 2 * W_B), F32), pltpu.VMEM((1, LANES), F32)],
        compiler_params=_params("arbitrary", "arbitrary"),
        name="mlstm_prompt",
    )(qk_pre, v_pre, ob, rows_pack, cols_pack, conv_w, conv_b.reshape(1, -1), norm_w.reshape(1, -1))


HEAD_DIM = 64
LOG2_E = 1.4426950408889634


def _head_masks(shape):
    head = lax.broadcasted_iota(jnp.int32, shape, 1) // HEAD_DIM
    return [head == g for g in range(shape[1] // HEAD_DIM)]


def _stage_kv(k_ref, vt_in_ref, kb_ref, vt_ref):
    kb_ref[...] = k_ref[...].astype(BF16)
    for j in range(k_ref.shape[0] // KEY_TILE):
        vt_ref[j] = vt_in_ref[0, :, j * KEY_TILE:(j + 1) * KEY_TILE].astype(BF16)


def _flash_group(qbs, qi, kb_ref, vt_ref, tq, adjust):
    tk = KEY_TILE
    per = tq // tk
    ng = len(qbs)
    krow = lax.broadcasted_iota(jnp.int32, (tk, ng * tq), 0)
    qcol = lax.broadcasted_iota(jnp.int32, (tk, ng * tq), 1) % tq
    q_cat = jnp.concatenate(qbs, axis=0)

    def logits(j, diagonal):
        start = pl.multiple_of(j * tk, tk)
        s = adjust(_dot_nt(kb_ref[pl.ds(start, tk), :], q_cat), j, start, diagonal is not None)
        if diagonal is not None:
            s = jnp.where(krow + diagonal * tk <= qcol, s, NEG_INF)
        return s

    def softmax(m, l, s):
        m_new = jnp.maximum(m, jnp.max(s, axis=0, keepdims=True))
        alpha = jnp.exp2(m - m_new)
        p = jnp.exp2(s - m_new)
        return m_new, alpha * l + jnp.sum(p, axis=0, keepdims=True), alpha, p.astype(BF16)

    def accumulate(accs, alpha, pb, j):
        return tuple(alpha[:, g * tq:(g + 1) * tq] * accs[g]
                     + _dot(vt_ref[j, g * HEAD_DIM:(g + 1) * HEAD_DIM, :], pb[:, g * tq:(g + 1) * tq])
                     for g in range(ng))

    m = jnp.full((1, ng * tq), NEG_INF, F32)
    l = jnp.zeros((1, ng * tq), F32)
    accs = tuple(jnp.zeros((HEAD_DIM, tq), F32) for _ in range(ng))
    def trip(first, count, carry, diagonal=False):
        m, l, accs = carry
        tiles = [first + u for u in range(count)]
        ss = [logits(j, u if diagonal else None) for u, j in enumerate(tiles)]
        for j, s in zip(tiles, ss):
            m, l, alpha, pb = softmax(m, l, s)
            accs = accumulate(accs, alpha, pb, j)
        return m, l, accs

    n = qi * per
    carry = trip(n, per, (m, l, accs), diagonal=True)
    carry = lax.fori_loop(0, n // KEY_UNROLL, lambda t, c: trip(t * KEY_UNROLL, KEY_UNROLL, c), carry)
    m, l, accs = lax.fori_loop((n // KEY_UNROLL) * KEY_UNROLL, n, lambda j, c: trip(j, 1, c), carry)
    return jnp.concatenate([accs[g] / l[:, g * tq:(g + 1) * tq] for g in range(ng)], axis=0)


def _store_heads(o_ref, o_t):
    for r in range(o_t.shape[0] // LANES):
        for c in range(o_t.shape[1] // LANES):
            o_ref[c * LANES:(c + 1) * LANES, r * LANES:(r + 1) * LANES] = jnp.transpose(
                o_t[r * LANES:(r + 1) * LANES, c * LANES:(c + 1) * LANES])


def _fox_prompt_kernel(q_ref, k_ref, v_ref, crow_ref, o_ref, kb_ref, vt_ref, cb_ref, *, tq):
    qi = pl.program_id(2)

    @pl.when(qi == 0)
    def _():
        _stage_kv(k_ref, v_ref, kb_ref, vt_ref)
        for g in range(cb_ref.shape[0]):
            for j in range(k_ref.shape[0] // LANES):
                sl = slice(j * LANES, (j + 1) * LANES)
                cb_ref[g, sl, :] = jnp.transpose(
                    jnp.broadcast_to(crow_ref[0, g:g + 1, sl] * LOG2_E, (LANES, LANES)))

    q = q_ref[...] * (DH_C ** -0.5 * LOG2_E)

    def adjust(s, j, start, diagonal):
        return s - jnp.concatenate([cb_ref[g, pl.ds(start, KEY_TILE), :] for g in range(cb_ref.shape[0])
                                    for _ in range(tq // LANES)], axis=1)

    qbs = [jnp.where(mask, q, 0.0).astype(BF16) for mask in _head_masks(q.shape)]
    _store_heads(o_ref, _flash_group(qbs, qi, kb_ref, vt_ref, tq, adjust))


def _fox_prompt(q, k, v, crow, nseq, seq):
    tq = FOX_TILE
    nq = seq // tq
    ng = FOX_GROUP
    width = ng * HEAD_DIM
    return pl.pallas_call(
        functools.partial(_fox_prompt_kernel, tq=tq),
        out_shape=jax.ShapeDtypeStruct(q.shape, F32),
        grid=(nseq, H_C // ng, nq),
        in_specs=[pl.BlockSpec((tq, width), lambda n, p, i: (n * nq + i, p)),
                  pl.BlockSpec((seq, width), lambda n, p, i: (n, p)),
                  pl.BlockSpec((1, width, seq), lambda n, p, i: (n, p, 0)),
                  pl.BlockSpec((1, ng, seq), lambda n, p, i: (p, 0, n))],
        out_specs=pl.BlockSpec((tq, width), lambda n, p, i: (n * nq + i, p)),
        scratch_shapes=[pltpu.VMEM((seq, width), BF16), pltpu.VMEM((seq // KEY_TILE, width, KEY_TILE), BF16),
                        pltpu.VMEM((ng, seq, LANES), F32)],
        compiler_params=_params("arbitrary", "arbitrary", "arbitrary"),
        name="fox_prompt",
    )(q, k, v, crow)


def _rank_rows(g, n_valid):
    nb = g.shape[0]
    r = lax.broadcasted_iota(jnp.int32, g.shape, 0)
    g = jnp.where(r < n_valid, g, NEG_INF)
    rank = jnp.zeros(g.shape, jnp.int32)
    for i in range(nb):
        gi = g[i:i + 1, :]
        rank = rank + ((gi > g) | ((gi == g) & (i < r))).astype(jnp.int32)
    return rank, r


def _moba_prompt_kernel(q_ref, k_ref, v_ref, o_ref, kb_ref, vt_ref, kmean_ref, sel_ref, *, tq, nb):
    qi = pl.program_id(2)

    @pl.when(qi == 0)
    def _():
        _stage_kv(k_ref, v_ref, kb_ref, vt_ref)
        kmean_ref[...] = jnp.zeros_like(kmean_ref)
        for j in range(nb):
            kmean_ref[j:j + 1, :] = jnp.mean(k_ref[j * MOBA_BLOCK:(j + 1) * MOBA_BLOCK, :], axis=0, keepdims=True)

    q = q_ref[...]
    km_hi, km_lo = _split_bf16(kmean_ref[...])
    qbs = []
    for hh, mask in enumerate(_head_masks(q.shape)):
        qm = jnp.where(mask, q, 0.0)
        q_hi, q_lo = _split_bf16(qm)
        gate = _dot_nt(km_hi, q_hi) + _dot_nt(km_lo, q_hi) + _dot_nt(km_hi, q_lo)
        rank, r = _rank_rows(gate, qi)
        sel_ref[hh] = ((rank < MOBA_TOPK) & (r < qi)).astype(F32)
        qbs.append((qm * (DH_A ** -0.5 * LOG2_E)).astype(BF16))

    def adjust(s, j, start, diagonal):
        if diagonal:
            return s
        blk = (j * KEY_TILE) // MOBA_BLOCK
        chosen = jnp.concatenate([sel_ref[g, pl.ds(blk, 1), :] for g in range(sel_ref.shape[0])], axis=1)
        return jnp.where(chosen > 0.5, s, NEG_INF)

    _store_heads(o_ref, _flash_group(qbs, qi, kb_ref, vt_ref, tq, adjust))


def _moba_prompt(q, k, v, nseq, seq):
    tq = ATT_TILE
    assert tq == MOBA_BLOCK and seq % MOBA_BLOCK == 0 and seq // MOBA_BLOCK >= MOBA_TOPK
    nq = seq // tq
    nb = seq // MOBA_BLOCK
    nbp = -(-nb // SUBLANES) * SUBLANES
    ng = MOBA_GROUP
    width = ng * HEAD_DIM
    return pl.pallas_call(
        functools.partial(_moba_prompt_kernel, tq=tq, nb=nb),
        out_shape=jax.ShapeDtypeStruct(q.shape, F32),
        grid=(nseq, H_A // ng, nq),
        in_specs=[pl.BlockSpec((tq, width), lambda n, p, i: (n * nq + i, p)),
                  pl.BlockSpec((seq, width), lambda n, p, i: (n, p)),
                  pl.BlockSpec((1, width, seq), lambda n, p, i: (n, p, 0))],
        out_specs=pl.BlockSpec((tq, width), lambda n, p, i: (n * nq + i, p)),
        scratch_shapes=[pltpu.VMEM((seq, width), BF16), pltpu.VMEM((seq // KEY_TILE, width, KEY_TILE), BF16),
                        pltpu.VMEM((nbp, width), F32), pltpu.VMEM((ng, nbp, tq), F32)],
        compiler_params=_params("arbitrary", "arbitrary", "arbitrary"),
        name="moba_prompt",
    )(q, k, v)


def _column_replicated(row):
    return jnp.concatenate(
        [jnp.transpose(jnp.broadcast_to(row[:, c * LANES:(c + 1) * LANES], (LANES, LANES)))
         for c in range(row.shape[1] // LANES)], axis=0)


def _head_sums(x, dh):
    return jnp.concatenate([jnp.sum(x[h * dh:(h + 1) * dh, :], axis=0, keepdims=True)
                            for h in range(x.shape[0] // dh)], axis=0)


def _moba_sample_select_kernel(pt_ref, q_ref, *refs, group, nblk):
    k_refs = refs[:group]
    idx_ref = refs[group]
    qcol_ref, gate_ref = refs[group + 1:]
    g = pl.program_id(1)

    @pl.when(g == 0)
    def _():
        qcol_ref[...] = _column_replicated(q_ref[0])
        gate_ref[...] = jnp.zeros_like(gate_ref)

    lane = lax.broadcasted_iota(jnp.int32, gate_ref.shape, 1)
    pages_per_block = MOBA_BLOCK // PAGE_SIZE
    qcol = qcol_ref[...]
    upd = jnp.zeros(gate_ref.shape, F32)
    for i in range(group):
        qk = _head_sums(k_refs[i][0] * qcol, DH_A)
        blk = (g * group + i) // pages_per_block
        val = jnp.sum(qk, axis=1, keepdims=True) * (1.0 / (PAGE_SIZE * pages_per_block))
        upd = upd + jnp.where(lane == blk, val, 0.0)
    gate_ref[...] += upd

    @pl.when(g == pl.num_programs(1) - 1)
    def _():
        gate = jnp.where(lane < nblk, gate_ref[...], NEG_INF)
        rank = jnp.zeros(gate.shape, jnp.int32)
        for i in range(nblk):
            gi = gate[:, i:i + 1]
            rank = rank + ((gi > gate) | ((gi == gate) & (i < lane))).astype(jnp.int32)
        out = jnp.zeros(gate.shape, F32)
        for t in range(MOBA_TOPK):
            it = jnp.sum(jnp.where((rank == t) & (lane < nblk), lane.astype(F32), 0.0), axis=1, keepdims=True)
            out = jnp.where(lane == t, it, out)
        idx_ref[0] = out.astype(jnp.int32)


def _moba_sample_select(q, pool_kt, page_table, nblk):
    nb = q.shape[0]
    group = PAGES_PER_STEP
    npg = nblk * (MOBA_BLOCK // PAGE_SIZE)
    assert npg % group == 0 and MOBA_TOPK <= nblk <= LANES
    in_specs = [pl.BlockSpec((1, 1, W_A), lambda n, g, pt: (n, 0, 0))]
    for i in range(group):
        in_specs.append(pl.BlockSpec((1, W_A, PAGE_SIZE), lambda n, g, pt, i=i: (pt[n, g * group + i], 0, 0)))
    return pl.pallas_call(
        functools.partial(_moba_sample_select_kernel, group=group, nblk=nblk),
        out_shape=jax.ShapeDtypeStruct((nb, H_A, LANES), jnp.int32),
        grid_spec=pltpu.PrefetchScalarGridSpec(
            num_scalar_prefetch=1,
            grid=(nb, npg // group),
            in_specs=in_specs,
            out_specs=pl.BlockSpec((1, H_A, LANES), lambda n, g, pt: (n, 0, 0)),
            scratch_shapes=[pltpu.VMEM((W_A, LANES), F32), pltpu.VMEM((H_A, LANES), F32)]),
        compiler_params=_params("arbitrary", "arbitrary"),
        name="moba_sample_select",
    )(page_table, q.reshape(nb, 1, W_A), *([pool_kt] * group))


def _moba_sample_attend_kernel(pt_ref, idx_ref, q_ref, kn_ref, vn_ref, *refs, n_sel, n_own, qblk):
    n_pages = n_sel + n_own
    k_refs = refs[:n_pages]
    v_refs = refs[n_pages:2 * n_pages]
    o_ref = refs[2 * n_pages]
    n = pl.program_id(0)
    h = pl.program_id(1)
    q = q_ref[0, pl.ds(h, 1), :] * (DH_A ** -0.5)
    qcol = _column_replicated(q)[0:DH_A]
    pages_per_block = MOBA_BLOCK // PAGE_SIZE
    logits = []
    for i in range(n_pages):
        s = jnp.sum(k_refs[i][0] * qcol, axis=0, keepdims=True)
        if i < n_sel:
            valid = idx_ref[n, h, i // pages_per_block] < qblk
            s = jnp.where(valid, s, NEG_INF)
        logits.append(s)
    s_new = jnp.sum(q * kn_ref[0, pl.ds(h, 1), :], axis=1, keepdims=True)
    m = s_new
    for s in logits:
        m = jnp.maximum(m, jnp.max(s, axis=1, keepdims=True))
    p_new = jnp.exp(s_new - m)
    l = p_new
    acc_t = jnp.zeros((DH_A, PAGE_SIZE), F32)
    for i, s in enumerate(logits):
        p = jnp.exp(s - m)
        l = l + jnp.sum(p, axis=1, keepdims=True)
        acc_t = acc_t + v_refs[i][0] * p
    acc = jnp.sum(jnp.transpose(jnp.concatenate([acc_t, jnp.zeros((LANES - DH_A, PAGE_SIZE), F32)], axis=0)),
                  axis=0, keepdims=True)
    res = (acc + p_new * vn_ref[0, pl.ds(h, 1), :]) / l
    row = lax.broadcasted_iota(jnp.int32, (H_A, LANES), 0)

    @pl.when(h == 0)
    def _():
        o_ref[0] = jnp.where(row == 0, res, 0.0)

    @pl.when(h > 0)
    def _():
        o_ref[0] = jnp.where(row == h, res, o_ref[0])


def _moba_sample_attend(q, k_new, v_new, pool_kt, pool_vt, page_table, idx):
    nb = q.shape[0]
    n_pages_total = page_table.shape[1]
    pages_per_block = MOBA_BLOCK // PAGE_SIZE
    past = n_pages_total * PAGE_SIZE
    qblk = past // MOBA_BLOCK
    n_sel = MOBA_TOPK * pages_per_block
    n_own = 1 if (past - PAGE_SIZE) // MOBA_BLOCK == qblk else 0

    def sel_map(i):
        def index_map(n, h, pt, ix):
            lpage = jnp.minimum(ix[n, h, i // pages_per_block] * pages_per_block + i % pages_per_block,
                                n_pages_total - 1)
            return (pt[n, lpage], h, 0)
        return index_map

    def own_map(n, h, pt, ix):
        return (pt[n, n_pages_total - 1], h, 0)

    page_specs = [pl.BlockSpec((1, DH_A, PAGE_SIZE), sel_map(i)) for i in range(n_sel)]
    page_specs += [pl.BlockSpec((1, DH_A, PAGE_SIZE), own_map)] * n_own
    tok_spec = pl.BlockSpec((1, H_A, LANES), lambda n, h, pt, ix: (n, 0, 0))
    tok = lambda a: _pad_lanes(a.reshape(nb, H_A, DH_A))
    n_pg = n_sel + n_own
    att = pl.pallas_call(
        functools.partial(_moba_sample_attend_kernel, n_sel=n_sel, n_own=n_own, qblk=qblk),
        out_shape=jax.ShapeDtypeStruct((nb, H_A, LANES), F32),
        grid_spec=pltpu.PrefetchScalarGridSpec(
            num_scalar_prefetch=2,
            grid=(nb, H_A),
            in_specs=[tok_spec, tok_spec, tok_spec] + page_specs + page_specs,
            out_specs=tok_spec),
        compiler_params=_params("arbitrary", "arbitrary"),
        name="moba_sample_attend",
    )(page_table, idx, tok(q), tok(k_new), tok(v_new), *([pool_kt] * n_pg), *([pool_vt] * n_pg))
    return att[:, :, :DH_A].reshape(nb, W_A)


def _mlstm_sample_kernel(qk_ref, st_ref, v_ref, ob_ref, gate_ref, c_ref, n_ref, m_ref, cw_ref, cb_ref, nw_ref,
                         g_ref, cn_ref, nn_ref, mn_ref):
    cw = cw_ref[...]
    st = st_ref[0]
    y = cb_ref[...] + cw[MLSTM_CONV - 1:MLSTM_CONV] * qk_ref[0]
    for t in range(MLSTM_CONV - 1):
        y = y + cw[t:t + 1] * st[t:t + 1]
    qk = y * _sigmoid(y)
    gates = gate_ref[0]
    m_in = m_ref[0]
    m_out = jnp.zeros((1, LANES), F32)
    lane = lax.broadcasted_iota(jnp.int32, (1, LANES), 1)
    for h in range(H_B):
        q = qk[:, h * DH_B:(h + 1) * DH_B]
        k = qk[:, W_B + h * DH_B:W_B + (h + 1) * DH_B] * (DH_B ** -0.5)
        v = v_ref[0][:, h * DH_B:(h + 1) * DH_B]
        ig = gates[:, h:h + 1]
        lf = gates[:, H_B + h:H_B + h + 1]
        m_old = m_in[:, h:h + 1]
        m_row = jnp.maximum(ig, lf + m_old)
        w_inter = jnp.exp(lf + m_old - m_row)
        w_s = jnp.exp(ig - m_row)
        s = jnp.sum(q * k, axis=1, keepdims=True) * w_s
        cmat = c_ref[0, h]
        nvec = n_ref[0, h:h + 1, :]
        qc = _dot(jnp.broadcast_to(q, (SUBLANES, DH_B)).astype(BF16), cmat.astype(BF16))[0:1]
        num = s * v + w_inter * qc
        den = s + w_inter * jnp.sum(q * nvec, axis=1, keepdims=True)
        hh = num / jnp.maximum(jnp.abs(den), jnp.exp(-m_row))
        k_col = jnp.transpose(jnp.broadcast_to(k, (DH_B, DH_B)))
        cn_ref[0, h] = w_inter * cmat + w_s * (k_col * v)
        nn_ref[0, h:h + 1, :] = w_inter * nvec + w_s * k
        m_out = jnp.where(lane == h, m_row, m_out)
        hn = _rms(hh, nw_ref[:, h * DH_B:(h + 1) * DH_B])
        g_ref[0, :, h * DH_B:(h + 1) * DH_B] = _sigmoid(ob_ref[0][:, h * DH_B:(h + 1) * DH_B]) * hn
    mn_ref[0] = m_out


def _mlstm_sample(qk_pre, conv_state, v_pre, ob, gates, c, nvec, m, conv_w, conv_b, norm_w):
    nb = qk_pre.shape[0]
    tok = lambda a: a.reshape(nb, 1, a.shape[-1])
    tspec = lambda w: pl.BlockSpec((1, 1, w), lambda n: (n, 0, 0))
    full = lambda shape: pl.BlockSpec(shape, lambda n: (0,) * len(shape))
    g, cn, nn, mn = pl.pallas_call(
        _mlstm_sample_kernel,
        out_shape=[jax.ShapeDtypeStruct((nb, 1, W_B), F32),
                   jax.ShapeDtypeStruct((nb, H_B, DH_B, DH_B), F32),
                   jax.ShapeDtypeStruct((nb, H_B, DH_B), F32),
                   jax.ShapeDtypeStruct((nb, 1, LANES), F32)],
        grid=(nb,),
        in_specs=[tspec(2 * W_B),
                  pl.BlockSpec((1, MLSTM_CONV - 1, 2 * W_B), lambda n: (n, 0, 0)),
                  tspec(W_B), tspec(W_B), tspec(LANES),
                  pl.BlockSpec((1, H_B, DH_B, DH_B), lambda n: (n, 0, 0, 0)),
                  pl.BlockSpec((1, H_B, DH_B), lambda n: (n, 0, 0)),
                  tspec(LANES),
                  full((MLSTM_CONV, 2 * W_B)), full((1, 2 * W_B)), full((1, W_B))],
        out_specs=[tspec(W_B),
                   pl.BlockSpec((1, H_B, DH_B, DH_B), lambda n: (n, 0, 0, 0)),
                   pl.BlockSpec((1, H_B, DH_B), lambda n: (n, 0, 0)),
                   tspec(LANES)],
        compiler_params=_params("arbitrary"),
        name="mlstm_sample",
    )(tok(qk_pre), conv_state, tok(v_pre), tok(ob), tok(gates), c, nvec, tok(_pad_lanes(m)),
      conv_w, conv_b.reshape(1, -1), norm_w.reshape(1, -1))
    return g.reshape(nb, W_B), cn, nn, mn[:, 0, :H_B]


def _fox_sample_kernel(pt_ref, q_ref, kn_ref, vn_ref, lfn_ref, *refs, group):
    k_refs = refs[:group]
    v_refs = refs[group:2 * group]
    lf_refs = refs[2 * group:3 * group]
    o_ref = refs[3 * group]
    qcol_ref, m_ref, l_ref, run_ref, acc_ref = refs[3 * group + 1:]
    g = pl.program_id(1)

    @pl.when(g == 0)
    def _():
        qcol_ref[...] = _column_replicated(q_ref[0] * (DH_C ** -0.5))
        m_ref[...] = jnp.full(m_ref.shape, NEG_INF, F32)
        l_ref[...] = jnp.zeros_like(l_ref)
        run_ref[...] = jnp.zeros_like(run_ref)
        acc_ref[...] = jnp.zeros_like(acc_ref)

    qcol = qcol_ref[...]
    run = run_ref[...]
    logits = []
    for i in range(group):
        qk = _head_sums(k_refs[i][0] * qcol, DH_C)
        cum = run + _scan_lanes(lf_refs[i][0], jnp.add, 0.0)
        run = cum[:, PAGE_SIZE - 1:PAGE_SIZE]
        logits.append(qk - cum)
    s = jnp.concatenate(logits, axis=1)
    m = m_ref[...]
    m_new = jnp.maximum(m, jnp.max(s, axis=1, keepdims=True))
    alpha = jnp.exp(m - m_new)
    p = jnp.exp(s - m_new)
    l = alpha * l_ref[...] + jnp.sum(p, axis=1, keepdims=True)
    m_ref[...], l_ref[...], run_ref[...] = m_new, l, run
    for h in range(H_C):
        rows = slice(h * DH_C, (h + 1) * DH_C)
        a = acc_ref[rows, :] * alpha[h:h + 1, :]
        for i in range(group):
            a = a + v_refs[i][0, rows, :] * p[h:h + 1, i * PAGE_SIZE:(i + 1) * PAGE_SIZE]
        acc_ref[rows, :] = a

    @pl.when(g == pl.num_programs(1) - 1)
    def _():
        head = lax.broadcasted_iota(jnp.int32, (H_C, W_C), 0)
        mine = (lax.broadcasted_iota(jnp.int32, (H_C, W_C), 1) // DH_C) == head
        q_bd = jnp.where(mine, jnp.broadcast_to(q_ref[0], (H_C, W_C)), 0.0) * (DH_C ** -0.5)
        hrow = lax.broadcasted_iota(jnp.int32, (H_C, LANES), 0)
        hlane = lax.broadcasted_iota(jnp.int32, (H_C, LANES), 1)
        lf_new = jnp.sum(jnp.where(hrow == hlane, jnp.broadcast_to(lfn_ref[0], (H_C, LANES)), 0.0),
                         axis=1, keepdims=True)
        s_new = jnp.sum(q_bd * kn_ref[0], axis=1, keepdims=True) - (run + lf_new)
        m_f = jnp.maximum(m_new, s_new)
        alpha_f = jnp.exp(m_new - m_f)
        p_new = jnp.exp(s_new - m_f)
        l_f = alpha_f * l + p_new
        past = jnp.concatenate([jnp.sum(jnp.transpose(acc_ref[c * LANES:(c + 1) * LANES, :]), axis=0, keepdims=True)
                                for c in range(W_C // LANES)], axis=1)
        per_lane = lambda col: jnp.sum(jnp.where(mine, col, 0.0), axis=0, keepdims=True)
        o_ref[0] = (past * per_lane(alpha_f) + per_lane(p_new) * vn_ref[0]) / per_lane(l_f)


def _fox_sample(q, k_new, v_new, lf_new, pool_k, pool_v, pool_lf_t, page_table):
    nb = q.shape[0]
    n_pages = page_table.shape[1]
    group = PAGES_PER_STEP
    assert n_pages % group == 0
    tok_spec = pl.BlockSpec((1, 1, W_C), lambda n, g, pt: (n, 0, 0))
    tok = lambda a: a.reshape(nb, 1, a.shape[-1])
    page = lambda i, shape: pl.BlockSpec(shape, lambda n, g, pt: (pt[n, g * group + i], 0, 0))
    in_specs = [tok_spec, tok_spec, tok_spec, pl.BlockSpec((1, 1, LANES), lambda n, g, pt: (n, 0, 0))]
    in_specs += [page(i, (1, W_C, PAGE_SIZE)) for i in range(group)]
    in_specs += [page(i, (1, W_C, PAGE_SIZE)) for i in range(group)]
    in_specs += [page(i, (1, H_C, PAGE_SIZE)) for i in range(group)]
    return pl.pallas_call(
        functools.partial(_fox_sample_kernel, group=group),
        out_shape=jax.ShapeDtypeStruct((nb, 1, W_C), F32),
        grid_spec=pltpu.PrefetchScalarGridSpec(
            num_scalar_prefetch=1,
            grid=(nb, n_pages // group),
            in_specs=in_specs,
            out_specs=tok_spec,
            scratch_shapes=[pltpu.VMEM((W_C, LANES), F32), pltpu.VMEM((H_C, 1), F32), pltpu.VMEM((H_C, 1), F32),
                            pltpu.VMEM((H_C, 1), F32), pltpu.VMEM((W_C, PAGE_SIZE), F32)]),
        compiler_params=_params("arbitrary", "arbitrary"),
        name="fox_sample",
    )(page_table, tok(q), tok(k_new), tok(v_new), tok(lf_new),
      *([pool_k] * group), *([pool_v] * group), *([pool_lf_t] * group)).reshape(nb, W_C)


EVEN_SEGS = ((0, W_A), (W_A, W_A), (2 * W_A, W_A), (3 * W_A, 2 * W_B), (3 * W_A + 2 * W_B, W_B),
             (3 * W_A + 3 * W_B, W_B), (3 * W_A + 4 * W_B, LANES))
ODD_SEGS = ((0, W_C), (W_C, W_C), (2 * W_C, W_C), (3 * W_C, LANES))


def _even_weights(w_in, b_i, b_f):
    main = 3 * W_A + 4 * W_B
    w_pad = jnp.concatenate([w_in[:, :main], _pad_lanes(w_in[:, main:])], axis=1)
    w_bf = w_pad.astype(BF16)
    wq = w_in[:, :W_A]
    w_lo = (wq - wq.astype(BF16).astype(F32)).astype(BF16)
    gate_bias = _pad_lanes(jnp.concatenate([b_i, b_f]).reshape(1, -1))
    return w_bf, w_lo, gate_bias


def _odd_weights(w_in, b_f):
    w_pad = jnp.concatenate([w_in[:, :3 * W_C], _pad_lanes(w_in[:, 3 * W_C:])], axis=1)
    return w_pad.astype(BF16), _pad_lanes(b_f.reshape(1, -1))


def _feature_major(pool):
    n_pool, page, heads, dh = pool.shape
    return jnp.transpose(pool, (0, 2, 3, 1)).reshape(n_pool, heads * dh, page)


def _time_minor_to_cache(a_t, heads):
    n, width, seq = a_t.shape
    return jnp.transpose(a_t.reshape(n, heads, width // heads, seq), (0, 3, 1, 2))


def _rows_to_sublanes(a):
    return jnp.pad(a.T, ((0, SUBLANES - a.shape[1]), (0, 0)))


def kernel(x_prompt, x_sample, cache_moba_k, cache_moba_v, state_mlstm_c, state_mlstm_n, state_mlstm_m,
           state_mlstm_conv, cache_fox_k, cache_fox_v, cache_fox_logf, state_ffn_conv, page_table, norm_w,
           even_w_in, mlstm_conv_w, mlstm_conv_b, mlstm_b_i, mlstm_b_f, mlstm_norm_w, even_w_out, fox_w_in,
           fox_b_f, fox_w_out, ffn_w_in, ffn_conv_w, ffn_conv_b, ffn_w_out):
    bp, sp, d = x_prompt.shape
    bs, ss, _ = x_sample.shape
    assert ss == 1 and d == D_MODEL
    depth = norm_w.shape[0]
    n_pages = page_table.shape[1]
    tp = bp * sp
    yp = x_prompt.reshape(tp, d)
    ys = x_sample.reshape(bs, d)
    outs = {name: [] for name in (
        "mk_p", "mv_p", "mk_s", "mv_s", "mc_p", "mn_p", "mm_p", "mconv_p", "mc_s", "mn_s", "mm_s", "mconv_s",
        "fk_p", "fv_p", "fl_p", "fk_s", "fv_s", "fl_s", "ff_p", "ff_s")}
    for layer in range(depth):
        nw = norm_w[layer]
        if layer % 2 == 0:
            e = layer // 2
            w_bf, w_lo, gate_bias = _even_weights(even_w_in[e], mlstm_b_i[e], mlstm_b_f[e])
            w_out_bf = even_w_out[e].astype(BF16)
            proj = lambda x, tm, **kw: _rms_proj(x, nw[0], w_bf, w_lo, gate_bias, EVEN_SEGS, 1, (H_B, 2 * H_B), tm,
                                                 **kw)
            qa, ka, ka_t, va_t, qkb, vb, ob, gates = proj(yp, ROW_TILE, cols={1: True, 2: False}, seq_rows=sp)
            att = _moba_prompt(qa, ka, va_t, bp, sp)
            b_t, m_t, mrow_t = _mlstm_gate_scan(_rows_to_sublanes(gates[:, :H_B]),
                                                _rows_to_sublanes(gates[:, H_B:2 * H_B]), sp)
            cols = _pad_lanes(jnp.concatenate([b_t[:H_B].T, m_t[:H_B].T, mrow_t[:H_B].T], axis=1))
            gated, cc, nn, mfin = _mlstm_prompt(qkb, vb, ob, b_t, cols, mlstm_conv_w[e], mlstm_conv_b[e],
                                                mlstm_norm_w[e], bp, sp)
            yp = _proj_res([att, gated], [w_out_bf[:W_A], w_out_bf[W_A:]], yp, nw[1], FFN_ROWS)
            outs["mk_p"].append(_time_minor_to_cache(ka_t, H_A))
            outs["mv_p"].append(_time_minor_to_cache(va_t, H_A))
            outs["mc_p"].append(cc)
            outs["mn_p"].append(nn)
            outs["mm_p"].append(mfin[:, 0, 2 * H_B:3 * H_B])
            outs["mconv_p"].append(qkb.reshape(bp, sp, 2 * W_B)[:, sp - (MLSTM_CONV - 1):])
            qa, ka, va, qkb, vb, ob, gates = proj(ys, bs)
            pool_kt = _feature_major(cache_moba_k[e])
            pool_vt = _feature_major(cache_moba_v[e])
            nblk = (n_pages * PAGE_SIZE) // MOBA_BLOCK
            idx = _moba_sample_select(qa, pool_kt, page_table, nblk)[:, :, :MOBA_TOPK]
            att = _moba_sample_attend(qa, ka, va, pool_kt, pool_vt, page_table, idx)
            gated, cc, nn, mm = _mlstm_sample(qkb, state_mlstm_conv[e], vb, ob, gates, state_mlstm_c[e],
                                              state_mlstm_n[e], state_mlstm_m[e], mlstm_conv_w[e],
                                              mlstm_conv_b[e], mlstm_norm_w[e])
            ys = _proj_res([att, gated], [w_out_bf[:W_A], w_out_bf[W_A:]], ys, nw[1], bs)
            outs["mk_s"].append(ka.reshape(bs, ss, H_A, DH_A))
            outs["mv_s"].append(va.reshape(bs, ss, H_A, DH_A))
            outs["mc_s"].append(cc)
            outs["mn_s"].append(nn)
            outs["mm_s"].append(mm)
            outs["mconv_s"].append(jnp.concatenate([state_mlstm_conv[e][:, 1:], qkb[:, None, :]], axis=1))
        else:
            o = layer // 2
            w_bf, gate_bias = _odd_weights(fox_w_in[o], fox_b_f[o])
            w_out_bf = fox_w_out[o].astype(BF16)
            proj = lambda x, tm, **kw: _rms_proj(x, nw[0], w_bf, None, gate_bias, ODD_SEGS, 0, (0, H_C), tm, **kw)
            q, k, k_t, v_t, lf = proj(yp, ROW_TILE, cols={1: True, 2: False}, seq_rows=sp)
            lf_t = lf[:, :H_C].T
            crow = _cumsum_time(lf_t, sp).reshape(H_C // FOX_GROUP, FOX_GROUP, tp)
            att = _fox_prompt(q, k, v_t, crow, bp, sp)
            yp = _proj_res([att], [w_out_bf], yp, nw[1], FFN_ROWS)
            outs["fk_p"].append(_time_minor_to_cache(k_t, H_C))
            outs["fv_p"].append(_time_minor_to_cache(v_t, H_C))
            outs["fl_p"].append(lf[:, :H_C].reshape(bp, sp, H_C))
            q, k, v, lf = proj(ys, bs)
            pool_lf_t = jnp.transpose(cache_fox_logf[o], (0, 2, 1))
            att = _fox_sample(q, k, v, lf, _feature_major(cache_fox_k[o]), _feature_major(cache_fox_v[o]),
                              pool_lf_t, page_table)
            ys = _proj_res([att], [w_out_bf], ys, nw[1], bs)
            outs["fk_s"].append(k.reshape(bs, ss, H_C, DH_C))
            outs["fv_s"].append(v.reshape(bs, ss, H_C, DH_C))
            outs["fl_s"].append(lf[:, :H_C].reshape(bs, ss, H_C))
        w_in_bf = ffn_w_in[layer].astype(BF16)
        w_o_bf = ffn_w_out[layer].astype(BF16)
        yp, buf_p = _ffn(yp, nw[2], w_in_bf, ffn_conv_w[layer], ffn_conv_b[layer], w_o_bf, nw[3], FFN_ROWS, sp)
        ys, u_s = _ffn(ys, nw[2], w_in_bf, ffn_conv_w[layer], ffn_conv_b[layer], w_o_bf, nw[3], bs, 1,
                       state=state_ffn_conv[layer])
        outs["ff_p"].append(buf_p)
        outs["ff_s"].append(jnp.concatenate([state_ffn_conv[layer][:, 1:], u_s[:, None, :]], axis=1))
    st = {name: jnp.stack(vals) for name, vals in outs.items()}
    return (yp.reshape(bp, sp, d), ys.reshape(bs, ss, d), st["mk_p"], st["mv_p"], st["mk_s"], st["mv_s"],
            st["mc_p"], st["mn_p"], st["mm_p"], st["mconv_p"], st["mc_s"], st["mn_s"], st["mm_s"], st["mconv_s"],
            st["fk_p"], st["fv_p"], st["fl_p"], st["fk_s"], st["fv_s"], st["fl_s"], st["ff_p"], st["ff_s"])
```

```python
import functools

import jax
import jax.numpy as jnp
from jax import lax
from jax.experimental import pallas as pl
from jax.experimental.pallas import tpu as pltpu

F32 = jnp.float32
BF16 = jnp.bfloat16
NEG_INF = float("-inf")

D_MODEL = 1024
PAGE_SIZE = 128
H_A, DH_A = 8, 64
W_A = H_A * DH_A
MOBA_BLOCK = 256
MOBA_TOPK = 3
H_B, DH_B = 4, 128
W_B = H_B * DH_B
MLSTM_CONV = 4
H_C, DH_C = 16, 64
W_C = H_C * DH_C
D_FF = 2816
FFN_CONV = 3
RMS_EPS = 1e-6

MXU_DEPTH = 256
LANES = 128
SUBLANES = 8
VMEM_LIMIT = 56 * 1024 * 1024

ROW_TILE = 256
ATT_TILE = 256
FOX_TILE = 512
KEY_TILE = 256
KEY_UNROLL = 4
MOBA_GROUP = 8
FOX_GROUP = 4
MLSTM_TILE = 256
FFN_ROWS = 512
FFN_COLS = 1408
FFN_SLICE = 256
PAGES_PER_STEP = 16
SELECT_PAGES = 16


def _params(*sem):
    return pltpu.CompilerParams(dimension_semantics=sem, vmem_limit_bytes=VMEM_LIMIT)


def _rms(x, w):
    return x * lax.rsqrt(jnp.mean(x * x, axis=-1, keepdims=True) + RMS_EPS) * w


def _sigmoid(x):
    return 1.0 / (1.0 + jnp.exp(-x))


def _log_sigmoid(x):
    return -(jnp.maximum(-x, 0.0) + jnp.log1p(jnp.exp(-jnp.abs(x))))


def _gelu_tanh(x):
    c = 0.7978845608028654
    return 0.5 * x * (1.0 + jnp.tanh(c * (x + 0.044715 * (x * x * x))))


def _dot(a, b):
    return jnp.dot(a, b, preferred_element_type=F32)


def _dot_nt(a, b):
    return lax.dot_general(a, b, (((1,), (1,)), ((), ())), preferred_element_type=F32)


def _split_bf16(x):
    hi = x.astype(BF16)
    lo = (x - hi.astype(F32)).astype(BF16)
    return hi, lo


def _pad_lanes(a, width=LANES):
    return jnp.pad(a, ((0, 0),) * (a.ndim - 1) + ((0, width - a.shape[-1]),))


def _rms_proj_kernel(*refs, segs, n_hi, gate_lf, cols):
    x_ref, nw_ref, w_ref = refs[:3]
    pos = 3
    wlo_ref = None
    if n_hi:
        wlo_ref = refs[pos]
        pos += 1
    gb_ref = refs[pos]
    out_refs = list(refs[pos + 1:])
    h = _rms(x_ref[...], nw_ref[...])
    hb, hl = _split_bf16(h)
    lo_start = 0
    for i, (start, width) in enumerate(segs):
        w = w_ref[:, start:start + width]
        z = _dot(hb, w)
        if i < n_hi:
            z = z + _dot(hl, w) + _dot(hb, wlo_ref[:, lo_start:lo_start + width])
            lo_start += width
        if i == len(segs) - 1:
            z = z + gb_ref[...]
            lane = lax.broadcasted_iota(jnp.int32, z.shape, 1)
            z = jnp.where((lane >= gate_lf[0]) & (lane < gate_lf[1]), _log_sigmoid(z), z)
        if cols.get(i, True):
            out_refs.pop(0)[...] = z
        if i in cols:
            zt_ref = out_refs.pop(0)
            for r in range(z.shape[0] // LANES):
                for c in range(width // LANES):
                    zt_ref[0, c * LANES:(c + 1) * LANES, r * LANES:(r + 1) * LANES] = jnp.transpose(
                        z[r * LANES:(r + 1) * LANES, c * LANES:(c + 1) * LANES])


def _rms_proj(x, nw, w_bf, w_lo, gate_bias, segs, n_hi, gate_lf, tm, cols=None, seq_rows=None):
    rows, d = x.shape
    ncols = w_bf.shape[1]
    cols = cols or {}
    in_specs = [pl.BlockSpec((tm, d), lambda i: (i, 0)),
                pl.BlockSpec((1, d), lambda i: (0, 0)),
                pl.BlockSpec((d, ncols), lambda i: (0, 0))]
    args = [x, nw.reshape(1, d), w_bf]
    if n_hi:
        in_specs.append(pl.BlockSpec(w_lo.shape, lambda i: (0, 0)))
        args.append(w_lo)
    in_specs.append(pl.BlockSpec((1, LANES), lambda i: (0, 0)))
    args.append(gate_bias)
    out_shape, out_specs = [], []
    for i, (_, wd) in enumerate(segs):
        if cols.get(i, True):
            out_shape.append(jax.ShapeDtypeStruct((rows, wd), F32))
            out_specs.append(pl.BlockSpec((tm, wd), lambda i: (i, 0)))
        if i in cols:
            tps = seq_rows // tm
            out_shape.append(jax.ShapeDtypeStruct((rows // seq_rows, wd, seq_rows), F32))
            out_specs.append(pl.BlockSpec((1, wd, tm), lambda i: (i // tps, 0, i % tps)))
    return pl.pallas_call(
        functools.partial(_rms_proj_kernel, segs=segs, n_hi=n_hi, gate_lf=gate_lf, cols=cols),
        out_shape=out_shape,
        grid=(rows // tm,),
        in_specs=in_specs,
        out_specs=out_specs,
        compiler_params=_params("arbitrary"),
        name="rms_proj",
    )(*args)


def _proj_res_kernel(*refs, n_in):
    a_refs = refs[:n_in]
    w_refs = refs[n_in:2 * n_in]
    x_ref, nw_ref, o_ref = refs[2 * n_in:]
    acc = None
    for a_ref, w_ref in zip(a_refs, w_refs):
        z = _dot(a_ref[...].astype(BF16), w_ref[...])
        acc = z if acc is None else acc + z
    o_ref[...] = x_ref[...] + _rms(acc, nw_ref[...])


def _proj_res(acts, ws, x, nw, tm):
    rows, d = x.shape
    n_in = len(acts)
    in_specs = ([pl.BlockSpec((tm, a.shape[1]), lambda i: (i, 0)) for a in acts]
                + [pl.BlockSpec(w.shape, lambda i: (0, 0)) for w in ws]
                + [pl.BlockSpec((tm, d), lambda i: (i, 0)), pl.BlockSpec((1, d), lambda i: (0, 0))])
    return pl.pallas_call(
        functools.partial(_proj_res_kernel, n_in=n_in),
        out_shape=jax.ShapeDtypeStruct((rows, d), F32),
        grid=(rows // tm,),
        in_specs=in_specs,
        out_specs=pl.BlockSpec((tm, d), lambda i: (i, 0)),
        compiler_params=_params("arbitrary"),
        name="proj_res",
    )(*acts, *ws, x, nw.reshape(1, d))


def _ffn_kernel(*refs, tm, tf, nj, tiles_per_seq, stateful):
    (x_ref, nw_in_ref, wa_ref, wb_ref, cwa_ref, cwb_ref, cba_ref, cbb_ref, wo_ref, nw_out_ref) = refs[:10]
    pos = 10
    if stateful:
        s0a_ref, s0b_ref, s1a_ref, s1b_ref = refs[pos:pos + 4]
        pos += 4
    y_ref, oa_ref, ob_ref = refs[pos:pos + 3]
    pos += 3
    hb_ref, acc_ref = refs[pos:pos + 2]
    pos += 2
    if not stateful:
        ubuf_ref, carry_ref = refs[pos:pos + 2]
    i = pl.program_id(0)
    j = pl.program_id(1)

    @pl.when(j == 0)
    def _():
        hb_ref[...] = _rms(x_ref[...], nw_in_ref[...]).astype(BF16)
        acc_ref[...] = jnp.zeros_like(acc_ref)

    hb = hb_ref[...]
    if not stateful:
        first = i % tiles_per_seq == 0

        @pl.when(first)
        def _():
            ubuf_ref[:, 0:SUBLANES, :] = jnp.zeros((2, SUBLANES, tf), F32)

        @pl.when(jnp.logical_not(first))
        def _():
            ubuf_ref[:, 0:SUBLANES, :] = carry_ref[:, j]

    gs = []
    for c0 in range(0, tf, FFN_SLICE):
        cs = slice(c0, min(c0 + FFN_SLICE, tf))
        halves = []
        for half, (w_ref, cw_ref, cb_ref) in enumerate(((wa_ref, cwa_ref, cba_ref), (wb_ref, cwb_ref, cbb_ref))):
            u = _dot(hb, w_ref[:, cs])
            cw = cw_ref[:, cs]
            if stateful:
                s0 = (s0a_ref, s0b_ref)[half][:, cs]
                s1 = (s1a_ref, s1b_ref)[half][:, cs]
                y = cw[0:1] * s0 + cw[1:2] * s1 + cw[2:3] * u + cb_ref[:, cs]
                (oa_ref, ob_ref)[half][:, cs] = u
            else:
                ubuf_ref[half, SUBLANES:, cs] = u
                y = (cw[0:1] * ubuf_ref[half, SUBLANES - 2:SUBLANES - 2 + tm, cs]
                     + cw[1:2] * ubuf_ref[half, SUBLANES - 1:SUBLANES - 1 + tm, cs]
                     + cw[2:3] * u + cb_ref[:, cs])
            halves.append(y)
        gs.append((_gelu_tanh(halves[0]) * halves[1]).astype(BF16))
    if not stateful:
        carry_ref[:, j] = ubuf_ref[:, tm:tm + SUBLANES, :]
        oa_ref[0] = ubuf_ref[0, tm + SUBLANES - 2:tm + SUBLANES, :]
        ob_ref[0] = ubuf_ref[1, tm + SUBLANES - 2:tm + SUBLANES, :]
    acc_ref[...] += _dot(jnp.concatenate(gs, axis=1), wo_ref[...])

    @pl.when(j == nj - 1)
    def _():
        y_ref[...] = x_ref[...] + _rms(acc_ref[...], nw_out_ref[...])


def _ffn(x, nw_in, w_in_bf, conv_w, conv_b, w_out_bf, nw_out, tm, seq_rows, state=None):
    rows, d = x.shape
    tf = FFN_COLS
    nj = D_FF // tf
    stateful = state is not None
    cb2 = conv_b.reshape(1, 2 * D_FF)
    in_specs = [pl.BlockSpec((tm, d), lambda i, j: (i, 0)),
                pl.BlockSpec((1, d), lambda i, j: (0, 0)),
                pl.BlockSpec((d, tf), lambda i, j: (0, j)),
                pl.BlockSpec((d, tf), lambda i, j: (0, nj + j)),
                pl.BlockSpec((FFN_CONV, tf), lambda i, j: (0, j)),
                pl.BlockSpec((FFN_CONV, tf), lambda i, j: (0, nj + j)),
                pl.BlockSpec((1, tf), lambda i, j: (0, j)),
                pl.BlockSpec((1, tf), lambda i, j: (0, nj + j)),
                pl.BlockSpec((tf, d), lambda i, j: (j, 0)),
                pl.BlockSpec((1, d), lambda i, j: (0, 0))]
    args = [x, nw_in.reshape(1, d), w_in_bf, w_in_bf, conv_w, conv_w, cb2, cb2, w_out_bf, nw_out.reshape(1, d)]
    scratch = [pltpu.VMEM((tm, d), BF16), pltpu.VMEM((tm, d), F32)]
    if stateful:
        s0, s1 = state[:, 0, :], state[:, 1, :]
        in_specs += [pl.BlockSpec((tm, tf), lambda i, j: (i, j)), pl.BlockSpec((tm, tf), lambda i, j: (i, nj + j)),
                     pl.BlockSpec((tm, tf), lambda i, j: (i, j)), pl.BlockSpec((tm, tf), lambda i, j: (i, nj + j))]
        args += [s0, s0, s1, s1]
        out_shape = [jax.ShapeDtypeStruct((rows, d), F32), jax.ShapeDtypeStruct((rows, D_FF), F32),
                     jax.ShapeDtypeStruct((rows, D_FF), F32)]
        out_specs = [pl.BlockSpec((tm, d), lambda i, j: (i, 0)), pl.BlockSpec((tm, tf), lambda i, j: (i, j)),
                     pl.BlockSpec((tm, tf), lambda i, j: (i, j))]
        tiles_per_seq = 1
    else:
        tiles_per_seq = seq_rows // tm
        ntiles = rows // tm
        out_shape = [jax.ShapeDtypeStruct((rows, d), F32), jax.ShapeDtypeStruct((ntiles, FFN_CONV - 1, D_FF), F32),
                     jax.ShapeDtypeStruct((ntiles, FFN_CONV - 1, D_FF), F32)]
        out_specs = [pl.BlockSpec((tm, d), lambda i, j: (i, 0)),
                     pl.BlockSpec((1, FFN_CONV - 1, tf), lambda i, j: (i, 0, j)),
                     pl.BlockSpec((1, FFN_CONV - 1, tf), lambda i, j: (i, 0, j))]
        scratch += [pltpu.VMEM((2, tm + SUBLANES, tf), F32), pltpu.VMEM((2, nj, SUBLANES, tf), F32)]
    y, ua, ub = pl.pallas_call(
        functools.partial(_ffn_kernel, tm=tm, tf=tf, nj=nj, tiles_per_seq=tiles_per_seq, stateful=stateful),
        out_shape=out_shape,
        grid=(rows // tm, nj),
        in_specs=in_specs,
        out_specs=out_specs,
        scratch_shapes=scratch,
        compiler_params=_params("arbitrary", "arbitrary"),
        name="conv_ffn",
    )(*args)
    if not stateful:
        ua, ub = ua[tiles_per_seq - 1::tiles_per_seq], ub[tiles_per_seq - 1::tiles_per_seq]
    return y, jnp.concatenate([ua, ub], axis=-1)


def _scan_lanes(x, op, fill):
    n = x.shape[1]
    lane = lax.broadcasted_iota(jnp.int32, x.shape, 1)
    s = 1
    while s < n:
        x = op(x, jnp.where(lane >= s, pltpu.roll(x, s, axis=1), fill))
        s *= 2
    return x


def _mlstm_gate_scan_kernel(ig_ref, lf_ref, b_ref, m_ref, mrow_ref):
    a = _scan_lanes(lf_ref[...], jnp.add, 0.0)
    b = ig_ref[...] - a
    m = jnp.maximum(_scan_lanes(b, jnp.maximum, NEG_INF), 0.0)
    b_ref[...] = b
    m_ref[...] = m
    mrow_ref[...] = a + m


def _mlstm_gate_scan(ig_t, lf_t, seq):
    rows, total = ig_t.shape
    spec = pl.BlockSpec((rows, seq), lambda n: (0, n))
    return pl.pallas_call(
        _mlstm_gate_scan_kernel,
        out_shape=[jax.ShapeDtypeStruct((rows, total), F32)] * 3,
        grid=(total // seq,),
        in_specs=[spec, spec],
        out_specs=[spec, spec, spec],
        compiler_params=_params("arbitrary"),
        name="mlstm_gate_scan",
    )(ig_t, lf_t)


def _cumsum_kernel(x_ref, o_ref):
    o_ref[...] = _scan_lanes(x_ref[...], jnp.add, 0.0)


def _cumsum_time(x_t, seq):
    rows, total = x_t.shape
    spec = pl.BlockSpec((rows, seq), lambda n: (0, n))
    return pl.pallas_call(
        _cumsum_kernel,
        out_shape=jax.ShapeDtypeStruct((rows, total), F32),
        grid=(total // seq,),
        in_specs=[spec],
        out_specs=spec,
        compiler_params=_params("arbitrary"),
        name="logf_cumsum",
    )(x_t)


def _mlstm_prompt_kernel(qk_ref, v_ref, ob_ref, rows_ref, cols_ref, cw_ref, cb_ref, nw_ref,
                         g_ref, c_ref, n_ref, mfin_ref, xbuf_ref, mprev_ref, *, tl):
    c = pl.program_id(1)
    nc = pl.num_programs(1)

    @pl.when(c == 0)
    def _():
        xbuf_ref[0:SUBLANES, :] = jnp.zeros((SUBLANES, 2 * W_B), F32)
        mprev_ref[...] = jnp.zeros_like(mprev_ref)
        c_ref[...] = jnp.zeros_like(c_ref)
        n_ref[...] = jnp.zeros_like(n_ref)

    xbuf_ref[SUBLANES:, :] = qk_ref[...]
    cw = cw_ref[...]
    y = cb_ref[...]
    for t in range(MLSTM_CONV):
        off = SUBLANES - (MLSTM_CONV - 1) + t
        y = y + cw[t:t + 1] * xbuf_ref[off:off + tl, :]
    xbuf_ref[0:SUBLANES, :] = xbuf_ref[tl:tl + SUBLANES, :]
    qk = y * _sigmoid(y)

    cols = cols_ref[...]
    mprev = mprev_ref[...]
    t_idx = lax.broadcasted_iota(jnp.int32, (tl, tl), 0)
    s_idx = lax.broadcasted_iota(jnp.int32, (tl, tl), 1)
    causal = s_idx <= t_idx
    for h in range(H_B):
        q = qk[:, h * DH_B:(h + 1) * DH_B]
        k = qk[:, W_B + h * DH_B:W_B + (h + 1) * DH_B] * (DH_B ** -0.5)
        v = v_ref[:, h * DH_B:(h + 1) * DH_B]
        qb, kb, vb = q.astype(BF16), k.astype(BF16), v.astype(BF16)
        b_row = rows_ref[h:h + 1, :]
        b_col = cols[:, h:h + 1]
        m_col = cols[:, H_B + h:H_B + h + 1]
        mrow_col = cols[:, 2 * H_B + h:2 * H_B + h + 1]
        m_last = cols[tl - 1:tl, H_B + h:H_B + h + 1]
        m_prev = mprev[:, H_B + h:H_B + h + 1]
        dmat = jnp.exp(jnp.where(causal, b_row - m_col, NEG_INF))
        s = _dot_nt(qb, kb) * dmat
        w_inter = jnp.exp(m_prev - m_col)
        cmat = c_ref[0, h]
        nvec = n_ref[0, h:h + 1, :]
        num = _dot(s.astype(BF16), vb) + w_inter * _dot(qb, cmat.astype(BF16))
        den = jnp.sum(s, axis=1, keepdims=True) + w_inter * jnp.sum(q * nvec, axis=1, keepdims=True)
        hh = num / jnp.maximum(jnp.abs(den), jnp.exp(-mrow_col))
        w_s = jnp.exp(b_col - m_last)
        w_c = jnp.exp(m_prev - m_last)
        kw = k * w_s
        c_ref[0, h] = w_c * cmat + lax.dot_general(kw.astype(BF16), vb, (((0,), (0,)), ((), ())),
                                                    preferred_element_type=F32)
        n_ref[0, h:h + 1, :] = w_c * nvec + jnp.sum(kw, axis=0, keepdims=True)
        hn = _rms(hh, nw_ref[:, h * DH_B:(h + 1) * DH_B])
        g_ref[:, h * DH_B:(h + 1) * DH_B] = _sigmoid(ob_ref[:, h * DH_B:(h + 1) * DH_B]) * hn
    mprev_ref[...] = cols[tl - 1:tl, :]

    @pl.when(c == nc - 1)
    def _():
        mfin_ref[0] = cols[tl - 1:tl, :]


def _mlstm_prompt(qk_pre, v_pre, ob, rows_pack, cols_pack, conv_w, conv_b, norm_w, nseq, seq):
    tl = MLSTM_TILE
    nc = seq // tl
    total = nseq * seq
    rmap = lambda n, c: (n * nc + c, 0)
    return pl.pallas_call(
        functools.partial(_mlstm_prompt_kernel, tl=tl),
        out_shape=[jax.ShapeDtypeStruct((total, W_B), F32),
                   jax.ShapeDtypeStruct((nseq, H_B, DH_B, DH_B), F32),
                   jax.ShapeDtypeStruct((nseq, H_B, DH_B), F32),
                   jax.ShapeDtypeStruct((nseq, 1, LANES), F32)],
        grid=(nseq, nc),
        in_specs=[pl.BlockSpec((tl, 2 * W_B), rmap),
                  pl.BlockSpec((tl, W_B), rmap),
                  pl.BlockSpec((tl, W_B), rmap),
                  pl.BlockSpec((SUBLANES, tl), lambda n, c: (0, n * nc + c)),
                  pl.BlockSpec((tl, LANES), rmap),
                  pl.BlockSpec((MLSTM_CONV, 2 * W_B), lambda n, c: (0, 0)),
                  pl.BlockSpec((1, 2 * W_B), lambda n, c: (0, 0)),
                  pl.BlockSpec((1, W_B), lambda n, c: (0, 0))],
        out_specs=[pl.BlockSpec((tl, W_B), rmap),
                   pl.BlockSpec((1, H_B, DH_B, DH_B), lambda n, c: (n, 0, 0, 0)),
                   pl.BlockSpec((1, H_B, DH_B), lambda n, c: (n, 0, 0)),
                   pl.BlockSpec((1, 1, LANES), lambda n, c: (n, 0, 0))],
        scratch_shapes=[pltpu.VMEM((tl + SUBLANES, 2 * W_B), F32), pltpu.VMEM((1, LANES), F32)],
        compiler_params=_params("arbitrary", "arbitrary"),
        name="mlstm_prompt",
    )(qk_pre, v_pre, ob, rows_pack, cols_pack, conv_w, conv_b.reshape(1, -1), norm_w.reshape(1, -1))


HEAD_DIM = 64
LOG2_E = 1.4426950408889634


def _head_masks(shape):
    head = lax.broadcasted_iota(jnp.int32, shape, 1) // HEAD_DIM
    return [head == g for g in range(shape[1] // HEAD_DIM)]


def _stage_kv(k_ref, vt_in_ref, kb_ref, vt_ref):
    kb_ref[...] = k_ref[...].astype(BF16)
    for j in range(k_ref.shape[0] // KEY_TILE):
        vt_ref[j] = vt_in_ref[0, :, j * KEY_TILE:(j + 1) * KEY_TILE].astype(BF16)


def _flash_group(qbs, qi, kb_ref, vt_ref, tq, adjust):
    tk = KEY_TILE
    per = tq // tk
    ng = len(qbs)
    krow = lax.broadcasted_iota(jnp.int32, (tk, ng * tq), 0)
    qcol = lax.broadcasted_iota(jnp.int32, (tk, ng * tq), 1) % tq
    q_cat = jnp.concatenate(qbs, axis=0)

    width = qbs[0].shape[1]
    chunk = min(width, MXU_DEPTH)
    heads_per_chunk = chunk // HEAD_DIM

    def logits(j, diagonal):
        start = pl.multiple_of(j * tk, tk)
        kt = kb_ref[pl.ds(start, tk), :]
        s = jnp.concatenate(
            [_dot_nt(kt[:, c * chunk:(c + 1) * chunk],
                     q_cat[c * heads_per_chunk * tq:(c + 1) * heads_per_chunk * tq, c * chunk:(c + 1) * chunk])
             for c in range(width // chunk)], axis=1)
        s = adjust(s, j, start, diagonal is not None)
        if diagonal is not None:
            s = jnp.where(krow + diagonal * tk <= qcol, s, NEG_INF)
        return s

    def softmax(m, l, s):
        m_new = jnp.maximum(m, jnp.max(s, axis=0, keepdims=True))
        alpha = jnp.exp2(m - m_new)
        p = jnp.exp2(s - m_new)
        return m_new, alpha * l + jnp.sum(p, axis=0, keepdims=True), alpha, p.astype(BF16)

    def accumulate(accs, alpha, pb, j):
        return tuple(alpha[:, g * tq:(g + 1) * tq] * accs[g]
                     + _dot(vt_ref[j, g * HEAD_DIM:(g + 1) * HEAD_DIM, :], pb[:, g * tq:(g + 1) * tq])
                     for g in range(ng))

    m = jnp.full((1, ng * tq), NEG_INF, F32)
    l = jnp.zeros((1, ng * tq), F32)
    accs = tuple(jnp.zeros((HEAD_DIM, tq), F32) for _ in range(ng))
    def trip(first, count, carry, diagonal=False):
        m, l, accs = carry
        tiles = [first + u for u in range(count)]
        ss = [logits(j, u if diagonal else None) for u, j in enumerate(tiles)]
        for j, s in zip(tiles, ss):
            m, l, alpha, pb = softmax(m, l, s)
            accs = accumulate(accs, alpha, pb, j)
        return m, l, accs

    n = qi * per
    carry = trip(n, per, (m, l, accs), diagonal=True)
    carry = lax.fori_loop(0, n // KEY_UNROLL, lambda t, c: trip(t * KEY_UNROLL, KEY_UNROLL, c), carry)
    m, l, accs = lax.fori_loop((n // KEY_UNROLL) * KEY_UNROLL, n, lambda j, c: trip(j, 1, c), carry)
    return jnp.concatenate([accs[g] / l[:, g * tq:(g + 1) * tq] for g in range(ng)], axis=0)


def _store_heads(o_ref, o_t):
    for r in range(o_t.shape[0] // LANES):
        for c in range(o_t.shape[1] // LANES):
            o_ref[c * LANES:(c + 1) * LANES, r * LANES:(r + 1) * LANES] = jnp.transpose(
                o_t[r * LANES:(r + 1) * LANES, c * LANES:(c + 1) * LANES])


def _fox_prompt_kernel(q_ref, k_ref, v_ref, crow_ref, o_ref, kb_ref, vt_ref, cb_ref, *, tq):
    qi = pl.program_id(2)

    @pl.when(qi == 0)
    def _():
        _stage_kv(k_ref, v_ref, kb_ref, vt_ref)
        for g in range(cb_ref.shape[0]):
            for j in range(k_ref.shape[0] // LANES):
                sl = slice(j * LANES, (j + 1) * LANES)
                cb_ref[g, sl, :] = jnp.transpose(
                    jnp.broadcast_to(crow_ref[0, g:g + 1, sl] * LOG2_E, (LANES, LANES)))

    q = q_ref[...] * (DH_C ** -0.5 * LOG2_E)

    def adjust(s, j, start, diagonal):
        return s - jnp.concatenate([cb_ref[g, pl.ds(start, KEY_TILE), :] for g in range(cb_ref.shape[0])
                                    for _ in range(tq // LANES)], axis=1)

    qbs = [jnp.where(mask, q, 0.0).astype(BF16) for mask in _head_masks(q.shape)]
    _store_heads(o_ref, _flash_group(qbs, qi, kb_ref, vt_ref, tq, adjust))


def _fox_prompt(q, k, v, crow, nseq, seq):
    tq = FOX_TILE
    nq = seq // tq
    ng = FOX_GROUP
    width = ng * HEAD_DIM
    return pl.pallas_call(
        functools.partial(_fox_prompt_kernel, tq=tq),
        out_shape=jax.ShapeDtypeStruct(q.shape, F32),
        grid=(nseq, H_C // ng, nq),
        in_specs=[pl.BlockSpec((tq, width), lambda n, p, i: (n * nq + i, p)),
                  pl.BlockSpec((seq, width), lambda n, p, i: (n, p)),
                  pl.BlockSpec((1, width, seq), lambda n, p, i: (n, p, 0)),
                  pl.BlockSpec((1, ng, seq), lambda n, p, i: (p, 0, n))],
        out_specs=pl.BlockSpec((tq, width), lambda n, p, i: (n * nq + i, p)),
        scratch_shapes=[pltpu.VMEM((seq, width), BF16), pltpu.VMEM((seq // KEY_TILE, width, KEY_TILE), BF16),
                        pltpu.VMEM((ng, seq, LANES), F32)],
        compiler_params=_params("arbitrary", "arbitrary", "arbitrary"),
        name="fox_prompt",
    )(q, k, v, crow)


def _rank_rows(g, n_valid):
    nb = g.shape[0]
    r = lax.broadcasted_iota(jnp.int32, g.shape, 0)
    g = jnp.where(r < n_valid, g, NEG_INF)
    rank = jnp.zeros(g.shape, jnp.int32)
    for i in range(nb):
        gi = g[i:i + 1, :]
        rank = rank + ((gi > g) | ((gi == g) & (i < r))).astype(jnp.int32)
    return rank, r


def _moba_prompt_kernel(q_ref, k_ref, v_ref, o_ref, kb_ref, vt_ref, kmean_ref, sel_ref, *, tq, nb):
    qi = pl.program_id(2)

    @pl.when(qi == 0)
    def _():
        _stage_kv(k_ref, v_ref, kb_ref, vt_ref)
        kmean_ref[...] = jnp.zeros_like(kmean_ref)
        for j in range(nb):
            kmean_ref[j:j + 1, :] = jnp.mean(k_ref[j * MOBA_BLOCK:(j + 1) * MOBA_BLOCK, :], axis=0, keepdims=True)

    q = q_ref[...]
    km_hi, km_lo = _split_bf16(kmean_ref[...])
    qbs = []
    for hh, mask in enumerate(_head_masks(q.shape)):
        qm = jnp.where(mask, q, 0.0)
        q_hi, q_lo = _split_bf16(qm)
        gate = _dot_nt(km_hi, q_hi) + _dot_nt(km_lo, q_hi) + _dot_nt(km_hi, q_lo)
        rank, r = _rank_rows(gate, qi)
        sel_ref[hh] = ((rank < MOBA_TOPK) & (r < qi)).astype(F32)
        qbs.append((qm * (DH_A ** -0.5 * LOG2_E)).astype(BF16))

    def adjust(s, j, start, diagonal):
        if diagonal:
            return s
        blk = (j * KEY_TILE) // MOBA_BLOCK
        chosen = jnp.concatenate([sel_ref[g, pl.ds(blk, 1), :] for g in range(sel_ref.shape[0])], axis=1)
        return jnp.where(chosen > 0.5, s, NEG_INF)

    _store_heads(o_ref, _flash_group(qbs, qi, kb_ref, vt_ref, tq, adjust))


def _moba_prompt(q, k, v, nseq, seq):
    tq = ATT_TILE
    assert tq == MOBA_BLOCK and seq % MOBA_BLOCK == 0 and seq // MOBA_BLOCK >= MOBA_TOPK
    nq = seq // tq
    nb = seq // MOBA_BLOCK
    nbp = -(-nb // SUBLANES) * SUBLANES
    ng = MOBA_GROUP
    width = ng * HEAD_DIM
    return pl.pallas_call(
        functools.partial(_moba_prompt_kernel, tq=tq, nb=nb),
        out_shape=jax.ShapeDtypeStruct(q.shape, F32),
        grid=(nseq, H_A // ng, nq),
        in_specs=[pl.BlockSpec((tq, width), lambda n, p, i: (n * nq + i, p)),
                  pl.BlockSpec((seq, width), lambda n, p, i: (n, p)),
                  pl.BlockSpec((1, width, seq), lambda n, p, i: (n, p, 0))],
        out_specs=pl.BlockSpec((tq, width), lambda n, p, i: (n * nq + i, p)),
        scratch_shapes=[pltpu.VMEM((seq, width), BF16), pltpu.VMEM((seq // KEY_TILE, width, KEY_TILE), BF16),
                        pltpu.VMEM((nbp, width), F32), pltpu.VMEM((ng, nbp, tq), F32)],
        compiler_params=_params("arbitrary", "arbitrary", "arbitrary"),
        name="moba_prompt",
    )(q, k, v)


def _column_replicated(row):
    return jnp.concatenate(
        [jnp.transpose(jnp.broadcast_to(row[:, c * LANES:(c + 1) * LANES], (LANES, LANES)))
         for c in range(row.shape[1] // LANES)], axis=0)


def _head_sums(x, dh):
    return jnp.concatenate([jnp.sum(x[h * dh:(h + 1) * dh, :], axis=0, keepdims=True)
                            for h in range(x.shape[0] // dh)], axis=0)


def _moba_sample_select_kernel(pt_ref, q_ref, *refs, group, nblk):
    k_refs = refs[:group]
    idx_ref = refs[group]
    qcol_ref, gate_ref = refs[group + 1:]
    g = pl.program_id(1)

    @pl.when(g == 0)
    def _():
        qcol_ref[...] = _column_replicated(q_ref[0])
        gate_ref[...] = jnp.zeros_like(gate_ref)

    lane = lax.broadcasted_iota(jnp.int32, gate_ref.shape, 1)
    pages_per_block = MOBA_BLOCK // PAGE_SIZE
    qcol = qcol_ref[...]
    upd = jnp.zeros(gate_ref.shape, F32)
    for i in range(group):
        qk = _head_sums(k_refs[i][0] * qcol, DH_A)
        blk = (g * group + i) // pages_per_block
        val = jnp.sum(qk, axis=1, keepdims=True) * (1.0 / (PAGE_SIZE * pages_per_block))
        upd = upd + jnp.where(lane == blk, val, 0.0)
    gate_ref[...] += upd

    @pl.when(g == pl.num_programs(1) - 1)
    def _():
        gate = jnp.where(lane < nblk, gate_ref[...], NEG_INF)
        rank = jnp.zeros(gate.shape, jnp.int32)
        for i in range(nblk):
            gi = gate[:, i:i + 1]
            rank = rank + ((gi > gate) | ((gi == gate) & (i < lane))).astype(jnp.int32)
        out = jnp.zeros(gate.shape, F32)
        for t in range(MOBA_TOPK):
            it = jnp.sum(jnp.where((rank == t) & (lane < nblk), lane.astype(F32), 0.0), axis=1, keepdims=True)
            out = jnp.where(lane == t, it, out)
        idx_ref[0] = out.astype(jnp.int32)


def _moba_sample_select(q, pool_kt, page_table, nblk):
    nb = q.shape[0]
    npg = nblk * (MOBA_BLOCK // PAGE_SIZE)
    group = SELECT_PAGES if npg % SELECT_PAGES == 0 else PAGES_PER_STEP
    assert npg % group == 0 and MOBA_TOPK <= nblk <= LANES
    in_specs = [pl.BlockSpec((1, 1, W_A), lambda n, g, pt: (n, 0, 0))]
    for i in range(group):
        in_specs.append(pl.BlockSpec((1, W_A, PAGE_SIZE), lambda n, g, pt, i=i: (pt[n, g * group + i], 0, 0)))
    return pl.pallas_call(
        functools.partial(_moba_sample_select_kernel, group=group, nblk=nblk),
        out_shape=jax.ShapeDtypeStruct((nb, H_A, LANES), jnp.int32),
        grid_spec=pltpu.PrefetchScalarGridSpec(
            num_scalar_prefetch=1,
            grid=(nb, npg // group),
            in_specs=in_specs,
            out_specs=pl.BlockSpec((1, H_A, LANES), lambda n, g, pt: (n, 0, 0)),
            scratch_shapes=[pltpu.VMEM((W_A, LANES), F32), pltpu.VMEM((H_A, LANES), F32)]),
        compiler_params=_params("arbitrary", "arbitrary"),
        name="moba_sample_select",
    )(page_table, q.reshape(nb, 1, W_A), *([pool_kt] * group))


def _moba_sample_attend_kernel(pt_ref, idx_ref, q_ref, kn_ref, vn_ref, *refs, n_sel, n_own, qblk):
    n_pages = n_sel + n_own
    k_refs = refs[:H_A * n_pages]
    v_refs = refs[H_A * n_pages:2 * H_A * n_pages]
    o_ref = refs[2 * H_A * n_pages]
    n = pl.program_id(0)
    pages_per_block = MOBA_BLOCK // PAGE_SIZE
    for h in range(H_A):
        q = q_ref[0, h:h + 1, :] * (DH_A ** -0.5)
        qcol = _column_replicated(q)[0:DH_A]
        logits = []
        for i in range(n_pages):
            s = jnp.sum(k_refs[h * n_pages + i][0] * qcol, axis=0, keepdims=True)
            if i < n_sel:
                valid = idx_ref[n, h, i // pages_per_block] < qblk
                s = jnp.where(valid, s, NEG_INF)
            logits.append(s)
        s_new = jnp.sum(q * kn_ref[0, h:h + 1, :], axis=1, keepdims=True)
        m = s_new
        for s in logits:
            m = jnp.maximum(m, jnp.max(s, axis=1, keepdims=True))
        p_new = jnp.exp(s_new - m)
        l = p_new
        acc_t = jnp.zeros((DH_A, PAGE_SIZE), F32)
        for i, s in enumerate(logits):
            p = jnp.exp(s - m)
            l = l + jnp.sum(p, axis=1, keepdims=True)
            acc_t = acc_t + v_refs[h * n_pages + i][0] * p
        acc = jnp.sum(jnp.transpose(jnp.concatenate([acc_t, jnp.zeros((LANES - DH_A, PAGE_SIZE), F32)], axis=0)),
                      axis=0, keepdims=True)
        o_ref[0, h:h + 1, :] = (acc + p_new * vn_ref[0, h:h + 1, :]) / l


def _moba_sample_attend(q, k_new, v_new, pool_kt, pool_vt, page_table, idx):
    nb = q.shape[0]
    n_pages_total = page_table.shape[1]
    pages_per_block = MOBA_BLOCK // PAGE_SIZE
    past = n_pages_total * PAGE_SIZE
    qblk = past // MOBA_BLOCK
    n_sel = MOBA_TOPK * pages_per_block
    n_own = 1 if (past - PAGE_SIZE) // MOBA_BLOCK == qblk else 0

    def sel_map(h, i):
        def index_map(n, pt, ix):
            lpage = jnp.minimum(ix[n, h, i // pages_per_block] * pages_per_block + i % pages_per_block,
                                n_pages_total - 1)
            return (pt[n, lpage], h, 0)
        return index_map

    def own_map(h):
        return lambda n, pt, ix: (pt[n, n_pages_total - 1], h, 0)

    page_specs = []
    for h in range(H_A):
        page_specs += [pl.BlockSpec((1, DH_A, PAGE_SIZE), sel_map(h, i)) for i in range(n_sel)]
        page_specs += [pl.BlockSpec((1, DH_A, PAGE_SIZE), own_map(h))] * n_own
    tok_spec = pl.BlockSpec((1, H_A, LANES), lambda n, pt, ix: (n, 0, 0))
    tok = lambda a: _pad_lanes(a.reshape(nb, H_A, DH_A))
    n_pg = H_A * (n_sel + n_own)
    att = pl.pallas_call(
        functools.partial(_moba_sample_attend_kernel, n_sel=n_sel, n_own=n_own, qblk=qblk),
        out_shape=jax.ShapeDtypeStruct((nb, H_A, LANES), F32),
        grid_spec=pltpu.PrefetchScalarGridSpec(
            num_scalar_prefetch=2,
            grid=(nb,),
            in_specs=[tok_spec, tok_spec, tok_spec] + page_specs + page_specs,
            out_specs=tok_spec),
        compiler_params=_params("arbitrary"),
        name="moba_sample_attend",
    )(page_table, idx, tok(q), tok(k_new), tok(v_new), *([pool_kt] * n_pg), *([pool_vt] * n_pg))
    return att[:, :, :DH_A].reshape(nb, W_A)


def _mlstm_sample_kernel(qk_ref, st_ref, v_ref, ob_ref, gate_ref, c_ref, n_ref, m_ref, cw_ref, cb_ref, nw_ref,
                         g_ref, cn_ref, nn_ref, mn_ref):
    cw = cw_ref[...]
    st = st_ref[0]
    y = cb_ref[...] + cw[MLSTM_CONV - 1:MLSTM_CONV] * qk_ref[0]
    for t in range(MLSTM_CONV - 1):
        y = y + cw[t:t + 1] * st[t:t + 1]
    qk = y * _sigmoid(y)
    gates = gate_ref[0]
    m_in = m_ref[0]
    m_out = jnp.zeros((1, LANES), F32)
    lane = lax.broadcasted_iota(jnp.int32, (1, LANES), 1)
    for h in range(H_B):
        q = qk[:, h * DH_B:(h + 1) * DH_B]
        k = qk[:, W_B + h * DH_B:W_B + (h + 1) * DH_B] * (DH_B ** -0.5)
        v = v_ref[0][:, h * DH_B:(h + 1) * DH_B]
        ig = gates[:, h:h + 1]
        lf = gates[:, H_B + h:H_B + h + 1]
        m_old = m_in[:, h:h + 1]
        m_row = jnp.maximum(ig, lf + m_old)
        w_inter = jnp.exp(lf + m_old - m_row)
        w_s = jnp.exp(ig - m_row)
        s = jnp.sum(q * k, axis=1, keepdims=True) * w_s
        cmat = c_ref[0, h]
        nvec = n_ref[0, h:h + 1, :]
        qc = _dot(jnp.broadcast_to(q, (SUBLANES, DH_B)).astype(BF16), cmat.astype(BF16))[0:1]
        num = s * v + w_inter * qc
        den = s + w_inter * jnp.sum(q * nvec, axis=1, keepdims=True)
        hh = num / jnp.maximum(jnp.abs(den), jnp.exp(-m_row))
        k_col = jnp.transpose(jnp.broadcast_to(k, (DH_B, DH_B)))
        cn_ref[0, h] = w_inter * cmat + w_s * (k_col * v)
        nn_ref[0, h:h + 1, :] = w_inter * nvec + w_s * k
        m_out = jnp.where(lane == h, m_row, m_out)
        hn = _rms(hh, nw_ref[:, h * DH_B:(h + 1) * DH_B])
        g_ref[0, :, h * DH_B:(h + 1) * DH_B] = _sigmoid(ob_ref[0][:, h * DH_B:(h + 1) * DH_B]) * hn
    mn_ref[0] = m_out


def _mlstm_sample(qk_pre, conv_state, v_pre, ob, gates, c, nvec, m, conv_w, conv_b, norm_w):
    nb = qk_pre.shape[0]
    tok = lambda a: a.reshape(nb, 1, a.shape[-1])
    tspec = lambda w: pl.BlockSpec((1, 1, w), lambda n: (n, 0, 0))
    full = lambda shape: pl.BlockSpec(shape, lambda n: (0,) * len(shape))
    g, cn, nn, mn = pl.pallas_call(
        _mlstm_sample_kernel,
        out_shape=[jax.ShapeDtypeStruct((nb, 1, W_B), F32),
                   jax.ShapeDtypeStruct((nb, H_B, DH_B, DH_B), F32),
                   jax.ShapeDtypeStruct((nb, H_B, DH_B), F32),
                   jax.ShapeDtypeStruct((nb, 1, LANES), F32)],
        grid=(nb,),
        in_specs=[tspec(2 * W_B),
                  pl.BlockSpec((1, MLSTM_CONV - 1, 2 * W_B), lambda n: (n, 0, 0)),
                  tspec(W_B), tspec(W_B), tspec(LANES),
                  pl.BlockSpec((1, H_B, DH_B, DH_B), lambda n: (n, 0, 0, 0)),
                  pl.BlockSpec((1, H_B, DH_B), lambda n: (n, 0, 0)),
                  tspec(LANES),
                  full((MLSTM_CONV, 2 * W_B)), full((1, 2 * W_B)), full((1, W_B))],
        out_specs=[tspec(W_B),
                   pl.BlockSpec((1, H_B, DH_B, DH_B), lambda n: (n, 0, 0, 0)),
                   pl.BlockSpec((1, H_B, DH_B), lambda n: (n, 0, 0)),
                   tspec(LANES)],
        compiler_params=_params("arbitrary"),
        name="mlstm_sample",
    )(tok(qk_pre), conv_state, tok(v_pre), tok(ob), tok(gates), c, nvec, tok(_pad_lanes(m)),
      conv_w, conv_b.reshape(1, -1), norm_w.reshape(1, -1))
    return g.reshape(nb, W_B), cn, nn, mn[:, 0, :H_B]


def _fox_sample_kernel(pt_ref, q_ref, kn_ref, vn_ref, lfn_ref, *refs, group):
    k_refs = refs[:group]
    v_refs = refs[group:2 * group]
    lf_refs = refs[2 * group:3 * group]
    o_ref = refs[3 * group]
    qcol_ref, m_ref, l_ref, run_ref, acc_ref = refs[3 * group + 1:]
    g = pl.program_id(1)

    @pl.when(g == 0)
    def _():
        qcol_ref[...] = _column_replicated(q_ref[0] * (DH_C ** -0.5))
        m_ref[...] = jnp.full(m_ref.shape, NEG_INF, F32)
        l_ref[...] = jnp.zeros_like(l_ref)
        run_ref[...] = jnp.zeros_like(run_ref)
        acc_ref[...] = jnp.zeros_like(acc_ref)

    qcol = qcol_ref[...]
    run = run_ref[...]
    logits = []
    for i in range(group):
        qk = _head_sums(k_refs[i][0] * qcol, DH_C)
        cum = run + _scan_lanes(lf_refs[i][0], jnp.add, 0.0)
        run = cum[:, PAGE_SIZE - 1:PAGE_SIZE]
        logits.append(qk - cum)
    s = jnp.concatenate(logits, axis=1)
    m = m_ref[...]
    m_new = jnp.maximum(m, jnp.max(s, axis=1, keepdims=True))
    alpha = jnp.exp(m - m_new)
    p = jnp.exp(s - m_new)
    l = alpha * l_ref[...] + jnp.sum(p, axis=1, keepdims=True)
    m_ref[...], l_ref[...], run_ref[...] = m_new, l, run
    for h in range(H_C):
        rows = slice(h * DH_C, (h + 1) * DH_C)
        a = acc_ref[rows, :] * alpha[h:h + 1, :]
        for i in range(group):
            a = a + v_refs[i][0, rows, :] * p[h:h + 1, i * PAGE_SIZE:(i + 1) * PAGE_SIZE]
        acc_ref[rows, :] = a

    @pl.when(g == pl.num_programs(1) - 1)
    def _():
        head = lax.broadcasted_iota(jnp.int32, (H_C, W_C), 0)
        mine = (lax.broadcasted_iota(jnp.int32, (H_C, W_C), 1) // DH_C) == head
        q_bd = jnp.where(mine, jnp.broadcast_to(q_ref[0], (H_C, W_C)), 0.0) * (DH_C ** -0.5)
        hrow = lax.broadcasted_iota(jnp.int32, (H_C, LANES), 0)
        hlane = lax.broadcasted_iota(jnp.int32, (H_C, LANES), 1)
        lf_new = jnp.sum(jnp.where(hrow == hlane, jnp.broadcast_to(lfn_ref[0], (H_C, LANES)), 0.0),
                         axis=1, keepdims=True)
        s_new = jnp.sum(q_bd * kn_ref[0], axis=1, keepdims=True) - (run + lf_new)
        m_f = jnp.maximum(m_new, s_new)
        alpha_f = jnp.exp(m_new - m_f)
        p_new = jnp.exp(s_new - m_f)
        l_f = alpha_f * l + p_new
        past = jnp.concatenate([jnp.sum(jnp.transpose(acc_ref[c * LANES:(c + 1) * LANES, :]), axis=0, keepdims=True)
                                for c in range(W_C // LANES)], axis=1)
        per_lane = lambda col: jnp.sum(jnp.where(mine, col, 0.0), axis=0, keepdims=True)
        o_ref[0] = (past * per_lane(alpha_f) + per_lane(p_new) * vn_ref[0]) / per_lane(l_f)


def _fox_sample(q, k_new, v_new, lf_new, pool_k, pool_v, pool_lf_t, page_table):
    nb = q.shape[0]
    n_pages = page_table.shape[1]
    group = PAGES_PER_STEP
    assert n_pages % group == 0
    tok_spec = pl.BlockSpec((1, 1, W_C), lambda n, g, pt: (n, 0, 0))
    tok = lambda a: a.reshape(nb, 1, a.shape[-1])
    page = lambda i, shape: pl.BlockSpec(shape, lambda n, g, pt: (pt[n, g * group + i], 0, 0))
    in_specs = [tok_spec, tok_spec, tok_spec, pl.BlockSpec((1, 1, LANES), lambda n, g, pt: (n, 0, 0))]
    in_specs += [page(i, (1, W_C, PAGE_SIZE)) for i in range(group)]
    in_specs += [page(i, (1, W_C, PAGE_SIZE)) for i in range(group)]
    in_specs += [page(i, (1, H_C, PAGE_SIZE)) for i in range(group)]
    return pl.pallas_call(
        functools.partial(_fox_sample_kernel, group=group),
        out_shape=jax.ShapeDtypeStruct((nb, 1, W_C), F32),
        grid_spec=pltpu.PrefetchScalarGridSpec(
            num_scalar_prefetch=1,
            grid=(nb, n_pages // group),
            in_specs=in_specs,
            out_specs=tok_spec,
            scratch_shapes=[pltpu.VMEM((W_C, LANES), F32), pltpu.VMEM((H_C, 1), F32), pltpu.VMEM((H_C, 1), F32),
                            pltpu.VMEM((H_C, 1), F32), pltpu.VMEM((W_C, PAGE_SIZE), F32)]),
        compiler_params=_params("arbitrary", "arbitrary"),
        name="fox_sample",
    )(page_table, tok(q), tok(k_new), tok(v_new), tok(lf_new),
      *([pool_k] * group), *([pool_v] * group), *([pool_lf_t] * group)).reshape(nb, W_C)


EVEN_SEGS = ((0, W_A), (W_A, W_A), (2 * W_A, W_A), (3 * W_A, 2 * W_B), (3 * W_A + 2 * W_B, W_B),
             (3 * W_A + 3 * W_B, W_B), (3 * W_A + 4 * W_B, LANES))
ODD_SEGS = ((0, W_C), (W_C, W_C), (2 * W_C, W_C), (3 * W_C, LANES))


def _even_weights(w_in, b_i, b_f):
    main = 3 * W_A + 4 * W_B
    w_pad = jnp.concatenate([w_in[:, :main], _pad_lanes(w_in[:, main:])], axis=1)
    w_bf = w_pad.astype(BF16)
    wq = w_in[:, :W_A]
    w_lo = (wq - wq.astype(BF16).astype(F32)).astype(BF16)
    gate_bias = _pad_lanes(jnp.concatenate([b_i, b_f]).reshape(1, -1))
    return w_bf, w_lo, gate_bias


def _odd_weights(w_in, b_f):
    w_pad = jnp.concatenate([w_in[:, :3 * W_C], _pad_lanes(w_in[:, 3 * W_C:])], axis=1)
    return w_pad.astype(BF16), _pad_lanes(b_f.reshape(1, -1))


def _feature_major(pool):
    n_pool, page, heads, dh = pool.shape
    return jnp.transpose(pool, (0, 2, 3, 1)).reshape(n_pool, heads * dh, page)


def _time_minor_to_cache(a_t, heads):
    n, width, seq = a_t.shape
    return jnp.transpose(a_t.reshape(n, heads, width // heads, seq), (0, 3, 1, 2))


def _rows_to_sublanes(a):
    return jnp.pad(a.T, ((0, SUBLANES - a.shape[1]), (0, 0)))


def kernel(x_prompt, x_sample, cache_moba_k, cache_moba_v, state_mlstm_c, state_mlstm_n, state_mlstm_m,
           state_mlstm_conv, cache_fox_k, cache_fox_v, cache_fox_logf, state_ffn_conv, page_table, norm_w,
           even_w_in, mlstm_conv_w, mlstm_conv_b, mlstm_b_i, mlstm_b_f, mlstm_norm_w, even_w_out, fox_w_in,
           fox_b_f, fox_w_out, ffn_w_in, ffn_conv_w, ffn_conv_b, ffn_w_out):
    bp, sp, d = x_prompt.shape
    bs, ss, _ = x_sample.shape
    assert ss == 1 and d == D_MODEL
    depth = norm_w.shape[0]
    n_pages = page_table.shape[1]
    tp = bp * sp
    yp = x_prompt.reshape(tp, d)
    ys = x_sample.reshape(bs, d)
    outs = {name: [] for name in (
        "mk_p", "mv_p", "mk_s", "mv_s", "mc_p", "mn_p", "mm_p", "mconv_p", "mc_s", "mn_s", "mm_s", "mconv_s",
        "fk_p", "fv_p", "fl_p", "fk_s", "fv_s", "fl_s", "ff_p", "ff_s")}
    for layer in range(depth):
        nw = norm_w[layer]
        if layer % 2 == 0:
            e = layer // 2
            w_bf, w_lo, gate_bias = _even_weights(even_w_in[e], mlstm_b_i[e], mlstm_b_f[e])
            w_out_bf = even_w_out[e].astype(BF16)
            proj = lambda x, tm, **kw: _rms_proj(x, nw[0], w_bf, w_lo, gate_bias, EVEN_SEGS, 1, (H_B, 2 * H_B), tm,
                                                 **kw)
            qa, ka, ka_t, va_t, qkb, vb, ob, gates = proj(yp, ROW_TILE, cols={1: True, 2: False}, seq_rows=sp)
            att = _moba_prompt(qa, ka, va_t, bp, sp)
            b_t, m_t, mrow_t = _mlstm_gate_scan(_rows_to_sublanes(gates[:, :H_B]),
                                                _rows_to_sublanes(gates[:, H_B:2 * H_B]), sp)
            cols = _pad_lanes(jnp.concatenate([b_t[:H_B].T, m_t[:H_B].T, mrow_t[:H_B].T], axis=1))
            gated, cc, nn, mfin = _mlstm_prompt(qkb, vb, ob, b_t, cols, mlstm_conv_w[e], mlstm_conv_b[e],
                                                mlstm_norm_w[e], bp, sp)
            yp = _proj_res([att, gated], [w_out_bf[:W_A], w_out_bf[W_A:]], yp, nw[1], FFN_ROWS)
            outs["mk_p"].append(_time_minor_to_cache(ka_t, H_A))
            outs["mv_p"].append(_time_minor_to_cache(va_t, H_A))
            outs["mc_p"].append(cc)
            outs["mn_p"].append(nn)
            outs["mm_p"].append(mfin[:, 0, 2 * H_B:3 * H_B])
            outs["mconv_p"].append(qkb.reshape(bp, sp, 2 * W_B)[:, sp - (MLSTM_CONV - 1):])
            qa, ka, va, qkb, vb, ob, gates = proj(ys, bs)
            pool_kt = _feature_major(cache_moba_k[e])
            pool_vt = _feature_major(cache_moba_v[e])
            nblk = (n_pages * PAGE_SIZE) // MOBA_BLOCK
            idx = _moba_sample_select(qa, pool_kt, page_table, nblk)[:, :, :MOBA_TOPK]
            att = _moba_sample_attend(qa, ka, va, pool_kt, pool_vt, page_table, idx)
            gated, cc, nn, mm = _mlstm_sample(qkb, state_mlstm_conv[e], vb, ob, gates, state_mlstm_c[e],
                                              state_mlstm_n[e], state_mlstm_m[e], mlstm_conv_w[e],
                                              mlstm_conv_b[e], mlstm_norm_w[e])
            ys = _proj_res([att, gated], [w_out_bf[:W_A], w_out_bf[W_A:]], ys, nw[1], bs)
            outs["mk_s"].append(ka.reshape(bs, ss, H_A, DH_A))
            outs["mv_s"].append(va.reshape(bs, ss, H_A, DH_A))
            outs["mc_s"].append(cc)
            outs["mn_s"].append(nn)
            outs["mm_s"].append(mm)
            outs["mconv_s"].append(jnp.concatenate([state_mlstm_conv[e][:, 1:], qkb[:, None, :]], axis=1))
        else:
            o = layer // 2
            w_bf, gate_bias = _odd_weights(fox_w_in[o], fox_b_f[o])
            w_out_bf = fox_w_out[o].astype(BF16)
            proj = lambda x, tm, **kw: _rms_proj(x, nw[0], w_bf, None, gate_bias, ODD_SEGS, 0, (0, H_C), tm, **kw)
            q, k, k_t, v_t, lf = proj(yp, ROW_TILE, cols={1: True, 2: False}, seq_rows=sp)
            lf_t = lf[:, :H_C].T
            crow = _cumsum_time(lf_t, sp).reshape(H_C // FOX_GROUP, FOX_GROUP, tp)
            att = _fox_prompt(q, k, v_t, crow, bp, sp)
            yp = _proj_res([att], [w_out_bf], yp, nw[1], FFN_ROWS)
            outs["fk_p"].append(_time_minor_to_cache(k_t, H_C))
            outs["fv_p"].append(_time_minor_to_cache(v_t, H_C))
            outs["fl_p"].append(lf[:, :H_C].reshape(bp, sp, H_C))
            q, k, v, lf = proj(ys, bs)
            pool_lf_t = jnp.transpose(cache_fox_logf[o], (0, 2, 1))
            att = _fox_sample(q, k, v, lf, _feature_major(cache_fox_k[o]), _feature_major(cache_fox_v[o]),
                              pool_lf_t, page_table)
            ys = _proj_res([att], [w_out_bf], ys, nw[1], bs)
            outs["fk_s"].append(k.reshape(bs, ss, H_C, DH_C))
            outs["fv_s"].append(v.reshape(bs, ss, H_C, DH_C))
            outs["fl_s"].append(lf[:, :H_C].reshape(bs, ss, H_C))
        w_in_bf = ffn_w_in[layer].astype(BF16)
        w_o_bf = ffn_w_out[layer].astype(BF16)
        yp, buf_p = _ffn(yp, nw[2], w_in_bf, ffn_conv_w[layer], ffn_conv_b[layer], w_o_bf, nw[3], FFN_ROWS, sp)
        ys, u_s = _ffn(ys, nw[2], w_in_bf, ffn_conv_w[layer], ffn_conv_b[layer], w_o_bf, nw[3], bs, 1,
                       state=state_ffn_conv[layer])
        outs["ff_p"].append(buf_p)
        outs["ff_s"].append(jnp.concatenate([state_ffn_conv[layer][:, 1:], u_s[:, None, :]], axis=1))
    st = {name: jnp.stack(vals) for name, vals in outs.items()}
    return (yp.reshape(bp, sp, d), ys.reshape(bs, ss, d), st["mk_p"], st["mv_p"], st["mk_s"], st["mv_s"],
            st["mc_p"], st["mn_p"], st["mm_p"], st["mconv_p"], st["mc_s"], st["mn_s"], st["mm_s"], st["mconv_s"],
            st["fk_p"], st["fv_p"], st["fl_p"], st["fk_s"], st["fv_s"], st["fl_s"], st["ff_p"], st["ff_s"])
```

```python
import functools

import jax
import jax.numpy as jnp
from jax import lax
from jax.experimental import pallas as pl
from jax.experimental.pallas import tpu as pltpu

F32 = jnp.float32
BF16 = jnp.bfloat16
NEG_INF = float("-inf")

D_MODEL = 1024
PAGE_SIZE = 128
H_A, DH_A = 8, 64
W_A = H_A * DH_A
MOBA_BLOCK = 256
MOBA_TOPK = 3
H_B, DH_B = 4, 128
W_B = H_B * DH_B
MLSTM_CONV = 4
H_C, DH_C = 16, 64
W_C = H_C * DH_C
D_FF = 2816
FFN_CONV = 3
RMS_EPS = 1e-6

MXU_DEPTH = 256
LANES = 128
SUBLANES = 8
VMEM_LIMIT = 56 * 1024 * 1024

ROW_TILE = 256
ATT_TILE = 256
FOX_TILE = 512
KEY_TILE = 256
KEY_UNROLL = 4
MOBA_GROUP = 8
FOX_GROUP = 4
MLSTM_TILE = 256
FFN_ROWS = 512
FFN_COLS = 1408
FFN_SLICE = 256
PAGES_PER_STEP = 16
SELECT_PAGES = 16


def _params(*sem):
    return pltpu.CompilerParams(dimension_semantics=sem, vmem_limit_bytes=VMEM_LIMIT)


def _rms(x, w):
    return x * lax.rsqrt(jnp.mean(x * x, axis=-1, keepdims=True) + RMS_EPS) * w


def _sigmoid(x):
    return 1.0 / (1.0 + jnp.exp(-x))


def _log_sigmoid(x):
    return -(jnp.maximum(-x, 0.0) + jnp.log1p(jnp.exp(-jnp.abs(x))))


def _gelu_tanh(x):
    c = 0.7978845608028654
    return 0.5 * x * (1.0 + jnp.tanh(c * (x + 0.044715 * (x * x * x))))


def _dot(a, b):
    return jnp.dot(a, b, preferred_element_type=F32)


def _dot_nt(a, b):
    return lax.dot_general(a, b, (((1,), (1,)), ((), ())), preferred_element_type=F32)


def _split_bf16(x):
    hi = x.astype(BF16)
    lo = (x - hi.astype(F32)).astype(BF16)
    return hi, lo


def _pad_lanes(a, width=LANES):
    return jnp.pad(a, ((0, 0),) * (a.ndim - 1) + ((0, width - a.shape[-1]),))


def _rms_proj_kernel(*refs, segs, n_hi, gate_lf, cols):
    x_ref, nw_ref, w_ref = refs[:3]
    pos = 3
    wlo_ref = None
    if n_hi:
        wlo_ref = refs[pos]
        pos += 1
    gb_ref = refs[pos]
    out_refs = list(refs[pos + 1:])
    h = _rms(x_ref[...], nw_ref[...])
    hb, hl = _split_bf16(h)
    lo_start = 0
    for i, (start, width) in enumerate(segs):
        w = w_ref[:, start:start + width]
        z = _dot(hb, w)
        if i < n_hi:
            z = z + _dot(hl, w) + _dot(hb, wlo_ref[:, lo_start:lo_start + width])
            lo_start += width
        if i == len(segs) - 1:
            z = z + gb_ref[...]
            lane = lax.broadcasted_iota(jnp.int32, z.shape, 1)
            z = jnp.where((lane >= gate_lf[0]) & (lane < gate_lf[1]), _log_sigmoid(z), z)
        if cols.get(i, True):
            out_refs.pop(0)[...] = z
        if i in cols:
            zt_ref = out_refs.pop(0)
            for r in range(z.shape[0] // LANES):
                for c in range(width // LANES):
                    zt_ref[0, c * LANES:(c + 1) * LANES, r * LANES:(r + 1) * LANES] = jnp.transpose(
                        z[r * LANES:(r + 1) * LANES, c * LANES:(c + 1) * LANES])


def _rms_proj(x, nw, w_bf, w_lo, gate_bias, segs, n_hi, gate_lf, tm, cols=None, seq_rows=None):
    rows, d = x.shape
    ncols = w_bf.shape[1]
    cols = cols or {}
    in_specs = [pl.BlockSpec((tm, d), lambda i: (i, 0)),
                pl.BlockSpec((1, d), lambda i: (0, 0)),
                pl.BlockSpec((d, ncols), lambda i: (0, 0))]
    args = [x, nw.reshape(1, d), w_bf]
    if n_hi:
        in_specs.append(pl.BlockSpec(w_lo.shape, lambda i: (0, 0)))
        args.append(w_lo)
    in_specs.append(pl.BlockSpec((1, LANES), lambda i: (0, 0)))
    args.append(gate_bias)
    out_shape, out_specs = [], []
    for i, (_, wd) in enumerate(segs):
        if cols.get(i, True):
            out_shape.append(jax.ShapeDtypeStruct((rows, wd), F32))
            out_specs.append(pl.BlockSpec((tm, wd), lambda i: (i, 0)))
        if i in cols:
            tps = seq_rows // tm
            out_shape.append(jax.ShapeDtypeStruct((rows // seq_rows, wd, seq_rows), F32))
            out_specs.append(pl.BlockSpec((1, wd, tm), lambda i: (i // tps, 0, i % tps)))
    return pl.pallas_call(
        functools.partial(_rms_proj_kernel, segs=segs, n_hi=n_hi, gate_lf=gate_lf, cols=cols),
        out_shape=out_shape,
        grid=(rows // tm,),
        in_specs=in_specs,
        out_specs=out_specs,
        compiler_params=_params("arbitrary"),
        name="rms_proj",
    )(*args)


def _proj_res_kernel(*refs, n_in):
    a_refs = refs[:n_in]
    w_refs = refs[n_in:2 * n_in]
    x_ref, nw_ref, o_ref = refs[2 * n_in:]
    acc = None
    for a_ref, w_ref in zip(a_refs, w_refs):
        z = _dot(a_ref[...].astype(BF16), w_ref[...])
        acc = z if acc is None else acc + z
    o_ref[...] = x_ref[...] + _rms(acc, nw_ref[...])


def _proj_res(acts, ws, x, nw, tm):
    rows, d = x.shape
    n_in = len(acts)
    in_specs = ([pl.BlockSpec((tm, a.shape[1]), lambda i: (i, 0)) for a in acts]
                + [pl.BlockSpec(w.shape, lambda i: (0, 0)) for w in ws]
                + [pl.BlockSpec((tm, d), lambda i: (i, 0)), pl.BlockSpec((1, d), lambda i: (0, 0))])
    return pl.pallas_call(
        functools.partial(_proj_res_kernel, n_in=n_in),
        out_shape=jax.ShapeDtypeStruct((rows, d), F32),
        grid=(rows // tm,),
        in_specs=in_specs,
        out_specs=pl.BlockSpec((tm, d), lambda i: (i, 0)),
        compiler_params=_params("arbitrary"),
        name="proj_res",
    )(*acts, *ws, x, nw.reshape(1, d))


def _ffn_kernel(*refs, tm, tf, nj, tiles_per_seq, stateful):
    (x_ref, nw_in_ref, wa_ref, wb_ref, cwa_ref, cwb_ref, cba_ref, cbb_ref, wo_ref, nw_out_ref) = refs[:10]
    pos = 10
    if stateful:
        s0a_ref, s0b_ref, s1a_ref, s1b_ref = refs[pos:pos + 4]
        pos += 4
    y_ref, oa_ref, ob_ref = refs[pos:pos + 3]
    pos += 3
    hb_ref, acc_ref = refs[pos:pos + 2]
    pos += 2
    if not stateful:
        ubuf_ref, carry_ref = refs[pos:pos + 2]
    i = pl.program_id(0)
    j = pl.program_id(1)

    @pl.when(j == 0)
    def _():
        hb_ref[...] = _rms(x_ref[...], nw_in_ref[...]).astype(BF16)
        acc_ref[...] = jnp.zeros_like(acc_ref)

    hb = hb_ref[...]
    if not stateful:
        first = i % tiles_per_seq == 0

        @pl.when(first)
        def _():
            ubuf_ref[:, 0:SUBLANES, :] = jnp.zeros((2, SUBLANES, tf), F32)

        @pl.when(jnp.logical_not(first))
        def _():
            ubuf_ref[:, 0:SUBLANES, :] = carry_ref[:, j]

    gs = []
    for c0 in range(0, tf, FFN_SLICE):
        cs = slice(c0, min(c0 + FFN_SLICE, tf))
        halves = []
        for half, (w_ref, cw_ref, cb_ref) in enumerate(((wa_ref, cwa_ref, cba_ref), (wb_ref, cwb_ref, cbb_ref))):
            u = _dot(hb, w_ref[:, cs])
            cw = cw_ref[:, cs]
            if stateful:
                s0 = (s0a_ref, s0b_ref)[half][:, cs]
                s1 = (s1a_ref, s1b_ref)[half][:, cs]
                y = cw[0:1] * s0 + cw[1:2] * s1 + cw[2:3] * u + cb_ref[:, cs]
                (oa_ref, ob_ref)[half][:, cs] = u
            else:
                ubuf_ref[half, SUBLANES:, cs] = u
                y = (cw[0:1] * ubuf_ref[half, SUBLANES - 2:SUBLANES - 2 + tm, cs]
                     + cw[1:2] * ubuf_ref[half, SUBLANES - 1:SUBLANES - 1 + tm, cs]
                     + cw[2:3] * u + cb_ref[:, cs])
            halves.append(y)
        gs.append((_gelu_tanh(halves[0]) * halves[1]).astype(BF16))
    if not stateful:
        carry_ref[:, j] = ubuf_ref[:, tm:tm + SUBLANES, :]
        oa_ref[0] = ubuf_ref[0, tm + SUBLANES - 2:tm + SUBLANES, :]
        ob_ref[0] = ubuf_ref[1, tm + SUBLANES - 2:tm + SUBLANES, :]
    acc_ref[...] += _dot(jnp.concatenate(gs, axis=1), wo_ref[...])

    @pl.when(j == nj - 1)
    def _():
        y_ref[...] = x_ref[...] + _rms(acc_ref[...], nw_out_ref[...])


def _ffn(x, nw_in, w_in_bf, conv_w, conv_b, w_out_bf, nw_out, tm, seq_rows, state=None):
    rows, d = x.shape
    tf = FFN_COLS
    nj = D_FF // tf
    stateful = state is not None
    cb2 = conv_b.reshape(1, 2 * D_FF)
    in_specs = [pl.BlockSpec((tm, d), lambda i, j: (i, 0)),
                pl.BlockSpec((1, d), lambda i, j: (0, 0)),
                pl.BlockSpec((d, tf), lambda i, j: (0, j)),
                pl.BlockSpec((d, tf), lambda i, j: (0, nj + j)),
                pl.BlockSpec((FFN_CONV, tf), lambda i, j: (0, j)),
                pl.BlockSpec((FFN_CONV, tf), lambda i, j: (0, nj + j)),
                pl.BlockSpec((1, tf), lambda i, j: (0, j)),
                pl.BlockSpec((1, tf), lambda i, j: (0, nj + j)),
                pl.BlockSpec((tf, d), lambda i, j: (j, 0)),
                pl.BlockSpec((1, d), lambda i, j: (0, 0))]
    args = [x, nw_in.reshape(1, d), w_in_bf, w_in_bf, conv_w, conv_w, cb2, cb2, w_out_bf, nw_out.reshape(1, d)]
    scratch = [pltpu.VMEM((tm, d), BF16), pltpu.VMEM((tm, d), F32)]
    if stateful:
        s0, s1 = state[:, 0, :], state[:, 1, :]
        in_specs += [pl.BlockSpec((tm, tf), lambda i, j: (i, j)), pl.BlockSpec((tm, tf), lambda i, j: (i, nj + j)),
                     pl.BlockSpec((tm, tf), lambda i, j: (i, j)), pl.BlockSpec((tm, tf), lambda i, j: (i, nj + j))]
        args += [s0, s0, s1, s1]
        out_shape = [jax.ShapeDtypeStruct((rows, d), F32), jax.ShapeDtypeStruct((rows, D_FF), F32),
                     jax.ShapeDtypeStruct((rows, D_FF), F32)]
        out_specs = [pl.BlockSpec((tm, d), lambda i, j: (i, 0)), pl.BlockSpec((tm, tf), lambda i, j: (i, j)),
                     pl.BlockSpec((tm, tf), lambda i, j: (i, j))]
        tiles_per_seq = 1
    else:
        tiles_per_seq = seq_rows // tm
        ntiles = rows // tm
        out_shape = [jax.ShapeDtypeStruct((rows, d), F32), jax.ShapeDtypeStruct((ntiles, FFN_CONV - 1, D_FF), F32),
                     jax.ShapeDtypeStruct((ntiles, FFN_CONV - 1, D_FF), F32)]
        out_specs = [pl.BlockSpec((tm, d), lambda i, j: (i, 0)),
                     pl.BlockSpec((1, FFN_CONV - 1, tf), lambda i, j: (i, 0, j)),
                     pl.BlockSpec((1, FFN_CONV - 1, tf), lambda i, j: (i, 0, j))]
        scratch += [pltpu.VMEM((2, tm + SUBLANES, tf), F32), pltpu.VMEM((2, nj, SUBLANES, tf), F32)]
    y, ua, ub = pl.pallas_call(
        functools.partial(_ffn_kernel, tm=tm, tf=tf, nj=nj, tiles_per_seq=tiles_per_seq, stateful=stateful),
        out_shape=out_shape,
        grid=(rows // tm, nj),
        in_specs=in_specs,
        out_specs=out_specs,
        scratch_shapes=scratch,
        compiler_params=_params("arbitrary", "arbitrary"),
        name="conv_ffn",
    )(*args)
    if not stateful:
        ua, ub = ua[tiles_per_seq - 1::tiles_per_seq], ub[tiles_per_seq - 1::tiles_per_seq]
    return y, jnp.concatenate([ua, ub], axis=-1)


def _scan_lanes(x, op, fill):
    n = x.shape[1]
    lane = lax.broadcasted_iota(jnp.int32, x.shape, 1)
    s = 1
    while s < n:
        x = op(x, jnp.where(lane >= s, pltpu.roll(x, s, axis=1), fill))
        s *= 2
    return x


def _mlstm_gate_scan_kernel(ig_ref, lf_ref, b_ref, m_ref, mrow_ref):
    a = _scan_lanes(lf_ref[...], jnp.add, 0.0)
    b = ig_ref[...] - a
    m = jnp.maximum(_scan_lanes(b, jnp.maximum, NEG_INF), 0.0)
    b_ref[...] = b
    m_ref[...] = m
    mrow_ref[...] = a + m


def _mlstm_gate_scan(ig_t, lf_t, seq):
    rows, total = ig_t.shape
    spec = pl.BlockSpec((rows, seq), lambda n: (0, n))
    return pl.pallas_call(
        _mlstm_gate_scan_kernel,
        out_shape=[jax.ShapeDtypeStruct((rows, total), F32)] * 3,
        grid=(total // seq,),
        in_specs=[spec, spec],
        out_specs=[spec, spec, spec],
        compiler_params=_params("arbitrary"),
        name="mlstm_gate_scan",
    )(ig_t, lf_t)


def _cumsum_kernel(x_ref, o_ref):
    o_ref[...] = _scan_lanes(x_ref[...], jnp.add, 0.0)


def _cumsum_time(x_t, seq):
    rows, total = x_t.shape
    spec = pl.BlockSpec((rows, seq), lambda n: (0, n))
    return pl.pallas_call(
        _cumsum_kernel,
        out_shape=jax.ShapeDtypeStruct((rows, total), F32),
        grid=(total // seq,),
        in_specs=[spec],
        out_specs=spec,
        compiler_params=_params("arbitrary"),
        name="logf_cumsum",
    )(x_t)


def _mlstm_prompt_kernel(qk_ref, v_ref, ob_ref, rows_ref, cols_ref, cw_ref, cb_ref, nw_ref,
                         g_ref, c_ref, n_ref, mfin_ref, xbuf_ref, mprev_ref, *, tl):
    c = pl.program_id(1)
    nc = pl.num_programs(1)

    @pl.when(c == 0)
    def _():
        xbuf_ref[0:SUBLANES, :] = jnp.zeros((SUBLANES, 2 * W_B), F32)
        mprev_ref[...] = jnp.zeros_like(mprev_ref)
        c_ref[...] = jnp.zeros_like(c_ref)
        n_ref[...] = jnp.zeros_like(n_ref)

    xbuf_ref[SUBLANES:, :] = qk_ref[...]
    cw = cw_ref[...]
    y = cb_ref[...]
    for t in range(MLSTM_CONV):
        off = SUBLANES - (MLSTM_CONV - 1) + t
        y = y + cw[t:t + 1] * xbuf_ref[off:off + tl, :]
    xbuf_ref[0:SUBLANES, :] = xbuf_ref[tl:tl + SUBLANES, :]
    qk = y * _sigmoid(y)

    cols = cols_ref[...]
    mprev = mprev_ref[...]
    t_idx = lax.broadcasted_iota(jnp.int32, (tl, tl), 0)
    s_idx = lax.broadcasted_iota(jnp.int32, (tl, tl), 1)
    causal = s_idx <= t_idx
    for h in range(H_B):
        q = qk[:, h * DH_B:(h + 1) * DH_B]
        k = qk[:, W_B + h * DH_B:W_B + (h + 1) * DH_B] * (DH_B ** -0.5)
        v = v_ref[:, h * DH_B:(h + 1) * DH_B]
        qb, kb, vb = q.astype(BF16), k.astype(BF16), v.astype(BF16)
        b_row = rows_ref[h:h + 1, :]
        b_col = cols[:, h:h + 1]
        m_col = cols[:, H_B + h:H_B + h + 1]
        mrow_col = cols[:, 2 * H_B + h:2 * H_B + h + 1]
        m_last = cols[tl - 1:tl, H_B + h:H_B + h + 1]
        m_prev = mprev[:, H_B + h:H_B + h + 1]
        dmat = jnp.exp(jnp.where(causal, b_row - m_col, NEG_INF))
        s = _dot_nt(qb, kb) * dmat
        w_inter = jnp.exp(m_prev - m_col)
        cmat = c_ref[0, h]
        nvec = n_ref[0, h:h + 1, :]
        num = _dot(s.astype(BF16), vb) + w_inter * _dot(qb, cmat.astype(BF16))
        den = jnp.sum(s, axis=1, keepdims=True) + w_inter * jnp.sum(q * nvec, axis=1, keepdims=True)
        hh = num / jnp.maximum(jnp.abs(den), jnp.exp(-mrow_col))
        w_s = jnp.exp(b_col - m_last)
        w_c = jnp.exp(m_prev - m_last)
        kw = k * w_s
        c_ref[0, h] = w_c * cmat + lax.dot_general(kw.astype(BF16), vb, (((0,), (0,)), ((), ())),
                                                    preferred_element_type=F32)
        n_ref[0, h:h + 1, :] = w_c * nvec + jnp.sum(kw, axis=0, keepdims=True)
        hn = _rms(hh, nw_ref[:, h * DH_B:(h + 1) * DH_B])
        g_ref[:, h * DH_B:(h + 1) * DH_B] = _sigmoid(ob_ref[:, h * DH_B:(h + 1) * DH_B]) * hn
    mprev_ref[...] = cols[tl - 1:tl, :]

    @pl.when(c == nc - 1)
    def _():
        mfin_ref[0] = cols[tl - 1:tl, :]


def _mlstm_prompt(qk_pre, v_pre, ob, rows_pack, cols_pack, conv_w, conv_b, norm_w, nseq, seq):
    tl = MLSTM_TILE
    nc = seq // tl
    total = nseq * seq
    rmap = lambda n, c: (n * nc + c, 0)
    return pl.pallas_call(
        functools.partial(_mlstm_prompt_kernel, tl=tl),
        out_shape=[jax.ShapeDtypeStruct((total, W_B), F32),
                   jax.ShapeDtypeStruct((nseq, H_B, DH_B, DH_B), F32),
                   jax.ShapeDtypeStruct((nseq, H_B, DH_B), F32),
                   jax.ShapeDtypeStruct((nseq, 1, LANES), F32)],
        grid=(nseq, nc),
        in_specs=[pl.BlockSpec((tl, 2 * W_B), rmap),
                  pl.BlockSpec((tl, W_B), rmap),
                  pl.BlockSpec((tl, W_B), rmap),
                  pl.BlockSpec((SUBLANES, tl), lambda n, c: (0, n * nc + c)),
                  pl.BlockSpec((tl, LANES), rmap),
                  pl.BlockSpec((MLSTM_CONV, 2 * W_B), lambda n, c: (0, 0)),
                  pl.BlockSpec((1, 2 * W_B), lambda n, c: (0, 0)),
                  pl.BlockSpec((1, W_B), lambda n, c: (0, 0))],
        out_specs=[pl.BlockSpec((tl, W_B), rmap),
                   pl.BlockSpec((1, H_B, DH_B, DH_B), lambda n, c: (n, 0, 0, 0)),
                   pl.BlockSpec((1, H_B, DH_B), lambda n, c: (n, 0, 0)),
                   pl.BlockSpec((1, 1, LANES), lambda n, c: (n, 0, 0))],
        scratch_shapes=[pltpu.VMEM((tl + SUBLANES, 2 * W_B), F32), pltpu.VMEM((1, LANES), F32)],
        compiler_params=_params("arbitrary", "arbitrary"),
        name="mlstm_prompt",
    )(qk_pre, v_pre, ob, rows_pack, cols_pack, conv_w, conv_b.reshape(1, -1), norm_w.reshape(1, -1))


HEAD_DIM = 64
LOG2_E = 1.4426950408889634


def _head_masks(shape):
    head = lax.broadcasted_iota(jnp.int32, shape, 1) // HEAD_DIM
    return [head == g for g in range(shape[1] // HEAD_DIM)]


def _masked_copies(q_bf):
    head = lax.broadcasted_iota(jnp.int32, (1, q_bf.shape[1]), 1) // HEAD_DIM
    return [q_bf * jnp.where(head == g, 1.0, 0.0).astype(BF16) for g in range(q_bf.shape[1] // HEAD_DIM)]


def _stage_kv(k_ref, vt_in_ref, kb_ref, vt_ref):
    kb_ref[...] = k_ref[...].astype(BF16)
    for j in range(k_ref.shape[0] // KEY_TILE):
        vt_ref[j] = vt_in_ref[0, :, j * KEY_TILE:(j + 1) * KEY_TILE].astype(BF16)


def _flash_group(qbs, qi, kb_ref, vt_ref, tq, adjust):
    tk = KEY_TILE
    per = tq // tk
    ng = len(qbs)
    krow = lax.broadcasted_iota(jnp.int32, (tk, ng * tq), 0)
    qcol = lax.broadcasted_iota(jnp.int32, (tk, ng * tq), 1) % tq
    q_cat = jnp.concatenate(qbs, axis=0)

    width = qbs[0].shape[1]
    chunk = min(width, MXU_DEPTH)
    heads_per_chunk = chunk // HEAD_DIM

    def logits(j, diagonal):
        start = pl.multiple_of(j * tk, tk)
        kt = kb_ref[pl.ds(start, tk), :]
        s = jnp.concatenate(
            [_dot_nt(kt[:, c * chunk:(c + 1) * chunk],
                     q_cat[c * heads_per_chunk * tq:(c + 1) * heads_per_chunk * tq, c * chunk:(c + 1) * chunk])
             for c in range(width // chunk)], axis=1)
        s = adjust(s, j, start, diagonal is not None)
        if diagonal is not None:
            s = jnp.where(krow + diagonal * tk <= qcol, s, NEG_INF)
        return s

    def softmax(m, l, s):
        m_new = jnp.maximum(m, jnp.max(s, axis=0, keepdims=True))
        alpha = jnp.exp2(m - m_new)
        p = jnp.exp2(s - m_new)
        return m_new, alpha * l + jnp.sum(p, axis=0, keepdims=True), alpha, p.astype(BF16)

    def accumulate(accs, alpha, pb, j):
        return tuple(alpha[:, g * tq:(g + 1) * tq] * accs[g]
                     + _dot(vt_ref[j, g * HEAD_DIM:(g + 1) * HEAD_DIM, :], pb[:, g * tq:(g + 1) * tq])
                     for g in range(ng))

    m = jnp.full((1, ng * tq), NEG_INF, F32)
    l = jnp.zeros((1, ng * tq), F32)
    accs = tuple(jnp.zeros((HEAD_DIM, tq), F32) for _ in range(ng))
    def trip(first, count, carry, diagonal=False):
        m, l, accs = carry
        tiles = [first + u for u in range(count)]
        ss = [logits(j, u if diagonal else None) for u, j in enumerate(tiles)]
        for j, s in zip(tiles, ss):
            m, l, alpha, pb = softmax(m, l, s)
            accs = accumulate(accs, alpha, pb, j)
        return m, l, accs

    n = qi * per
    carry = trip(n, per, (m, l, accs), diagonal=True)
    carry = lax.fori_loop(0, n // KEY_UNROLL, lambda t, c: trip(t * KEY_UNROLL, KEY_UNROLL, c), carry)
    m, l, accs = lax.fori_loop((n // KEY_UNROLL) * KEY_UNROLL, n, lambda j, c: trip(j, 1, c), carry)
    return jnp.concatenate([accs[g] / l[:, g * tq:(g + 1) * tq] for g in range(ng)], axis=0)


def _store_heads(o_ref, o_t):
    for r in range(o_t.shape[0] // LANES):
        for c in range(o_t.shape[1] // LANES):
            o_ref[c * LANES:(c + 1) * LANES, r * LANES:(r + 1) * LANES] = jnp.transpose(
                o_t[r * LANES:(r + 1) * LANES, c * LANES:(c + 1) * LANES])


def _fox_prompt_kernel(q_ref, k_ref, v_ref, crow_ref, o_ref, kb_ref, vt_ref, cb_ref, *, tq):
    qi = pl.program_id(2)

    @pl.when(qi == 0)
    def _():
        _stage_kv(k_ref, v_ref, kb_ref, vt_ref)
        for g in range(cb_ref.shape[0]):
            for j in range(k_ref.shape[0] // LANES):
                sl = slice(j * LANES, (j + 1) * LANES)
                cb_ref[g, sl, :] = jnp.transpose(
                    jnp.broadcast_to(crow_ref[0, g:g + 1, sl] * LOG2_E, (LANES, LANES)))

    q = q_ref[...] * (DH_C ** -0.5 * LOG2_E)

    def adjust(s, j, start, diagonal):
        return s - jnp.concatenate([cb_ref[g, pl.ds(start, KEY_TILE), :] for g in range(cb_ref.shape[0])
                                    for _ in range(tq // LANES)], axis=1)

    qbs = _masked_copies(q.astype(BF16))
    _store_heads(o_ref, _flash_group(qbs, qi, kb_ref, vt_ref, tq, adjust))


def _fox_prompt(q, k, v, crow, nseq, seq):
    tq = FOX_TILE
    nq = seq // tq
    ng = FOX_GROUP
    width = ng * HEAD_DIM
    return pl.pallas_call(
        functools.partial(_fox_prompt_kernel, tq=tq),
        out_shape=jax.ShapeDtypeStruct(q.shape, F32),
        grid=(nseq, H_C // ng, nq),
        in_specs=[pl.BlockSpec((tq, width), lambda n, p, i: (n * nq + i, p)),
                  pl.BlockSpec((seq, width), lambda n, p, i: (n, p)),
                  pl.BlockSpec((1, width, seq), lambda n, p, i: (n, p, 0)),
                  pl.BlockSpec((1, ng, seq), lambda n, p, i: (p, 0, n))],
        out_specs=pl.BlockSpec((tq, width), lambda n, p, i: (n * nq + i, p)),
        scratch_shapes=[pltpu.VMEM((seq, width), BF16), pltpu.VMEM((seq // KEY_TILE, width, KEY_TILE), BF16),
                        pltpu.VMEM((ng, seq, LANES), F32)],
        compiler_params=_params("arbitrary", "arbitrary", "arbitrary"),
        name="fox_prompt",
    )(q, k, v, crow)


def _rank_rows(g, n_valid):
    nb = g.shape[0]
    r = lax.broadcasted_iota(jnp.int32, g.shape, 0)
    g = jnp.where(r < n_valid, g, NEG_INF)
    rank = jnp.zeros(g.shape, jnp.int32)
    for i in range(nb):
        gi = g[i:i + 1, :]
        rank = rank + ((gi > g) | ((gi == g) & (i < r))).astype(jnp.int32)
    return rank, r


def _moba_prompt_kernel(q_ref, k_ref, v_ref, o_ref, kb_ref, vt_ref, kmean_ref, kmh_ref, kml_ref, sel_ref, *, tq, nb):
    qi = pl.program_id(2)
    ng, nbp = sel_ref.shape[0], sel_ref.shape[1]

    @pl.when(qi == 0)
    def _():
        _stage_kv(k_ref, v_ref, kb_ref, vt_ref)
        kmean_ref[...] = jnp.zeros_like(kmean_ref)
        for j in range(nb):
            kmean_ref[j:j + 1, :] = jnp.mean(k_ref[j * MOBA_BLOCK:(j + 1) * MOBA_BLOCK, :], axis=0, keepdims=True)
        kmean = kmean_ref[...]
        for g, mask in enumerate(_head_masks(kmean.shape)):
            hi, lo = _split_bf16(jnp.where(mask, kmean, 0.0))
            kmh_ref[g * nbp:(g + 1) * nbp, :] = hi
            kml_ref[g * nbp:(g + 1) * nbp, :] = lo

    q = q_ref[...]
    q_hi, q_lo = _split_bf16(q)
    km_hi, km_lo = kmh_ref[...], kml_ref[...]
    gates = _dot_nt(km_hi, q_hi) + _dot_nt(km_lo, q_hi) + _dot_nt(km_hi, q_lo)
    for g in range(ng):
        rank, r = _rank_rows(gates[g * nbp:(g + 1) * nbp, :], qi)
        sel_ref[g] = ((rank < MOBA_TOPK) & (r < qi)).astype(F32)
    qbs = _masked_copies((q * (DH_A ** -0.5 * LOG2_E)).astype(BF16))

    def adjust(s, j, start, diagonal):
        if diagonal:
            return s
        blk = (j * KEY_TILE) // MOBA_BLOCK
        chosen = jnp.concatenate([sel_ref[g, pl.ds(blk, 1), :] for g in range(sel_ref.shape[0])], axis=1)
        return jnp.where(chosen > 0.5, s, NEG_INF)

    _store_heads(o_ref, _flash_group(qbs, qi, kb_ref, vt_ref, tq, adjust))


def _moba_prompt(q, k, v, nseq, seq):
    tq = ATT_TILE
    assert tq == MOBA_BLOCK and seq % MOBA_BLOCK == 0 and seq // MOBA_BLOCK >= MOBA_TOPK
    nq = seq // tq
    nb = seq // MOBA_BLOCK
    nbp = -(-nb // SUBLANES) * SUBLANES
    ng = MOBA_GROUP
    width = ng * HEAD_DIM
    return pl.pallas_call(
        functools.partial(_moba_prompt_kernel, tq=tq, nb=nb),
        out_shape=jax.ShapeDtypeStruct(q.shape, F32),
        grid=(nseq, H_A // ng, nq),
        in_specs=[pl.BlockSpec((tq, width), lambda n, p, i: (n * nq + i, p)),
                  pl.BlockSpec((seq, width), lambda n, p, i: (n, p)),
                  pl.BlockSpec((1, width, seq), lambda n, p, i: (n, p, 0))],
        out_specs=pl.BlockSpec((tq, width), lambda n, p, i: (n * nq + i, p)),
        scratch_shapes=[pltpu.VMEM((seq, width), BF16), pltpu.VMEM((seq // KEY_TILE, width, KEY_TILE), BF16),
                        pltpu.VMEM((nbp, width), F32), pltpu.VMEM((ng * nbp, width), BF16),
                        pltpu.VMEM((ng * nbp, width), BF16), pltpu.VMEM((ng, nbp, tq), F32)],
        compiler_params=_params("arbitrary", "arbitrary", "arbitrary"),
        name="moba_prompt",
    )(q, k, v)


def _column_replicated(row):
    return jnp.concatenate(
        [jnp.transpose(jnp.broadcast_to(row[:, c * LANES:(c + 1) * LANES], (LANES, LANES)))
         for c in range(row.shape[1] // LANES)], axis=0)


def _head_sums(x, dh):
    return jnp.concatenate([jnp.sum(x[h * dh:(h + 1) * dh, :], axis=0, keepdims=True)
                            for h in range(x.shape[0] // dh)], axis=0)


def _moba_sample_select_kernel(pt_ref, q_ref, *refs, group, nblk):
    k_refs = refs[:group]
    idx_ref = refs[group]
    qcol_ref, gate_ref = refs[group + 1:]
    g = pl.program_id(1)

    @pl.when(g == 0)
    def _():
        qcol_ref[...] = _column_replicated(q_ref[0])
        gate_ref[...] = jnp.zeros_like(gate_ref)

    lane = lax.broadcasted_iota(jnp.int32, gate_ref.shape, 1)
    pages_per_block = MOBA_BLOCK // PAGE_SIZE
    qcol = qcol_ref[...]
    upd = jnp.zeros(gate_ref.shape, F32)
    for i in range(group):
        qk = _head_sums(k_refs[i][0] * qcol, DH_A)
        blk = (g * group + i) // pages_per_block
        val = jnp.sum(qk, axis=1, keepdims=True) * (1.0 / (PAGE_SIZE * pages_per_block))
        upd = upd + jnp.where(lane == blk, val, 0.0)
    gate_ref[...] += upd

    @pl.when(g == pl.num_programs(1) - 1)
    def _():
        gate = jnp.where(lane < nblk, gate_ref[...], NEG_INF)
        rank = jnp.zeros(gate.shape, jnp.int32)
        for i in range(nblk):
            gi = gate[:, i:i + 1]
            rank = rank + ((gi > gate) | ((gi == gate) & (i < lane))).astype(jnp.int32)
        out = jnp.zeros(gate.shape, F32)
        for t in range(MOBA_TOPK):
            it = jnp.sum(jnp.where((rank == t) & (lane < nblk), lane.astype(F32), 0.0), axis=1, keepdims=True)
            out = jnp.where(lane == t, it, out)
        idx_ref[0] = out.astype(jnp.int32)


def _moba_sample_select(q, pool_kt, page_table, nblk):
    nb = q.shape[0]
    npg = nblk * (MOBA_BLOCK // PAGE_SIZE)
    group = SELECT_PAGES if npg % SELECT_PAGES == 0 else PAGES_PER_STEP
    assert npg % group == 0 and MOBA_TOPK <= nblk <= LANES
    in_specs = [pl.BlockSpec((1, 1, W_A), lambda n, g, pt: (n, 0, 0))]
    for i in range(group):
        in_specs.append(pl.BlockSpec((1, W_A, PAGE_SIZE), lambda n, g, pt, i=i: (pt[n, g * group + i], 0, 0)))
    return pl.pallas_call(
        functools.partial(_moba_sample_select_kernel, group=group, nblk=nblk),
        out_shape=jax.ShapeDtypeStruct((nb, H_A, LANES), jnp.int32),
        grid_spec=pltpu.PrefetchScalarGridSpec(
            num_scalar_prefetch=1,
            grid=(nb, npg // group),
            in_specs=in_specs,
            out_specs=pl.BlockSpec((1, H_A, LANES), lambda n, g, pt: (n, 0, 0)),
            scratch_shapes=[pltpu.VMEM((W_A, LANES), F32), pltpu.VMEM((H_A, LANES), F32)]),
        compiler_params=_params("arbitrary", "arbitrary"),
        name="moba_sample_select",
    )(page_table, q.reshape(nb, 1, W_A), *([pool_kt] * group))


def _moba_sample_attend_kernel(pt_ref, idx_ref, q_ref, kn_ref, vn_ref, *refs, n_sel, n_own, qblk):
    n_pages = n_sel + n_own
    k_refs = refs[:H_A * n_pages]
    v_refs = refs[H_A * n_pages:2 * H_A * n_pages]
    o_ref = refs[2 * H_A * n_pages]
    n = pl.program_id(0)
    pages_per_block = MOBA_BLOCK // PAGE_SIZE
    for h in range(H_A):
        q = q_ref[0, h:h + 1, :] * (DH_A ** -0.5)
        qcol = _column_replicated(q)[0:DH_A]
        logits = []
        for i in range(n_pages):
            s = jnp.sum(k_refs[h * n_pages + i][0] * qcol, axis=0, keepdims=True)
            if i < n_sel:
                valid = idx_ref[n, h, i // pages_per_block] < qblk
                s = jnp.where(valid, s, NEG_INF)
            logits.append(s)
        s_new = jnp.sum(q * kn_ref[0, h:h + 1, :], axis=1, keepdims=True)
        m = s_new
        for s in logits:
            m = jnp.maximum(m, jnp.max(s, axis=1, keepdims=True))
        p_new = jnp.exp(s_new - m)
        l = p_new
        acc_t = jnp.zeros((DH_A, PAGE_SIZE), F32)
        for i, s in enumerate(logits):
            p = jnp.exp(s - m)
            l = l + jnp.sum(p, axis=1, keepdims=True)
            acc_t = acc_t + v_refs[h * n_pages + i][0] * p
        acc = jnp.sum(jnp.transpose(jnp.concatenate([acc_t, jnp.zeros((LANES - DH_A, PAGE_SIZE), F32)], axis=0)),
                      axis=0, keepdims=True)
        o_ref[0, h:h + 1, :] = (acc + p_new * vn_ref[0, h:h + 1, :]) / l


def _moba_sample_attend(q, k_new, v_new, pool_kt, pool_vt, page_table, idx):
    nb = q.shape[0]
    n_pages_total = page_table.shape[1]
    pages_per_block = MOBA_BLOCK // PAGE_SIZE
    past = n_pages_total * PAGE_SIZE
    qblk = past // MOBA_BLOCK
    n_sel = MOBA_TOPK * pages_per_block
    n_own = 1 if (past - PAGE_SIZE) // MOBA_BLOCK == qblk else 0

    def sel_map(h, i):
        def index_map(n, pt, ix):
            lpage = jnp.minimum(ix[n, h, i // pages_per_block] * pages_per_block + i % pages_per_block,
                                n_pages_total - 1)
            return (pt[n, lpage], h, 0)
        return index_map

    def own_map(h):
        return lambda n, pt, ix: (pt[n, n_pages_total - 1], h, 0)

    page_specs = []
    for h in range(H_A):
        page_specs += [pl.BlockSpec((1, DH_A, PAGE_SIZE), sel_map(h, i)) for i in range(n_sel)]
        page_specs += [pl.BlockSpec((1, DH_A, PAGE_SIZE), own_map(h))] * n_own
    tok_spec = pl.BlockSpec((1, H_A, LANES), lambda n, pt, ix: (n, 0, 0))
    tok = lambda a: _pad_lanes(a.reshape(nb, H_A, DH_A))
    n_pg = H_A * (n_sel + n_own)
    att = pl.pallas_call(
        functools.partial(_moba_sample_attend_kernel, n_sel=n_sel, n_own=n_own, qblk=qblk),
        out_shape=jax.ShapeDtypeStruct((nb, H_A, LANES), F32),
        grid_spec=pltpu.PrefetchScalarGridSpec(
            num_scalar_prefetch=2,
            grid=(nb,),
            in_specs=[tok_spec, tok_spec, tok_spec] + page_specs + page_specs,
            out_specs=tok_spec),
        compiler_params=_params("arbitrary"),
        name="moba_sample_attend",
    )(page_table, idx, tok(q), tok(k_new), tok(v_new), *([pool_kt] * n_pg), *([pool_vt] * n_pg))
    return att[:, :, :DH_A].reshape(nb, W_A)


def _mlstm_sample_kernel(qk_ref, st_ref, v_ref, ob_ref, gate_ref, c_ref, n_ref, m_ref, cw_ref, cb_ref, nw_ref,
                         g_ref, cn_ref, nn_ref, mn_ref):
    cw = cw_ref[...]
    st = st_ref[0]
    y = cb_ref[...] + cw[MLSTM_CONV - 1:MLSTM_CONV] * qk_ref[0]
    for t in range(MLSTM_CONV - 1):
        y = y + cw[t:t + 1] * st[t:t + 1]
    qk = y * _sigmoid(y)
    gates = gate_ref[0]
    m_in = m_ref[0]
    m_out = jnp.zeros((1, LANES), F32)
    lane = lax.broadcasted_iota(jnp.int32, (1, LANES), 1)
    for h in range(H_B):
        q = qk[:, h * DH_B:(h + 1) * DH_B]
        k = qk[:, W_B + h * DH_B:W_B + (h + 1) * DH_B] * (DH_B ** -0.5)
        v = v_ref[0][:, h * DH_B:(h + 1) * DH_B]
        ig = gates[:, h:h + 1]
        lf = gates[:, H_B + h:H_B + h + 1]
        m_old = m_in[:, h:h + 1]
        m_row = jnp.maximum(ig, lf + m_old)
        w_inter = jnp.exp(lf + m_old - m_row)
        w_s = jnp.exp(ig - m_row)
        s = jnp.sum(q * k, axis=1, keepdims=True) * w_s
        cmat = c_ref[0, h]
        nvec = n_ref[0, h:h + 1, :]
        qc = _dot(jnp.broadcast_to(q, (SUBLANES, DH_B)).astype(BF16), cmat.astype(BF16))[0:1]
        num = s * v + w_inter * qc
        den = s + w_inter * jnp.sum(q * nvec, axis=1, keepdims=True)
        hh = num / jnp.maximum(jnp.abs(den), jnp.exp(-m_row))
        k_col = jnp.transpose(jnp.broadcast_to(k, (DH_B, DH_B)))
        cn_ref[0, h] = w_inter * cmat + w_s * (k_col * v)
        nn_ref[0, h:h + 1, :] = w_inter * nvec + w_s * k
        m_out = jnp.where(lane == h, m_row, m_out)
        hn = _rms(hh, nw_ref[:, h * DH_B:(h + 1) * DH_B])
        g_ref[0, :, h * DH_B:(h + 1) * DH_B] = _sigmoid(ob_ref[0][:, h * DH_B:(h + 1) * DH_B]) * hn
    mn_ref[0] = m_out


def _mlstm_sample(qk_pre, conv_state, v_pre, ob, gates, c, nvec, m, conv_w, conv_b, norm_w):
    nb = qk_pre.shape[0]
    tok = lambda a: a.reshape(nb, 1, a.shape[-1])
    tspec = lambda w: pl.BlockSpec((1, 1, w), lambda n: (n, 0, 0))
    full = lambda shape: pl.BlockSpec(shape, lambda n: (0,) * len(shape))
    g, cn, nn, mn = pl.pallas_call(
        _mlstm_sample_kernel,
        out_shape=[jax.ShapeDtypeStruct((nb, 1, W_B), F32),
                   jax.ShapeDtypeStruct((nb, H_B, DH_B, DH_B), F32),
                   jax.ShapeDtypeStruct((nb, H_B, DH_B), F32),
                   jax.ShapeDtypeStruct((nb, 1, LANES), F32)],
        grid=(nb,),
        in_specs=[tspec(2 * W_B),
                  pl.BlockSpec((1, MLSTM_CONV - 1, 2 * W_B), lambda n: (n, 0, 0)),
                  tspec(W_B), tspec(W_B), tspec(LANES),
                  pl.BlockSpec((1, H_B, DH_B, DH_B), lambda n: (n, 0, 0, 0)),
                  pl.BlockSpec((1, H_B, DH_B), lambda n: (n, 0, 0)),
                  tspec(LANES),
                  full((MLSTM_CONV, 2 * W_B)), full((1, 2 * W_B)), full((1, W_B))],
        out_specs=[tspec(W_B),
                   pl.BlockSpec((1, H_B, DH_B, DH_B), lambda n: (n, 0, 0, 0)),
                   pl.BlockSpec((1, H_B, DH_B), lambda n: (n, 0, 0)),
                   tspec(LANES)],
        compiler_params=_params("arbitrary"),
        name="mlstm_sample",
    )(tok(qk_pre), conv_state, tok(v_pre), tok(ob), tok(gates), c, nvec, tok(_pad_lanes(m)),
      conv_w, conv_b.reshape(1, -1), norm_w.reshape(1, -1))
    return g.reshape(nb, W_B), cn, nn, mn[:, 0, :H_B]


def _fox_sample_kernel(pt_ref, q_ref, kn_ref, vn_ref, lfn_ref, *refs, group):
    k_refs = refs[:group]
    v_refs = refs[group:2 * group]
    lf_refs = refs[2 * group:3 * group]
    o_ref = refs[3 * group]
    qcol_ref, m_ref, l_ref, run_ref, acc_ref = refs[3 * group + 1:]
    g = pl.program_id(1)

    @pl.when(g == 0)
    def _():
        qcol_ref[...] = _column_replicated(q_ref[0] * (DH_C ** -0.5))
        m_ref[...] = jnp.full(m_ref.shape, NEG_INF, F32)
        l_ref[...] = jnp.zeros_like(l_ref)
        run_ref[...] = jnp.zeros_like(run_ref)
        acc_ref[...] = jnp.zeros_like(acc_ref)

    qcol = qcol_ref[...]
    run = run_ref[...]
    logits = []
    for i in range(group):
        qk = _head_sums(k_refs[i][0] * qcol, DH_C)
        cum = run + _scan_lanes(lf_refs[i][0], jnp.add, 0.0)
        run = cum[:, PAGE_SIZE - 1:PAGE_SIZE]
        logits.append(qk - cum)
    s = jnp.concatenate(logits, axis=1)
    m = m_ref[...]
    m_new = jnp.maximum(m, jnp.max(s, axis=1, keepdims=True))
    alpha = jnp.exp(m - m_new)
    p = jnp.exp(s - m_new)
    l = alpha * l_ref[...] + jnp.sum(p, axis=1, keepdims=True)
    m_ref[...], l_ref[...], run_ref[...] = m_new, l, run
    for h in range(H_C):
        rows = slice(h * DH_C, (h + 1) * DH_C)
        a = acc_ref[rows, :] * alpha[h:h + 1, :]
        for i in range(group):
            a = a + v_refs[i][0, rows, :] * p[h:h + 1, i * PAGE_SIZE:(i + 1) * PAGE_SIZE]
        acc_ref[rows, :] = a

    @pl.when(g == pl.num_programs(1) - 1)
    def _():
        head = lax.broadcasted_iota(jnp.int32, (H_C, W_C), 0)
        mine = (lax.broadcasted_iota(jnp.int32, (H_C, W_C), 1) // DH_C) == head
        q_bd = jnp.where(mine, jnp.broadcast_to(q_ref[0], (H_C, W_C)), 0.0) * (DH_C ** -0.5)
        hrow = lax.broadcasted_iota(jnp.int32, (H_C, LANES), 0)
        hlane = lax.broadcasted_iota(jnp.int32, (H_C, LANES), 1)
        lf_new = jnp.sum(jnp.where(hrow == hlane, jnp.broadcast_to(lfn_ref[0], (H_C, LANES)), 0.0),
                         axis=1, keepdims=True)
        s_new = jnp.sum(q_bd * kn_ref[0], axis=1, keepdims=True) - (run + lf_new)
        m_f = jnp.maximum(m_new, s_new)
        alpha_f = jnp.exp(m_new - m_f)
        p_new = jnp.exp(s_new - m_f)
        l_f = alpha_f * l + p_new
        past = jnp.concatenate([jnp.sum(jnp.transpose(acc_ref[c * LANES:(c + 1) * LANES, :]), axis=0, keepdims=True)
                                for c in range(W_C // LANES)], axis=1)
        per_lane = lambda col: jnp.sum(jnp.where(mine, col, 0.0), axis=0, keepdims=True)
        o_ref[0] = (past * per_lane(alpha_f) + per_lane(p_new) * vn_ref[0]) / per_lane(l_f)


def _fox_sample(q, k_new, v_new, lf_new, pool_k, pool_v, pool_lf_t, page_table):
    nb = q.shape[0]
    n_pages = page_table.shape[1]
    group = PAGES_PER_STEP
    assert n_pages % group == 0
    tok_spec = pl.BlockSpec((1, 1, W_C), lambda n, g, pt: (n, 0, 0))
    tok = lambda a: a.reshape(nb, 1, a.shape[-1])
    page = lambda i, shape: pl.BlockSpec(shape, lambda n, g, pt: (pt[n, g * group + i], 0, 0))
    in_specs = [tok_spec, tok_spec, tok_spec, pl.BlockSpec((1, 1, LANES), lambda n, g, pt: (n, 0, 0))]
    in_specs += [page(i, (1, W_C, PAGE_SIZE)) for i in range(group)]
    in_specs += [page(i, (1, W_C, PAGE_SIZE)) for i in range(group)]
    in_specs += [page(i, (1, H_C, PAGE_SIZE)) for i in range(group)]
    return pl.pallas_call(
        functools.partial(_fox_sample_kernel, group=group),
        out_shape=jax.ShapeDtypeStruct((nb, 1, W_C), F32),
        grid_spec=pltpu.PrefetchScalarGridSpec(
            num_scalar_prefetch=1,
            grid=(nb, n_pages // group),
            in_specs=in_specs,
            out_specs=tok_spec,
            scratch_shapes=[pltpu.VMEM((W_C, LANES), F32), pltpu.VMEM((H_C, 1), F32), pltpu.VMEM((H_C, 1), F32),
                            pltpu.VMEM((H_C, 1), F32), pltpu.VMEM((W_C, PAGE_SIZE), F32)]),
        compiler_params=_params("arbitrary", "arbitrary"),
        name="fox_sample",
    )(page_table, tok(q), tok(k_new), tok(v_new), tok(lf_new),
      *([pool_k] * group), *([pool_v] * group), *([pool_lf_t] * group)).reshape(nb, W_C)


EVEN_SEGS = ((0, W_A), (W_A, W_A), (2 * W_A, W_A), (3 * W_A, 2 * W_B), (3 * W_A + 2 * W_B, W_B),
             (3 * W_A + 3 * W_B, W_B), (3 * W_A + 4 * W_B, LANES))
ODD_SEGS = ((0, W_C), (W_C, W_C), (2 * W_C, W_C), (3 * W_C, LANES))


def _even_weights(w_in, b_i, b_f):
    main = 3 * W_A + 4 * W_B
    w_pad = jnp.concatenate([w_in[:, :main], _pad_lanes(w_in[:, main:])], axis=1)
    w_bf = w_pad.astype(BF16)
    wq = w_in[:, :W_A]
    w_lo = (wq - wq.astype(BF16).astype(F32)).astype(BF16)
    gate_bias = _pad_lanes(jnp.concatenate([b_i, b_f]).reshape(1, -1))
    return w_bf, w_lo, gate_bias


def _odd_weights(w_in, b_f):
    w_pad = jnp.concatenate([w_in[:, :3 * W_C], _pad_lanes(w_in[:, 3 * W_C:])], axis=1)
    return w_pad.astype(BF16), _pad_lanes(b_f.reshape(1, -1))


def _feature_major(pool):
    n_pool, page, heads, dh = pool.shape
    return jnp.transpose(pool, (0, 2, 3, 1)).reshape(n_pool, heads * dh, page)


def _time_minor_to_cache(a_t, heads):
    n, width, seq = a_t.shape
    return jnp.transpose(a_t.reshape(n, heads, width // heads, seq), (0, 3, 1, 2))


def _rows_to_sublanes(a):
    return jnp.pad(a.T, ((0, SUBLANES - a.shape[1]), (0, 0)))


def kernel(x_prompt, x_sample, cache_moba_k, cache_moba_v, state_mlstm_c, state_mlstm_n, state_mlstm_m,
           state_mlstm_conv, cache_fox_k, cache_fox_v, cache_fox_logf, state_ffn_conv, page_table, norm_w,
           even_w_in, mlstm_conv_w, mlstm_conv_b, mlstm_b_i, mlstm_b_f, mlstm_norm_w, even_w_out, fox_w_in,
           fox_b_f, fox_w_out, ffn_w_in, ffn_conv_w, ffn_conv_b, ffn_w_out):
    bp, sp, d = x_prompt.shape
    bs, ss, _ = x_sample.shape
    assert ss == 1 and d == D_MODEL
    depth = norm_w.shape[0]
    n_pages = page_table.shape[1]
    tp = bp * sp
    yp = x_prompt.reshape(tp, d)
    ys = x_sample.reshape(bs, d)
    outs = {name: [] for name in (
        "mk_p", "mv_p", "mk_s", "mv_s", "mc_p", "mn_p", "mm_p", "mconv_p", "mc_s", "mn_s", "mm_s", "mconv_s",
        "fk_p", "fv_p", "fl_p", "fk_s", "fv_s", "fl_s", "ff_p", "ff_s")}
    for layer in range(depth):
        nw = norm_w[layer]
        if layer % 2 == 0:
            e = layer // 2
            w_bf, w_lo, gate_bias = _even_weights(even_w_in[e], mlstm_b_i[e], mlstm_b_f[e])
            w_out_bf = even_w_out[e].astype(BF16)
            proj = lambda x, tm, **kw: _rms_proj(x, nw[0], w_bf, w_lo, gate_bias, EVEN_SEGS, 1, (H_B, 2 * H_B), tm,
                                                 **kw)
            qa, ka, ka_t, va_t, qkb, vb, ob, gates = proj(yp, ROW_TILE, cols={1: True, 2: False}, seq_rows=sp)
            att = _moba_prompt(qa, ka, va_t, bp, sp)
            b_t, m_t, mrow_t = _mlstm_gate_scan(_rows_to_sublanes(gates[:, :H_B]),
                                                _rows_to_sublanes(gates[:, H_B:2 * H_B]), sp)
            cols = _pad_lanes(jnp.concatenate([b_t[:H_B].T, m_t[:H_B].T, mrow_t[:H_B].T], axis=1))
            gated, cc, nn, mfin = _mlstm_prompt(qkb, vb, ob, b_t, cols, mlstm_conv_w[e], mlstm_conv_b[e],
                                                mlstm_norm_w[e], bp, sp)
            yp = _proj_res([att, gated], [w_out_bf[:W_A], w_out_bf[W_A:]], yp, nw[1], FFN_ROWS)
            outs["mk_p"].append(_time_minor_to_cache(ka_t, H_A))
            outs["mv_p"].append(_time_minor_to_cache(va_t, H_A))
            outs["mc_p"].append(cc)
            outs["mn_p"].append(nn)
            outs["mm_p"].append(mfin[:, 0, 2 * H_B:3 * H_B])
            outs["mconv_p"].append(qkb.reshape(bp, sp, 2 * W_B)[:, sp - (MLSTM_CONV - 1):])
            qa, ka, va, qkb, vb, ob, gates = proj(ys, bs)
            pool_kt = _feature_major(cache_moba_k[e])
            pool_vt = _feature_major(cache_moba_v[e])
            nblk = (n_pages * PAGE_SIZE) // MOBA_BLOCK
            idx = _moba_sample_select(qa, pool_kt, page_table, nblk)[:, :, :MOBA_TOPK]
            att = _moba_sample_attend(qa, ka, va, pool_kt, pool_vt, page_table, idx)
            gated, cc, nn, mm = _mlstm_sample(qkb, state_mlstm_conv[e], vb, ob, gates, state_mlstm_c[e],
                                              state_mlstm_n[e], state_mlstm_m[e], mlstm_conv_w[e],
                                              mlstm_conv_b[e], mlstm_norm_w[e])
            ys = _proj_res([att, gated], [w_out_bf[:W_A], w_out_bf[W_A:]], ys, nw[1], bs)
            outs["mk_s"].append(ka.reshape(bs, ss, H_A, DH_A))
            outs["mv_s"].append(va.reshape(bs, ss, H_A, DH_A))
            outs["mc_s"].append(cc)
            outs["mn_s"].append(nn)
            outs["mm_s"].append(mm)
            outs["mconv_s"].append(jnp.concatenate([state_mlstm_conv[e][:, 1:], qkb[:, None, :]], axis=1))
        else:
            o = layer // 2
            w_bf, gate_bias = _odd_weights(fox_w_in[o], fox_b_f[o])
            w_out_bf = fox_w_out[o].astype(BF16)
            proj = lambda x, tm, **kw: _rms_proj(x, nw[0], w_bf, None, gate_bias, ODD_SEGS, 0, (0, H_C), tm, **kw)
            q, k, k_t, v_t, lf = proj(yp, ROW_TILE, cols={1: True, 2: False}, seq_rows=sp)
            lf_t = lf[:, :H_C].T
            crow = _cumsum_time(lf_t, sp).reshape(H_C // FOX_GROUP, FOX_GROUP, tp)
            att = _fox_prompt(q, k, v_t, crow, bp, sp)
            yp = _proj_res([att], [w_out_bf], yp, nw[1], FFN_ROWS)
            outs["fk_p"].append(_time_minor_to_cache(k_t, H_C))
            outs["fv_p"].append(_time_minor_to_cache(v_t, H_C))
            outs["fl_p"].append(lf[:, :H_C].reshape(bp, sp, H_C))
            q, k, v, lf = proj(ys, bs)
            pool_lf_t = jnp.transpose(cache_fox_logf[o], (0, 2, 1))
            att = _fox_sample(q, k, v, lf, _feature_major(cache_fox_k[o]), _feature_major(cache_fox_v[o]),
                              pool_lf_t, page_table)
            ys = _proj_res([att], [w_out_bf], ys, nw[1], bs)
            outs["fk_s"].append(k.reshape(bs, ss, H_C, DH_C))
            outs["fv_s"].append(v.reshape(bs, ss, H_C, DH_C))
            outs["fl_s"].append(lf[:, :H_C].reshape(bs, ss, H_C))
        w_in_bf = ffn_w_in[layer].astype(BF16)
        w_o_bf = ffn_w_out[layer].astype(BF16)
        yp, buf_p = _ffn(yp, nw[2], w_in_bf, ffn_conv_w[layer], ffn_conv_b[layer], w_o_bf, nw[3], FFN_ROWS, sp)
        ys, u_s = _ffn(ys, nw[2], w_in_bf, ffn_conv_w[layer], ffn_conv_b[layer], w_o_bf, nw[3], bs, 1,
                       state=state_ffn_conv[layer])
        outs["ff_p"].append(buf_p)
        outs["ff_s"].append(jnp.concatenate([state_ffn_conv[layer][:, 1:], u_s[:, None, :]], axis=1))
    st = {name: jnp.stack(vals) for name, vals in outs.items()}
    return (yp.reshape(bp, sp, d), ys.reshape(bs, ss, d), st["mk_p"], st["mv_p"], st["mk_s"], st["mv_s"],
            st["mc_p"], st["mn_p"], st["mm_p"], st["mconv_p"], st["mc_s"], st["mn_s"], st["mm_s"], st["mconv_s"],
            st["fk_p"], st["fv_p"], st["fl_p"], st["fk_s"], st["fv_s"], st["fl_s"], st["ff_p"], st["ff_s"])
```

```python
import functools

import jax
import jax.numpy as jnp
from jax import lax
from jax.experimental import pallas as pl
from jax.experimental.pallas import tpu as pltpu

F32 = jnp.float32
BF16 = jnp.bfloat16
NEG_INF = float("-inf")

D_MODEL = 1024
PAGE_SIZE = 128
H_A, DH_A = 8, 64
W_A = H_A * DH_A
MOBA_BLOCK = 256
MOBA_TOPK = 3
H_B, DH_B = 4, 128
W_B = H_B * DH_B
MLSTM_CONV = 4
H_C, DH_C = 16, 64
W_C = H_C * DH_C
D_FF = 2816
FFN_CONV = 3
RMS_EPS = 1e-6

MXU_DEPTH = 256
LANES = 128
SUBLANES = 8
VMEM_LIMIT = 56 * 1024 * 1024

ROW_TILE = 256
ATT_TILE = 256
FOX_TILE = 512
KEY_TILE = 256
KEY_UNROLL = (4, 2, 1)
MOBA_GROUP = 8
FOX_GROUP = 4
MLSTM_TILE = 512
FFN_ROWS = 512
FFN_COLS = 1408
FFN_SLICE = 256
PAGES_PER_STEP = 16
SELECT_PAGES = 16


def _params(*sem):
    return pltpu.CompilerParams(dimension_semantics=sem, vmem_limit_bytes=VMEM_LIMIT)


def _rms(x, w):
    return x * lax.rsqrt(jnp.mean(x * x, axis=-1, keepdims=True) + RMS_EPS) * w


def _sigmoid(x):
    return 1.0 / (1.0 + jnp.exp(-x))


def _log_sigmoid(x):
    return -(jnp.maximum(-x, 0.0) + jnp.log1p(jnp.exp(-jnp.abs(x))))


def _gelu_tanh(x):
    c = 0.7978845608028654
    return 0.5 * x * (1.0 + jnp.tanh(c * (x + 0.044715 * (x * x * x))))


def _dot(a, b):
    return jnp.dot(a, b, preferred_element_type=F32)


def _dot_nt(a, b):
    return lax.dot_general(a, b, (((1,), (1,)), ((), ())), preferred_element_type=F32)


def _split_bf16(x):
    hi = x.astype(BF16)
    lo = (x - hi.astype(F32)).astype(BF16)
    return hi, lo


def _pad_lanes(a, width=LANES):
    return jnp.pad(a, ((0, 0),) * (a.ndim - 1) + ((0, width - a.shape[-1]),))


def _rms_proj_kernel(*refs, segs, n_hi, gate_lf, cols):
    x_ref, nw_ref, w_ref = refs[:3]
    pos = 3
    wlo_ref = None
    if n_hi:
        wlo_ref = refs[pos]
        pos += 1
    gb_ref = refs[pos]
    out_refs = list(refs[pos + 1:])
    h = _rms(x_ref[...], nw_ref[...])
    hb, hl = _split_bf16(h)
    lo_start = 0
    for i, (start, width) in enumerate(segs):
        w = w_ref[:, start:start + width]
        z = _dot(hb, w)
        if i < n_hi:
            z = z + _dot(hl, w) + _dot(hb, wlo_ref[:, lo_start:lo_start + width])
            lo_start += width
        if i == len(segs) - 1:
            z = z + gb_ref[...]
            lane = lax.broadcasted_iota(jnp.int32, z.shape, 1)
            z = jnp.where((lane >= gate_lf[0]) & (lane < gate_lf[1]), _log_sigmoid(z), z)
        if cols.get(i, True):
            out_refs.pop(0)[...] = z
        if i in cols:
            zt_ref = out_refs.pop(0)
            for r in range(z.shape[0] // LANES):
                for c in range(width // LANES):
                    zt_ref[0, c * LANES:(c + 1) * LANES, r * LANES:(r + 1) * LANES] = jnp.transpose(
                        z[r * LANES:(r + 1) * LANES, c * LANES:(c + 1) * LANES])


def _rms_proj(x, nw, w_bf, w_lo, gate_bias, segs, n_hi, gate_lf, tm, cols=None, seq_rows=None):
    rows, d = x.shape
    ncols = w_bf.shape[1]
    cols = cols or {}
    in_specs = [pl.BlockSpec((tm, d), lambda i: (i, 0)),
                pl.BlockSpec((1, d), lambda i: (0, 0)),
                pl.BlockSpec((d, ncols), lambda i: (0, 0))]
    args = [x, nw.reshape(1, d), w_bf]
    if n_hi:
        in_specs.append(pl.BlockSpec(w_lo.shape, lambda i: (0, 0)))
        args.append(w_lo)
    in_specs.append(pl.BlockSpec((1, LANES), lambda i: (0, 0)))
    args.append(gate_bias)
    out_shape, out_specs = [], []
    for i, (_, wd) in enumerate(segs):
        if cols.get(i, True):
            out_shape.append(jax.ShapeDtypeStruct((rows, wd), F32))
            out_specs.append(pl.BlockSpec((tm, wd), lambda i: (i, 0)))
        if i in cols:
            tps = seq_rows // tm
            out_shape.append(jax.ShapeDtypeStruct((rows // seq_rows, wd, seq_rows), F32))
            out_specs.append(pl.BlockSpec((1, wd, tm), lambda i: (i // tps, 0, i % tps)))
    return pl.pallas_call(
        functools.partial(_rms_proj_kernel, segs=segs, n_hi=n_hi, gate_lf=gate_lf, cols=cols),
        out_shape=out_shape,
        grid=(rows // tm,),
        in_specs=in_specs,
        out_specs=out_specs,
        compiler_params=_params("arbitrary"),
        name="rms_proj",
    )(*args)


def _proj_res_kernel(*refs, n_in):
    a_refs = refs[:n_in]
    w_refs = refs[n_in:2 * n_in]
    x_ref, nw_ref, o_ref = refs[2 * n_in:]
    acc = None
    for a_ref, w_ref in zip(a_refs, w_refs):
        z = _dot(a_ref[...].astype(BF16), w_ref[...])
        acc = z if acc is None else acc + z
    o_ref[...] = x_ref[...] + _rms(acc, nw_ref[...])


def _proj_res(acts, ws, x, nw, tm):
    rows, d = x.shape
    n_in = len(acts)
    in_specs = ([pl.BlockSpec((tm, a.shape[1]), lambda i: (i, 0)) for a in acts]
                + [pl.BlockSpec(w.shape, lambda i: (0, 0)) for w in ws]
                + [pl.BlockSpec((tm, d), lambda i: (i, 0)), pl.BlockSpec((1, d), lambda i: (0, 0))])
    return pl.pallas_call(
        functools.partial(_proj_res_kernel, n_in=n_in),
        out_shape=jax.ShapeDtypeStruct((rows, d), F32),
        grid=(rows // tm,),
        in_specs=in_specs,
        out_specs=pl.BlockSpec((tm, d), lambda i: (i, 0)),
        compiler_params=_params("arbitrary"),
        name="proj_res",
    )(*acts, *ws, x, nw.reshape(1, d))


def _ffn_kernel(*refs, tm, tf, nj, tiles_per_seq, stateful):
    (x_ref, nw_in_ref, wa_ref, wb_ref, cwa_ref, cwb_ref, cba_ref, cbb_ref, wo_ref, nw_out_ref) = refs[:10]
    pos = 10
    if stateful:
        s0a_ref, s0b_ref, s1a_ref, s1b_ref = refs[pos:pos + 4]
        pos += 4
    y_ref, oa_ref, ob_ref = refs[pos:pos + 3]
    pos += 3
    hb_ref, acc_ref = refs[pos:pos + 2]
    pos += 2
    if not stateful:
        ubuf_ref, carry_ref = refs[pos:pos + 2]
    i = pl.program_id(0)
    j = pl.program_id(1)

    @pl.when(j == 0)
    def _():
        hb_ref[...] = _rms(x_ref[...], nw_in_ref[...]).astype(BF16)
        acc_ref[...] = jnp.zeros_like(acc_ref)

    hb = hb_ref[...]
    if not stateful:
        first = i % tiles_per_seq == 0

        @pl.when(first)
        def _():
            ubuf_ref[:, 0:SUBLANES, :] = jnp.zeros((2, SUBLANES, tf), F32)

        @pl.when(jnp.logical_not(first))
        def _():
            ubuf_ref[:, 0:SUBLANES, :] = carry_ref[:, j]

    gs = []
    for c0 in range(0, tf, FFN_SLICE):
        cs = slice(c0, min(c0 + FFN_SLICE, tf))
        halves = []
        for half, (w_ref, cw_ref, cb_ref) in enumerate(((wa_ref, cwa_ref, cba_ref), (wb_ref, cwb_ref, cbb_ref))):
            u = _dot(hb, w_ref[:, cs])
            cw = cw_ref[:, cs]
            if stateful:
                s0 = (s0a_ref, s0b_ref)[half][:, cs]
                s1 = (s1a_ref, s1b_ref)[half][:, cs]
                y = cw[0:1] * s0 + cw[1:2] * s1 + cw[2:3] * u + cb_ref[:, cs]
                (oa_ref, ob_ref)[half][:, cs] = u
            else:
                ubuf_ref[half, SUBLANES:, cs] = u
                y = (cw[0:1] * ubuf_ref[half, SUBLANES - 2:SUBLANES - 2 + tm, cs]
                     + cw[1:2] * ubuf_ref[half, SUBLANES - 1:SUBLANES - 1 + tm, cs]
                     + cw[2:3] * u + cb_ref[:, cs])
            halves.append(y)
        gs.append((_gelu_tanh(halves[0]) * halves[1]).astype(BF16))
    if not stateful:
        carry_ref[:, j] = ubuf_ref[:, tm:tm + SUBLANES, :]
        oa_ref[0] = ubuf_ref[0, tm + SUBLANES - 2:tm + SUBLANES, :]
        ob_ref[0] = ubuf_ref[1, tm + SUBLANES - 2:tm + SUBLANES, :]
    acc_ref[...] += _dot(jnp.concatenate(gs, axis=1), wo_ref[...])

    @pl.when(j == nj - 1)
    def _():
        y_ref[...] = x_ref[...] + _rms(acc_ref[...], nw_out_ref[...])


def _ffn(x, nw_in, w_in_bf, conv_w, conv_b, w_out_bf, nw_out, tm, seq_rows, state=None):
    rows, d = x.shape
    tf = FFN_COLS
    nj = D_FF // tf
    stateful = state is not None
    cb2 = conv_b.reshape(1, 2 * D_FF)
    in_specs = [pl.BlockSpec((tm, d), lambda i, j: (i, 0)),
                pl.BlockSpec((1, d), lambda i, j: (0, 0)),
                pl.BlockSpec((d, tf), lambda i, j: (0, j)),
                pl.BlockSpec((d, tf), lambda i, j: (0, nj + j)),
                pl.BlockSpec((FFN_CONV, tf), lambda i, j: (0, j)),
                pl.BlockSpec((FFN_CONV, tf), lambda i, j: (0, nj + j)),
                pl.BlockSpec((1, tf), lambda i, j: (0, j)),
                pl.BlockSpec((1, tf), lambda i, j: (0, nj + j)),
                pl.BlockSpec((tf, d), lambda i, j: (j, 0)),
                pl.BlockSpec((1, d), lambda i, j: (0, 0))]
    args = [x, nw_in.reshape(1, d), w_in_bf, w_in_bf, conv_w, conv_w, cb2, cb2, w_out_bf, nw_out.reshape(1, d)]
    scratch = [pltpu.VMEM((tm, d), BF16), pltpu.VMEM((tm, d), F32)]
    if stateful:
        s0, s1 = state[:, 0, :], state[:, 1, :]
        in_specs += [pl.BlockSpec((tm, tf), lambda i, j: (i, j)), pl.BlockSpec((tm, tf), lambda i, j: (i, nj + j)),
                     pl.BlockSpec((tm, tf), lambda i, j: (i, j)), pl.BlockSpec((tm, tf), lambda i, j: (i, nj + j))]
        args += [s0, s0, s1, s1]
        out_shape = [jax.ShapeDtypeStruct((rows, d), F32), jax.ShapeDtypeStruct((rows, D_FF), F32),
                     jax.ShapeDtypeStruct((rows, D_FF), F32)]
        out_specs = [pl.BlockSpec((tm, d), lambda i, j: (i, 0)), pl.BlockSpec((tm, tf), lambda i, j: (i, j)),
                     pl.BlockSpec((tm, tf), lambda i, j: (i, j))]
        tiles_per_seq = 1
    else:
        tiles_per_seq = seq_rows // tm
        ntiles = rows // tm
        out_shape = [jax.ShapeDtypeStruct((rows, d), F32), jax.ShapeDtypeStruct((ntiles, FFN_CONV - 1, D_FF), F32),
                     jax.ShapeDtypeStruct((ntiles, FFN_CONV - 1, D_FF), F32)]
        out_specs = [pl.BlockSpec((tm, d), lambda i, j: (i, 0)),
                     pl.BlockSpec((1, FFN_CONV - 1, tf), lambda i, j: (i, 0, j)),
                     pl.BlockSpec((1, FFN_CONV - 1, tf), lambda i, j: (i, 0, j))]
        scratch += [pltpu.VMEM((2, tm + SUBLANES, tf), F32), pltpu.VMEM((2, nj, SUBLANES, tf), F32)]
    y, ua, ub = pl.pallas_call(
        functools.partial(_ffn_kernel, tm=tm, tf=tf, nj=nj, tiles_per_seq=tiles_per_seq, stateful=stateful),
        out_shape=out_shape,
        grid=(rows // tm, nj),
        in_specs=in_specs,
        out_specs=out_specs,
        scratch_shapes=scratch,
        compiler_params=_params("arbitrary", "arbitrary"),
        name="conv_ffn",
    )(*args)
    if not stateful:
        ua, ub = ua[tiles_per_seq - 1::tiles_per_seq], ub[tiles_per_seq - 1::tiles_per_seq]
    return y, jnp.concatenate([ua, ub], axis=-1)


def _scan_lanes(x, op, fill):
    n = x.shape[1]
    lane = lax.broadcasted_iota(jnp.int32, x.shape, 1)
    s = 1
    while s < n:
        x = op(x, jnp.where(lane >= s, pltpu.roll(x, s, axis=1), fill))
        s *= 2
    return x


def _mlstm_gate_scan_kernel(ig_ref, lf_ref, b_ref, m_ref, mrow_ref):
    a = _scan_lanes(lf_ref[...], jnp.add, 0.0)
    b = ig_ref[...] - a
    m = jnp.maximum(_scan_lanes(b, jnp.maximum, NEG_INF), 0.0)
    b_ref[...] = b
    m_ref[...] = m
    mrow_ref[...] = a + m


def _mlstm_gate_scan(ig_t, lf_t, seq):
    rows, total = ig_t.shape
    spec = pl.BlockSpec((rows, seq), lambda n: (0, n))
    return pl.pallas_call(
        _mlstm_gate_scan_kernel,
        out_shape=[jax.ShapeDtypeStruct((rows, total), F32)] * 3,
        grid=(total // seq,),
        in_specs=[spec, spec],
        out_specs=[spec, spec, spec],
        compiler_params=_params("arbitrary"),
        name="mlstm_gate_scan",
    )(ig_t, lf_t)


def _cumsum_kernel(x_ref, o_ref):
    o_ref[...] = _scan_lanes(x_ref[...], jnp.add, 0.0)


def _cumsum_time(x_t, seq):
    rows, total = x_t.shape
    spec = pl.BlockSpec((rows, seq), lambda n: (0, n))
    return pl.pallas_call(
        _cumsum_kernel,
        out_shape=jax.ShapeDtypeStruct((rows, total), F32),
        grid=(total // seq,),
        in_specs=[spec],
        out_specs=spec,
        compiler_params=_params("arbitrary"),
        name="logf_cumsum",
    )(x_t)


def _mlstm_prompt_kernel(qk_ref, v_ref, ob_ref, rows_ref, cols_ref, cw_ref, cb_ref, nw_ref,
                         g_ref, c_ref, n_ref, mfin_ref, xbuf_ref, mprev_ref, *, tl):
    c = pl.program_id(1)
    nc = pl.num_programs(1)

    @pl.when(c == 0)
    def _():
        xbuf_ref[0:SUBLANES, :] = jnp.zeros((SUBLANES, 2 * W_B), F32)
        mprev_ref[...] = jnp.zeros_like(mprev_ref)
        c_ref[...] = jnp.zeros_like(c_ref)
        n_ref[...] = jnp.zeros_like(n_ref)

    xbuf_ref[SUBLANES:, :] = qk_ref[...]
    cw = cw_ref[...]
    y = cb_ref[...]
    for t in range(MLSTM_CONV):
        off = SUBLANES - (MLSTM_CONV - 1) + t
        y = y + cw[t:t + 1] * xbuf_ref[off:off + tl, :]
    xbuf_ref[0:SUBLANES, :] = xbuf_ref[tl:tl + SUBLANES, :]
    qk = y * _sigmoid(y)

    cols = cols_ref[...]
    mprev = mprev_ref[...]
    t_idx = lax.broadcasted_iota(jnp.int32, (tl, tl), 0)
    s_idx = lax.broadcasted_iota(jnp.int32, (tl, tl), 1)
    causal = s_idx <= t_idx
    for h in range(H_B):
        q = qk[:, h * DH_B:(h + 1) * DH_B]
        k = qk[:, W_B + h * DH_B:W_B + (h + 1) * DH_B] * (DH_B ** -0.5)
        v = v_ref[:, h * DH_B:(h + 1) * DH_B]
        qb, kb, vb = q.astype(BF16), k.astype(BF16), v.astype(BF16)
        b_row = rows_ref[h:h + 1, :]
        b_col = cols[:, h:h + 1]
        m_col = cols[:, H_B + h:H_B + h + 1]
        mrow_col = cols[:, 2 * H_B + h:2 * H_B + h + 1]
        m_last = cols[tl - 1:tl, H_B + h:H_B + h + 1]
        m_prev = mprev[:, H_B + h:H_B + h + 1]
        dmat = jnp.exp(jnp.where(causal, b_row - m_col, NEG_INF))
        s = _dot_nt(qb, kb) * dmat
        w_inter = jnp.exp(m_prev - m_col)
        cmat = c_ref[0, h]
        nvec = n_ref[0, h:h + 1, :]
        num = _dot(s.astype(BF16), vb) + w_inter * _dot(qb, cmat.astype(BF16))
        den = jnp.sum(s, axis=1, keepdims=True) + w_inter * jnp.sum(q * nvec, axis=1, keepdims=True)
        hh = num / jnp.maximum(jnp.abs(den), jnp.exp(-mrow_col))
        w_s = jnp.exp(b_col - m_last)
        w_c = jnp.exp(m_prev - m_last)
        kw = k * w_s
        c_ref[0, h] = w_c * cmat + lax.dot_general(kw.astype(BF16), vb, (((0,), (0,)), ((), ())),
                                                    preferred_element_type=F32)
        n_ref[0, h:h + 1, :] = w_c * nvec + jnp.sum(kw, axis=0, keepdims=True)
        hn = _rms(hh, nw_ref[:, h * DH_B:(h + 1) * DH_B])
        g_ref[:, h * DH_B:(h + 1) * DH_B] = _sigmoid(ob_ref[:, h * DH_B:(h + 1) * DH_B]) * hn
    mprev_ref[...] = cols[tl - 1:tl, :]

    @pl.when(c == nc - 1)
    def _():
        mfin_ref[0] = cols[tl - 1:tl, :]


def _mlstm_prompt(qk_pre, v_pre, ob, rows_pack, cols_pack, conv_w, conv_b, norm_w, nseq, seq):
    tl = MLSTM_TILE
    nc = seq // tl
    total = nseq * seq
    rmap = lambda n, c: (n * nc + c, 0)
    return pl.pallas_call(
        functools.partial(_mlstm_prompt_kernel, tl=tl),
        out_shape=[jax.ShapeDtypeStruct((total, W_B), F32),
                   jax.ShapeDtypeStruct((nseq, H_B, DH_B, DH_B), F32),
                   jax.ShapeDtypeStruct((nseq, H_B, DH_B), F32),
                   jax.ShapeDtypeStruct((nseq, 1, LANES), F32)],
        grid=(nseq, nc),
        in_specs=[pl.BlockSpec((tl, 2 * W_B), rmap),
                  pl.BlockSpec((tl, W_B), rmap),
                  pl.BlockSpec((tl, W_B), rmap),
                  pl.BlockSpec((SUBLANES, tl), lambda n, c: (0, n * nc + c)),
                  pl.BlockSpec((tl, LANES), rmap),
                  pl.BlockSpec((MLSTM_CONV, 2 * W_B), lambda n, c: (0, 0)),
                  pl.BlockSpec((1, 2 * W_B), lambda n, c: (0, 0)),
                  pl.BlockSpec((1, W_B), lambda n, c: (0, 0))],
        out_specs=[pl.BlockSpec((tl, W_B), rmap),
                   pl.BlockSpec((1, H_B, DH_B, DH_B), lambda n, c: (n, 0, 0, 0)),
                   pl.BlockSpec((1, H_B, DH_B), lambda n, c: (n, 0, 0)),
                   pl.BlockSpec((1, 1, LANES), lambda n, c: (n, 0, 0))],
        scratch_shapes=[pltpu.VMEM((tl + SUBLANES, 2 * W_B), F32), pltpu.VMEM((1, LANES), F32)],
        compiler_params=_params("arbitrary", "arbitrary"),
        name="mlstm_prompt",
    )(qk_pre, v_pre, ob, rows_pack, cols_pack, conv_w, conv_b.reshape(1, -1), norm_w.reshape(1, -1))


HEAD_DIM = 64
LOG2_E = 1.4426950408889634


def _head_masks(shape):
    head = lax.broadcasted_iota(jnp.int32, shape, 1) // HEAD_DIM
    return [head == g for g in range(shape[1] // HEAD_DIM)]


def _masked_copies(q_bf):
    head = lax.broadcasted_iota(jnp.int32, (1, q_bf.shape[1]), 1) // HEAD_DIM
    return [q_bf * jnp.where(head == g, 1.0, 0.0).astype(BF16) for g in range(q_bf.shape[1] // HEAD_DIM)]


def _stage_kv(k_ref, vt_in_ref, kb_ref, vt_ref):
    kb_ref[...] = k_ref[...].astype(BF16)
    for j in range(k_ref.shape[0] // KEY_TILE):
        vt_ref[j] = vt_in_ref[0, :, j * KEY_TILE:(j + 1) * KEY_TILE].astype(BF16)


def _flash_group(qbs, qi, kb_ref, vt_ref, tq, adjust):
    tk = KEY_TILE
    per = tq // tk
    ng = len(qbs)
    krow = lax.broadcasted_iota(jnp.int32, (tk, ng * tq), 0)
    qcol = lax.broadcasted_iota(jnp.int32, (tk, ng * tq), 1) % tq
    q_cat = jnp.concatenate(qbs, axis=0)

    width = qbs[0].shape[1]
    chunk = min(width, MXU_DEPTH)
    heads_per_chunk = chunk // HEAD_DIM

    def logits(j, diagonal):
        start = pl.multiple_of(j * tk, tk)
        kt = kb_ref[pl.ds(start, tk), :]
        s = jnp.concatenate(
            [_dot_nt(kt[:, c * chunk:(c + 1) * chunk],
                     q_cat[c * heads_per_chunk * tq:(c + 1) * heads_per_chunk * tq, c * chunk:(c + 1) * chunk])
             for c in range(width // chunk)], axis=1)
        s = adjust(s, j, start, diagonal is not None)
        if diagonal is not None:
            s = jnp.where(krow + diagonal * tk <= qcol, s, NEG_INF)
        return s

    def softmax(m, l, s):
        m_new = jnp.maximum(m, jnp.max(s, axis=0, keepdims=True))
        alpha = jnp.exp2(m - m_new)
        p = jnp.exp2(s - m_new)
        return m_new, alpha * l + jnp.sum(p, axis=0, keepdims=True), alpha, p.astype(BF16)

    def accumulate(accs, alpha, pb, j):
        return tuple(alpha[:, g * tq:(g + 1) * tq] * accs[g]
                     + _dot(vt_ref[j, g * HEAD_DIM:(g + 1) * HEAD_DIM, :], pb[:, g * tq:(g + 1) * tq])
                     for g in range(ng))

    m = jnp.full((1, ng * tq), NEG_INF, F32)
    l = jnp.zeros((1, ng * tq), F32)
    accs = tuple(jnp.zeros((HEAD_DIM, tq), F32) for _ in range(ng))
    def trip(first, count, carry, diagonal=False):
        m, l, accs = carry
        tiles = [first + u for u in range(count)]
        ss = [logits(j, u if diagonal else None) for u, j in enumerate(tiles)]
        for j, s in zip(tiles, ss):
            m, l, alpha, pb = softmax(m, l, s)
            accs = accumulate(accs, alpha, pb, j)
        return m, l, accs

    n = qi * per
    carry = trip(n, per, (m, l, accs), diagonal=True)
    done = 0
    for count in KEY_UNROLL:
        trips = (n - done) // count
        carry = lax.fori_loop(0, trips, lambda t, c, done=done, count=count: trip(done + t * count, count, c), carry)
        done = done + trips * count
    m, l, accs = carry
    return jnp.concatenate([accs[g] / l[:, g * tq:(g + 1) * tq] for g in range(ng)], axis=0)


def _store_heads(o_ref, o_t):
    for r in range(o_t.shape[0] // LANES):
        for c in range(o_t.shape[1] // LANES):
            o_ref[c * LANES:(c + 1) * LANES, r * LANES:(r + 1) * LANES] = jnp.transpose(
                o_t[r * LANES:(r + 1) * LANES, c * LANES:(c + 1) * LANES])


def _fox_prompt_kernel(q_ref, k_ref, v_ref, crow_ref, o_ref, kb_ref, vt_ref, cb_ref, *, tq):
    qi = pl.program_id(2)

    @pl.when(qi == 0)
    def _():
        _stage_kv(k_ref, v_ref, kb_ref, vt_ref)
        for g in range(cb_ref.shape[0]):
            for j in range(k_ref.shape[0] // LANES):
                sl = slice(j * LANES, (j + 1) * LANES)
                cb_ref[g, sl, :] = jnp.transpose(
                    jnp.broadcast_to(crow_ref[0, g:g + 1, sl] * LOG2_E, (LANES, LANES)))

    q = q_ref[...] * (DH_C ** -0.5 * LOG2_E)

    def adjust(s, j, start, diagonal):
        return s - jnp.concatenate([cb_ref[g, pl.ds(start, KEY_TILE), :] for g in range(cb_ref.shape[0])
                                    for _ in range(tq // LANES)], axis=1)

    qbs = _masked_copies(q.astype(BF16))
    _store_heads(o_ref, _flash_group(qbs, qi, kb_ref, vt_ref, tq, adjust))


def _fox_prompt(q, k, v, crow, nseq, seq):
    tq = FOX_TILE
    nq = seq // tq
    ng = FOX_GROUP
    width = ng * HEAD_DIM
    return pl.pallas_call(
        functools.partial(_fox_prompt_kernel, tq=tq),
        out_shape=jax.ShapeDtypeStruct(q.shape, F32),
        grid=(nseq, H_C // ng, nq),
        in_specs=[pl.BlockSpec((tq, width), lambda n, p, i: (n * nq + i, p)),
                  pl.BlockSpec((seq, width), lambda n, p, i: (n, p)),
                  pl.BlockSpec((1, width, seq), lambda n, p, i: (n, p, 0)),
                  pl.BlockSpec((1, ng, seq), lambda n, p, i: (p, 0, n))],
        out_specs=pl.BlockSpec((tq, width), lambda n, p, i: (n * nq + i, p)),
        scratch_shapes=[pltpu.VMEM((seq, width), BF16), pltpu.VMEM((seq // KEY_TILE, width, KEY_TILE), BF16),
                        pltpu.VMEM((ng, seq, LANES), F32)],
        compiler_params=_params("arbitrary", "arbitrary", "arbitrary"),
        name="fox_prompt",
    )(q, k, v, crow)


def _rank_rows(g, n_valid):
    nb = g.shape[0]
    r = lax.broadcasted_iota(jnp.int32, g.shape, 0)
    g = jnp.where(r < n_valid, g, NEG_INF)
    rank = jnp.zeros(g.shape, jnp.int32)
    for i in range(nb):
        gi = g[i:i + 1, :]
        rank = rank + ((gi > g) | ((gi == g) & (i < r))).astype(jnp.int32)
    return rank, r


def _moba_prompt_kernel(q_ref, k_ref, v_ref, o_ref, kb_ref, vt_ref, kmean_ref, kmh_ref, kml_ref, sel_ref, *, tq, nb):
    qi = pl.program_id(2)
    ng, nbp = sel_ref.shape[0], sel_ref.shape[1]

    @pl.when(qi == 0)
    def _():
        _stage_kv(k_ref, v_ref, kb_ref, vt_ref)
        kmean_ref[...] = jnp.zeros_like(kmean_ref)
        for j in range(nb):
            kmean_ref[j:j + 1, :] = jnp.mean(k_ref[j * MOBA_BLOCK:(j + 1) * MOBA_BLOCK, :], axis=0, keepdims=True)
        kmean = kmean_ref[...]
        for g, mask in enumerate(_head_masks(kmean.shape)):
            hi, lo = _split_bf16(jnp.where(mask, kmean, 0.0))
            kmh_ref[g * nbp:(g + 1) * nbp, :] = hi
            kml_ref[g * nbp:(g + 1) * nbp, :] = lo

    q = q_ref[...]
    q_hi, q_lo = _split_bf16(q)
    km_hi, km_lo = kmh_ref[...], kml_ref[...]
    gates = _dot_nt(km_hi, q_hi) + _dot_nt(km_lo, q_hi) + _dot_nt(km_hi, q_lo)
    for g in range(ng):
        rank, r = _rank_rows(gates[g * nbp:(g + 1) * nbp, :], qi)
        sel_ref[g] = ((rank < MOBA_TOPK) & (r < qi)).astype(F32)
    qbs = _masked_copies((q * (DH_A ** -0.5 * LOG2_E)).astype(BF16))

    def adjust(s, j, start, diagonal):
        if diagonal:
            return s
        blk = (j * KEY_TILE) // MOBA_BLOCK
        chosen = jnp.concatenate([sel_ref[g, pl.ds(blk, 1), :] for g in range(sel_ref.shape[0])], axis=1)
        return jnp.where(chosen > 0.5, s, NEG_INF)

    _store_heads(o_ref, _flash_group(qbs, qi, kb_ref, vt_ref, tq, adjust))


def _moba_prompt(q, k, v, nseq, seq):
    tq = ATT_TILE
    assert tq == MOBA_BLOCK and seq % MOBA_BLOCK == 0 and seq // MOBA_BLOCK >= MOBA_TOPK
    nq = seq // tq
    nb = seq // MOBA_BLOCK
    nbp = -(-nb // SUBLANES) * SUBLANES
    ng = MOBA_GROUP
    width = ng * HEAD_DIM
    return pl.pallas_call(
        functools.partial(_moba_prompt_kernel, tq=tq, nb=nb),
        out_shape=jax.ShapeDtypeStruct(q.shape, F32),
        grid=(nseq, H_A // ng, nq),
        in_specs=[pl.BlockSpec((tq, width), lambda n, p, i: (n * nq + i, p)),
                  pl.BlockSpec((seq, width), lambda n, p, i: (n, p)),
                  pl.BlockSpec((1, width, seq), lambda n, p, i: (n, p, 0))],
        out_specs=pl.BlockSpec((tq, width), lambda n, p, i: (n * nq + i, p)),
        scratch_shapes=[pltpu.VMEM((seq, width), BF16), pltpu.VMEM((seq // KEY_TILE, width, KEY_TILE), BF16),
                        pltpu.VMEM((nbp, width), F32), pltpu.VMEM((ng * nbp, width), BF16),
                        pltpu.VMEM((ng * nbp, width), BF16), pltpu.VMEM((ng, nbp, tq), F32)],
        compiler_params=_params("arbitrary", "arbitrary", "arbitrary"),
        name="moba_prompt",
    )(q, k, v)


def _column_replicated(row):
    return jnp.concatenate(
        [jnp.transpose(jnp.broadcast_to(row[:, c * LANES:(c + 1) * LANES], (LANES, LANES)))
         for c in range(row.shape[1] // LANES)], axis=0)


def _head_sums(x, dh):
    return jnp.concatenate([jnp.sum(x[h * dh:(h + 1) * dh, :], axis=0, keepdims=True)
                            for h in range(x.shape[0] // dh)], axis=0)


def _moba_sample_select_kernel(pt_ref, q_ref, *refs, group, nblk):
    k_refs = refs[:group]
    idx_ref = refs[group]
    qcol_ref, gate_ref = refs[group + 1:]
    g = pl.program_id(1)

    @pl.when(g == 0)
    def _():
        qcol_ref[...] = _column_replicated(q_ref[0])
        gate_ref[...] = jnp.zeros_like(gate_ref)

    lane = lax.broadcasted_iota(jnp.int32, gate_ref.shape, 1)
    pages_per_block = MOBA_BLOCK // PAGE_SIZE
    qcol = qcol_ref[...]
    upd = jnp.zeros(gate_ref.shape, F32)
    for i in range(group):
        qk = _head_sums(k_refs[i][0] * qcol, DH_A)
        blk = (g * group + i) // pages_per_block
        val = jnp.sum(qk, axis=1, keepdims=True) * (1.0 / (PAGE_SIZE * pages_per_block))
        upd = upd + jnp.where(lane == blk, val, 0.0)
    gate_ref[...] += upd

    @pl.when(g == pl.num_programs(1) - 1)
    def _():
        gate = jnp.where(lane < nblk, gate_ref[...], NEG_INF)
        rank = jnp.zeros(gate.shape, jnp.int32)
        for i in range(nblk):
            gi = gate[:, i:i + 1]
            rank = rank + ((gi > gate) | ((gi == gate) & (i < lane))).astype(jnp.int32)
        out = jnp.zeros(gate.shape, F32)
        for t in range(MOBA_TOPK):
            it = jnp.sum(jnp.where((rank == t) & (lane < nblk), lane.astype(F32), 0.0), axis=1, keepdims=True)
            out = jnp.where(lane == t, it, out)
        idx_ref[0] = out.astype(jnp.int32)


def _moba_sample_select(q, pool_kt, page_table, nblk):
    nb = q.shape[0]
    npg = nblk * (MOBA_BLOCK // PAGE_SIZE)
    group = SELECT_PAGES if npg % SELECT_PAGES == 0 else PAGES_PER_STEP
    assert npg % group == 0 and MOBA_TOPK <= nblk <= LANES
    in_specs = [pl.BlockSpec((1, 1, W_A), lambda n, g, pt: (n, 0, 0))]
    for i in range(group):
        in_specs.append(pl.BlockSpec((1, W_A, PAGE_SIZE), lambda n, g, pt, i=i: (pt[n, g * group + i], 0, 0)))
    return pl.pallas_call(
        functools.partial(_moba_sample_select_kernel, group=group, nblk=nblk),
        out_shape=jax.ShapeDtypeStruct((nb, H_A, LANES), jnp.int32),
        grid_spec=pltpu.PrefetchScalarGridSpec(
            num_scalar_prefetch=1,
            grid=(nb, npg // group),
            in_specs=in_specs,
            out_specs=pl.BlockSpec((1, H_A, LANES), lambda n, g, pt: (n, 0, 0)),
            scratch_shapes=[pltpu.VMEM((W_A, LANES), F32), pltpu.VMEM((H_A, LANES), F32)]),
        compiler_params=_params("arbitrary", "arbitrary"),
        name="moba_sample_select",
    )(page_table, q.reshape(nb, 1, W_A), *([pool_kt] * group))


def _moba_sample_attend_kernel(pt_ref, idx_ref, q_ref, kn_ref, vn_ref, *refs, n_sel, n_own, qblk):
    n_pages = n_sel + n_own
    k_refs = refs[:H_A * n_pages]
    v_refs = refs[H_A * n_pages:2 * H_A * n_pages]
    o_ref = refs[2 * H_A * n_pages]
    n = pl.program_id(0)
    pages_per_block = MOBA_BLOCK // PAGE_SIZE
    for h in range(H_A):
        q = q_ref[0, h:h + 1, :] * (DH_A ** -0.5)
        qcol = _column_replicated(q)[0:DH_A]
        logits = []
        for i in range(n_pages):
            s = jnp.sum(k_refs[h * n_pages + i][0] * qcol, axis=0, keepdims=True)
            if i < n_sel:
                valid = idx_ref[n, h, i // pages_per_block] < qblk
                s = jnp.where(valid, s, NEG_INF)
            logits.append(s)
        s_new = jnp.sum(q * kn_ref[0, h:h + 1, :], axis=1, keepdims=True)
        m = s_new
        for s in logits:
            m = jnp.maximum(m, jnp.max(s, axis=1, keepdims=True))
        p_new = jnp.exp(s_new - m)
        l = p_new
        acc_t = jnp.zeros((DH_A, PAGE_SIZE), F32)
        for i, s in enumerate(logits):
            p = jnp.exp(s - m)
            l = l + jnp.sum(p, axis=1, keepdims=True)
            acc_t = acc_t + v_refs[h * n_pages + i][0] * p
        acc = jnp.sum(jnp.transpose(jnp.concatenate([acc_t, jnp.zeros((LANES - DH_A, PAGE_SIZE), F32)], axis=0)),
                      axis=0, keepdims=True)
        o_ref[0, h:h + 1, :] = (acc + p_new * vn_ref[0, h:h + 1, :]) / l


def _moba_sample_attend(q, k_new, v_new, pool_kt, pool_vt, page_table, idx):
    nb = q.shape[0]
    n_pages_total = page_table.shape[1]
    pages_per_block = MOBA_BLOCK // PAGE_SIZE
    past = n_pages_total * PAGE_SIZE
    qblk = past // MOBA_BLOCK
    n_sel = MOBA_TOPK * pages_per_block
    n_own = 1 if (past - PAGE_SIZE) // MOBA_BLOCK == qblk else 0

    def sel_map(h, i):
        def index_map(n, pt, ix):
            lpage = jnp.minimum(ix[n, h, i // pages_per_block] * pages_per_block + i % pages_per_block,
                                n_pages_total - 1)
            return (pt[n, lpage], h, 0)
        return index_map

    def own_map(h):
        return lambda n, pt, ix: (pt[n, n_pages_total - 1], h, 0)

    page_specs = []
    for h in range(H_A):
        page_specs += [pl.BlockSpec((1, DH_A, PAGE_SIZE), sel_map(h, i)) for i in range(n_sel)]
        page_specs += [pl.BlockSpec((1, DH_A, PAGE_SIZE), own_map(h))] * n_own
    tok_spec = pl.BlockSpec((1, H_A, LANES), lambda n, pt, ix: (n, 0, 0))
    tok = lambda a: _pad_lanes(a.reshape(nb, H_A, DH_A))
    n_pg = H_A * (n_sel + n_own)
    att = pl.pallas_call(
        functools.partial(_moba_sample_attend_kernel, n_sel=n_sel, n_own=n_own, qblk=qblk),
        out_shape=jax.ShapeDtypeStruct((nb, H_A, LANES), F32),
        grid_spec=pltpu.PrefetchScalarGridSpec(
            num_scalar_prefetch=2,
            grid=(nb,),
            in_specs=[tok_spec, tok_spec, tok_spec] + page_specs + page_specs,
            out_specs=tok_spec),
        compiler_params=_params("arbitrary"),
        name="moba_sample_attend",
    )(page_table, idx, tok(q), tok(k_new), tok(v_new), *([pool_kt] * n_pg), *([pool_vt] * n_pg))
    return att[:, :, :DH_A].reshape(nb, W_A)


def _mlstm_sample_kernel(qk_ref, st_ref, v_ref, ob_ref, gate_ref, c_ref, n_ref, m_ref, cw_ref, cb_ref, nw_ref,
                         g_ref, cn_ref, nn_ref, mn_ref):
    cw = cw_ref[...]
    st = st_ref[0]
    y = cb_ref[...] + cw[MLSTM_CONV - 1:MLSTM_CONV] * qk_ref[0]
    for t in range(MLSTM_CONV - 1):
        y = y + cw[t:t + 1] * st[t:t + 1]
    qk = y * _sigmoid(y)
    gates = gate_ref[0]
    m_in = m_ref[0]
    m_out = jnp.zeros((1, LANES), F32)
    lane = lax.broadcasted_iota(jnp.int32, (1, LANES), 1)
    for h in range(H_B):
        q = qk[:, h * DH_B:(h + 1) * DH_B]
        k = qk[:, W_B + h * DH_B:W_B + (h + 1) * DH_B] * (DH_B ** -0.5)
        v = v_ref[0][:, h * DH_B:(h + 1) * DH_B]
        ig = gates[:, h:h + 1]
        lf = gates[:, H_B + h:H_B + h + 1]
        m_old = m_in[:, h:h + 1]
        m_row = jnp.maximum(ig, lf + m_old)
        w_inter = jnp.exp(lf + m_old - m_row)
        w_s = jnp.exp(ig - m_row)
        s = jnp.sum(q * k, axis=1, keepdims=True) * w_s
        cmat = c_ref[0, h]
        nvec = n_ref[0, h:h + 1, :]
        qc = _dot(jnp.broadcast_to(q, (SUBLANES, DH_B)).astype(BF16), cmat.astype(BF16))[0:1]
        num = s * v + w_inter * qc
        den = s + w_inter * jnp.sum(q * nvec, axis=1, keepdims=True)
        hh = num / jnp.maximum(jnp.abs(den), jnp.exp(-m_row))
        k_col = jnp.transpose(jnp.broadcast_to(k, (DH_B, DH_B)))
        cn_ref[0, h] = w_inter * cmat + w_s * (k_col * v)
        nn_ref[0, h:h + 1, :] = w_inter * nvec + w_s * k
        m_out = jnp.where(lane == h, m_row, m_out)
        hn = _rms(hh, nw_ref[:, h * DH_B:(h + 1) * DH_B])
        g_ref[0, :, h * DH_B:(h + 1) * DH_B] = _sigmoid(ob_ref[0][:, h * DH_B:(h + 1) * DH_B]) * hn
    mn_ref[0] = m_out


def _mlstm_sample(qk_pre, conv_state, v_pre, ob, gates, c, nvec, m, conv_w, conv_b, norm_w):
    nb = qk_pre.shape[0]
    tok = lambda a: a.reshape(nb, 1, a.shape[-1])
    tspec = lambda w: pl.BlockSpec((1, 1, w), lambda n: (n, 0, 0))
    full = lambda shape: pl.BlockSpec(shape, lambda n: (0,) * len(shape))
    g, cn, nn, mn = pl.pallas_call(
        _mlstm_sample_kernel,
        out_shape=[jax.ShapeDtypeStruct((nb, 1, W_B), F32),
                   jax.ShapeDtypeStruct((nb, H_B, DH_B, DH_B), F32),
                   jax.ShapeDtypeStruct((nb, H_B, DH_B), F32),
                   jax.ShapeDtypeStruct((nb, 1, LANES), F32)],
        grid=(nb,),
        in_specs=[tspec(2 * W_B),
                  pl.BlockSpec((1, MLSTM_CONV - 1, 2 * W_B), lambda n: (n, 0, 0)),
                  tspec(W_B), tspec(W_B), tspec(LANES),
                  pl.BlockSpec((1, H_B, DH_B, DH_B), lambda n: (n, 0, 0, 0)),
                  pl.BlockSpec((1, H_B, DH_B), lambda n: (n, 0, 0)),
                  tspec(LANES),
                  full((MLSTM_CONV, 2 * W_B)), full((1, 2 * W_B)), full((1, W_B))],
        out_specs=[tspec(W_B),
                   pl.BlockSpec((1, H_B, DH_B, DH_B), lambda n: (n, 0, 0, 0)),
                   pl.BlockSpec((1, H_B, DH_B), lambda n: (n, 0, 0)),
                   tspec(LANES)],
        compiler_params=_params("arbitrary"),
        name="mlstm_sample",
    )(tok(qk_pre), conv_state, tok(v_pre), tok(ob), tok(gates), c, nvec, tok(_pad_lanes(m)),
      conv_w, conv_b.reshape(1, -1), norm_w.reshape(1, -1))
    return g.reshape(nb, W_B), cn, nn, mn[:, 0, :H_B]


def _fox_sample_kernel(pt_ref, q_ref, kn_ref, vn_ref, lfn_ref, *refs, group):
    k_refs = refs[:group]
    v_refs = refs[group:2 * group]
    lf_refs = refs[2 * group:3 * group]
    o_ref = refs[3 * group]
    qcol_ref, m_ref, l_ref, run_ref, acc_ref = refs[3 * group + 1:]
    g = pl.program_id(1)

    @pl.when(g == 0)
    def _():
        qcol_ref[...] = _column_replicated(q_ref[0] * (DH_C ** -0.5))
        m_ref[...] = jnp.full(m_ref.shape, NEG_INF, F32)
        l_ref[...] = jnp.zeros_like(l_ref)
        run_ref[...] = jnp.zeros_like(run_ref)
        acc_ref[...] = jnp.zeros_like(acc_ref)

    qcol = qcol_ref[...]
    run = run_ref[...]
    logits = []
    for i in range(group):
        qk = _head_sums(k_refs[i][0] * qcol, DH_C)
        cum = run + _scan_lanes(lf_refs[i][0], jnp.add, 0.0)
        run = cum[:, PAGE_SIZE - 1:PAGE_SIZE]
        logits.append(qk - cum)
    s = jnp.concatenate(logits, axis=1)
    m = m_ref[...]
    m_new = jnp.maximum(m, jnp.max(s, axis=1, keepdims=True))
    alpha = jnp.exp(m - m_new)
    p = jnp.exp(s - m_new)
    l = alpha * l_ref[...] + jnp.sum(p, axis=1, keepdims=True)
    m_ref[...], l_ref[...], run_ref[...] = m_new, l, run
    for h in range(H_C):
        rows = slice(h * DH_C, (h + 1) * DH_C)
        a = acc_ref[rows, :] * alpha[h:h + 1, :]
        for i in range(group):
            a = a + v_refs[i][0, rows, :] * p[h:h + 1, i * PAGE_SIZE:(i + 1) * PAGE_SIZE]
        acc_ref[rows, :] = a

    @pl.when(g == pl.num_programs(1) - 1)
    def _():
        head = lax.broadcasted_iota(jnp.int32, (H_C, W_C), 0)
        mine = (lax.broadcasted_iota(jnp.int32, (H_C, W_C), 1) // DH_C) == head
        q_bd = jnp.where(mine, jnp.broadcast_to(q_ref[0], (H_C, W_C)), 0.0) * (DH_C ** -0.5)
        hrow = lax.broadcasted_iota(jnp.int32, (H_C, LANES), 0)
        hlane = lax.broadcasted_iota(jnp.int32, (H_C, LANES), 1)
        lf_new = jnp.sum(jnp.where(hrow == hlane, jnp.broadcast_to(lfn_ref[0], (H_C, LANES)), 0.0),
                         axis=1, keepdims=True)
        s_new = jnp.sum(q_bd * kn_ref[0], axis=1, keepdims=True) - (run + lf_new)
        m_f = jnp.maximum(m_new, s_new)
        alpha_f = jnp.exp(m_new - m_f)
        p_new = jnp.exp(s_new - m_f)
        l_f = alpha_f * l + p_new
        past = jnp.concatenate([jnp.sum(jnp.transpose(acc_ref[c * LANES:(c + 1) * LANES, :]), axis=0, keepdims=True)
                                for c in range(W_C // LANES)], axis=1)
        per_lane = lambda col: jnp.sum(jnp.where(mine, col, 0.0), axis=0, keepdims=True)
        o_ref[0] = (past * per_lane(alpha_f) + per_lane(p_new) * vn_ref[0]) / per_lane(l_f)


def _fox_sample(q, k_new, v_new, lf_new, pool_k, pool_v, pool_lf_t, page_table):
    nb = q.shape[0]
    n_pages = page_table.shape[1]
    group = PAGES_PER_STEP
    assert n_pages % group == 0
    tok_spec = pl.BlockSpec((1, 1, W_C), lambda n, g, pt: (n, 0, 0))
    tok = lambda a: a.reshape(nb, 1, a.shape[-1])
    page = lambda i, shape: pl.BlockSpec(shape, lambda n, g, pt: (pt[n, g * group + i], 0, 0))
    in_specs = [tok_spec, tok_spec, tok_spec, pl.BlockSpec((1, 1, LANES), lambda n, g, pt: (n, 0, 0))]
    in_specs += [page(i, (1, W_C, PAGE_SIZE)) for i in range(group)]
    in_specs += [page(i, (1, W_C, PAGE_SIZE)) for i in range(group)]
    in_specs += [page(i, (1, H_C, PAGE_SIZE)) for i in range(group)]
    return pl.pallas_call(
        functools.partial(_fox_sample_kernel, group=group),
        out_shape=jax.ShapeDtypeStruct((nb, 1, W_C), F32),
        grid_spec=pltpu.PrefetchScalarGridSpec(
            num_scalar_prefetch=1,
            grid=(nb, n_pages // group),
            in_specs=in_specs,
            out_specs=tok_spec,
            scratch_shapes=[pltpu.VMEM((W_C, LANES), F32), pltpu.VMEM((H_C, 1), F32), pltpu.VMEM((H_C, 1), F32),
                            pltpu.VMEM((H_C, 1), F32), pltpu.VMEM((W_C, PAGE_SIZE), F32)]),
        compiler_params=_params("arbitrary", "arbitrary"),
        name="fox_sample",
    )(page_table, tok(q), tok(k_new), tok(v_new), tok(lf_new),
      *([pool_k] * group), *([pool_v] * group), *([pool_lf_t] * group)).reshape(nb, W_C)


EVEN_SEGS = ((0, W_A), (W_A, W_A), (2 * W_A, W_A), (3 * W_A, 2 * W_B), (3 * W_A + 2 * W_B, W_B),
             (3 * W_A + 3 * W_B, W_B), (3 * W_A + 4 * W_B, LANES))
ODD_SEGS = ((0, W_C), (W_C, W_C), (2 * W_C, W_C), (3 * W_C, LANES))


def _even_weights(w_in, b_i, b_f):
    main = 3 * W_A + 4 * W_B
    w_pad = jnp.concatenate([w_in[:, :main], _pad_lanes(w_in[:, main:])], axis=1)
    w_bf = w_pad.astype(BF16)
    wq = w_in[:, :W_A]
    w_lo = (wq - wq.astype(BF16).astype(F32)).astype(BF16)
    gate_bias = _pad_lanes(jnp.concatenate([b_i, b_f]).reshape(1, -1))
    return w_bf, w_lo, gate_bias


def _odd_weights(w_in, b_f):
    w_pad = jnp.concatenate([w_in[:, :3 * W_C], _pad_lanes(w_in[:, 3 * W_C:])], axis=1)
    return w_pad.astype(BF16), _pad_lanes(b_f.reshape(1, -1))


def _feature_major(pool):
    n_pool, page, heads, dh = pool.shape
    return jnp.transpose(pool, (0, 2, 3, 1)).reshape(n_pool, heads * dh, page)


def _time_minor_to_cache(a_t, heads):
    n, width, seq = a_t.shape
    return jnp.transpose(a_t.reshape(n, heads, width // heads, seq), (0, 3, 1, 2))


def _rows_to_sublanes(a):
    return jnp.pad(a.T, ((0, SUBLANES - a.shape[1]), (0, 0)))


def kernel(x_prompt, x_sample, cache_moba_k, cache_moba_v, state_mlstm_c, state_mlstm_n, state_mlstm_m,
           state_mlstm_conv, cache_fox_k, cache_fox_v, cache_fox_logf, state_ffn_conv, page_table, norm_w,
           even_w_in, mlstm_conv_w, mlstm_conv_b, mlstm_b_i, mlstm_b_f, mlstm_norm_w, even_w_out, fox_w_in,
           fox_b_f, fox_w_out, ffn_w_in, ffn_conv_w, ffn_conv_b, ffn_w_out):
    bp, sp, d = x_prompt.shape
    bs, ss, _ = x_sample.shape
    assert ss == 1 and d == D_MODEL
    depth = norm_w.shape[0]
    n_pages = page_table.shape[1]
    tp = bp * sp
    yp = x_prompt.reshape(tp, d)
    ys = x_sample.reshape(bs, d)
    outs = {name: [] for name in (
        "mk_p", "mv_p", "mk_s", "mv_s", "mc_p", "mn_p", "mm_p", "mconv_p", "mc_s", "mn_s", "mm_s", "mconv_s",
        "fk_p", "fv_p", "fl_p", "fk_s", "fv_s", "fl_s", "ff_p", "ff_s")}
    for layer in range(depth):
        nw = norm_w[layer]
        if layer % 2 == 0:
            e = layer // 2
            w_bf, w_lo, gate_bias = _even_weights(even_w_in[e], mlstm_b_i[e], mlstm_b_f[e])
            w_out_bf = even_w_out[e].astype(BF16)
            proj = lambda x, tm, **kw: _rms_proj(x, nw[0], w_bf, w_lo, gate_bias, EVEN_SEGS, 1, (H_B, 2 * H_B), tm,
                                                 **kw)
            qa, ka, ka_t, va_t, qkb, vb, ob, gates = proj(yp, ROW_TILE, cols={1: True, 2: False}, seq_rows=sp)
            att = _moba_prompt(qa, ka, va_t, bp, sp)
            b_t, m_t, mrow_t = _mlstm_gate_scan(_rows_to_sublanes(gates[:, :H_B]),
                                                _rows_to_sublanes(gates[:, H_B:2 * H_B]), sp)
            cols = _pad_lanes(jnp.concatenate([b_t[:H_B].T, m_t[:H_B].T, mrow_t[:H_B].T], axis=1))
            gated, cc, nn, mfin = _mlstm_prompt(qkb, vb, ob, b_t, cols, mlstm_conv_w[e], mlstm_conv_b[e],
                                                mlstm_norm_w[e], bp, sp)
            yp = _proj_res([att, gated], [w_out_bf[:W_A], w_out_bf[W_A:]], yp, nw[1], FFN_ROWS)
            outs["mk_p"].append(_time_minor_to_cache(ka_t, H_A))
            outs["mv_p"].append(_time_minor_to_cache(va_t, H_A))
            outs["mc_p"].append(cc)
            outs["mn_p"].append(nn)
            outs["mm_p"].append(mfin[:, 0, 2 * H_B:3 * H_B])
            outs["mconv_p"].append(qkb.reshape(bp, sp, 2 * W_B)[:, sp - (MLSTM_CONV - 1):])
            qa, ka, va, qkb, vb, ob, gates = proj(ys, bs)
            pool_kt = _feature_major(cache_moba_k[e])
            pool_vt = _feature_major(cache_moba_v[e])
            nblk = (n_pages * PAGE_SIZE) // MOBA_BLOCK
            idx = _moba_sample_select(qa, pool_kt, page_table, nblk)[:, :, :MOBA_TOPK]
            att = _moba_sample_attend(qa, ka, va, pool_kt, pool_vt, page_table, idx)
            gated, cc, nn, mm = _mlstm_sample(qkb, state_mlstm_conv[e], vb, ob, gates, state_mlstm_c[e],
                                              state_mlstm_n[e], state_mlstm_m[e], mlstm_conv_w[e],
                                              mlstm_conv_b[e], mlstm_norm_w[e])
            ys = _proj_res([att, gated], [w_out_bf[:W_A], w_out_bf[W_A:]], ys, nw[1], bs)
            outs["mk_s"].append(ka.reshape(bs, ss, H_A, DH_A))
            outs["mv_s"].append(va.reshape(bs, ss, H_A, DH_A))
            outs["mc_s"].append(cc)
            outs["mn_s"].append(nn)
            outs["mm_s"].append(mm)
            outs["mconv_s"].append(jnp.concatenate([state_mlstm_conv[e][:, 1:], qkb[:, None, :]], axis=1))
        else:
            o = layer // 2
            w_bf, gate_bias = _odd_weights(fox_w_in[o], fox_b_f[o])
            w_out_bf = fox_w_out[o].astype(BF16)
            proj = lambda x, tm, **kw: _rms_proj(x, nw[0], w_bf, None, gate_bias, ODD_SEGS, 0, (0, H_C), tm, **kw)
            q, k, k_t, v_t, lf = proj(yp, ROW_TILE, cols={1: True, 2: False}, seq_rows=sp)
            lf_t = lf[:, :H_C].T
            crow = _cumsum_time(lf_t, sp).reshape(H_C // FOX_GROUP, FOX_GROUP, tp)
            att = _fox_prompt(q, k, v_t, crow, bp, sp)
            yp = _proj_res([att], [w_out_bf], yp, nw[1], FFN_ROWS)
            outs["fk_p"].append(_time_minor_to_cache(k_t, H_C))
            outs["fv_p"].append(_time_minor_to_cache(v_t, H_C))
            outs["fl_p"].append(lf[:, :H_C].reshape(bp, sp, H_C))
            q, k, v, lf = proj(ys, bs)
            pool_lf_t = jnp.transpose(cache_fox_logf[o], (0, 2, 1))
            att = _fox_sample(q, k, v, lf, _feature_major(cache_fox_k[o]), _feature_major(cache_fox_v[o]),
                              pool_lf_t, page_table)
            ys = _proj_res([att], [w_out_bf], ys, nw[1], bs)
            outs["fk_s"].append(k.reshape(bs, ss, H_C, DH_C))
            outs["fv_s"].append(v.reshape(bs, ss, H_C, DH_C))
            outs["fl_s"].append(lf[:, :H_C].reshape(bs, ss, H_C))
        w_in_bf = ffn_w_in[layer].astype(BF16)
        w_o_bf = ffn_w_out[layer].astype(BF16)
        yp, buf_p = _ffn(yp, nw[2], w_in_bf, ffn_conv_w[layer], ffn_conv_b[layer], w_o_bf, nw[3], FFN_ROWS, sp)
        ys, u_s = _ffn(ys, nw[2], w_in_bf, ffn_conv_w[layer], ffn_conv_b[layer], w_o_bf, nw[3], bs, 1,
                       state=state_ffn_conv[layer])
        outs["ff_p"].append(buf_p)
        outs["ff_s"].append(jnp.concatenate([state_ffn_conv[layer][:, 1:], u_s[:, None, :]], axis=1))
    st = {name: jnp.stack(vals) for name, vals in outs.items()}
    return (yp.reshape(bp, sp, d), ys.reshape(bs, ss, d), st["mk_p"], st["mv_p"], st["mk_s"], st["mv_s"],
            st["mc_p"], st["mn_p"], st["mm_p"], st["mconv_p"], st["mc_s"], st["mn_s"], st["mm_s"], st["mconv_s"],
            st["fk_p"], st["fv_p"], st["fl_p"], st["fk_s"], st["fv_s"], st["fl_s"], st["ff_p"], st["ff_s"])
```
